```python
import math
import jax, jax.numpy as jnp
from jax import lax
import numpy as np

D_MODEL = 2048
BATCH = 4
SEQ = 2048
DEPTH = 2
DEC_BATCH = 128
DEC_SEQ = 4
PAST_LEN = 2048
PAGE_SIZE = 128

C_A = D_MODEL // 4
C_B = D_MODEL // 2
C_C = D_MODEL - C_A - C_B
HEAD_DIM = 64
N_HEADS_B = C_B // HEAD_DIM
DIL_PATTERNS = ((128, 1), (512, 4), (2048, 16))
MAX_WINDOW = 2048
ATTN_BLOCK = 128
ATTN_SCALE = HEAD_DIM ** -0.5
CONV_WIDTH = 31
SSM_GROUP = 16
SSM_STATE = 64
N_SSM_GROUPS = C_C // SSM_GROUP
IN_SPLITS = (2 * C_A, 2 * C_A + C_B, 2 * C_A + 2 * C_B, 2 * C_A + 3 * C_B)
IN_COLS = 2 * C_A + 3 * C_B + C_C
D_MIX = C_A + C_B + C_C
D_FF_DENSE = ((8 * D_MODEL // 3 + 127) // 128) * 128
N_EXPERTS = 8
TOP_K = 2
D_FF_EXPERT = 7 * D_MODEL // 2
N_DENSE = (DEPTH + 1) // 2
N_MOE = DEPTH // 2
EPS = 1e-6
F32 = jnp.float32

kernel_name = 'hybrid_conv_dilattn_s5_decoder_step'


def rmsnorm(x, g):
    xf = x.astype(F32)
    y = xf * lax.rsqrt(jnp.mean(xf * xf, axis=-1, keepdims=True) + EPS)
    return (y * g.astype(F32)).astype(x.dtype)


def layernorm(x, g, b):
    xf = x.astype(F32)
    xc = xf - jnp.mean(xf, axis=-1, keepdims=True)
    var = jnp.mean(xc * xc, axis=-1, keepdims=True)
    return (xc * lax.rsqrt(var + EPS) * g.astype(F32) + b.astype(F32)).astype(x.dtype)


def conv_module(a_in, conv_past, w_dw, b_dw, ln_g, ln_b, w_pw):
    val, gate = jnp.split(a_in, 2, axis=-1)
    g = val * jax.nn.sigmoid(gate)
    if conv_past is None:
        conv_past = jnp.zeros((g.shape[0], CONV_WIDTH - 1, C_A), g.dtype)
    xp = jnp.concatenate([conv_past.astype(g.dtype), g], axis=1)
    y = lax.conv_general_dilated(xp, w_dw[:, None, :].astype(g.dtype), window_strides=(1,), padding='VALID',
                                 dimension_numbers=('NWC', 'WIO', 'NWC'), feature_group_count=C_A)
    y = jax.nn.silu(layernorm(y + b_dw, ln_g, ln_b))
    return y @ w_pw, xp[:, -(CONV_WIDTH - 1):]


def dilated_branch_banded(q, k, v, dil, steps):
    B_, S, H, Dh = q.shape
    n = S // dil
    nb = -(-n // ATTN_BLOCK)
    n_pad = nb * ATTN_BLOCK

    def sub(x):
        x = x.reshape(B_, n, dil, H, Dh).transpose(0, 2, 1, 3, 4).reshape(B_ * dil, n, H, Dh)
        return jnp.pad(x, ((0, 0), (0, n_pad - n), (0, 0), (0, 0)))

    def band(x):
        xb = jnp.pad(x, ((0, 0), (ATTN_BLOCK, 0), (0, 0), (0, 0))).reshape(-1, nb + 1, ATTN_BLOCK, H, Dh)
        return jnp.concatenate([xb[:, :-1], xb[:, 1:]], axis=2)

    qb = sub(q).reshape(-1, nb, ATTN_BLOCK, H, Dh)
    kb, vb = band(sub(k)), band(sub(v))
    s = jnp.einsum('bnqhd,bnkhd->bnqhk', qb.astype(F32), kb.astype(F32)) * ATTN_SCALE
    qi = jnp.arange(ATTN_BLOCK)[:, None]
    ki = jnp.arange(2 * ATTN_BLOCK)[None, :]
    dist = ATTN_BLOCK + qi - ki
    key_pos = (jnp.arange(nb) * ATTN_BLOCK)[:, None, None] - ATTN_BLOCK + ki[None]
    valid = (dist >= 0) & (dist <= steps) & (key_pos >= 0)
    s = jnp.where(valid[None, :, :, None, :], s, -jnp.inf)
    m = jnp.max(s, axis=-1)
    p = jnp.exp(s - m[..., None])
    den = jnp.sum(p, axis=-1)
    o = jnp.einsum('bnqhk,bnkhd->bnqhd', p, vb.astype(F32)) / den[..., None]

    def unsub(x):
        x = x.reshape((B_ * dil, n_pad) + x.shape[3:])[:, :n]
        x = jnp.moveaxis(x.reshape((B_, dil, n) + x.shape[2:]), 1, 2)
        return x.reshape((B_, S) + x.shape[3:])

    return unsub(m), unsub(den), unsub(o)


def dilated_branch_gather(q, k_all, v_all, dil, steps):
    S = q.shape[1]
    L = k_all.shape[1]
    q_idx = (L - S) + jnp.arange(S)
    key_idx = q_idx[:, None] - dil * jnp.arange(steps + 1)[None, :]
    valid = key_idx >= 0
    idx = jnp.maximum(key_idx, 0)
    kg = k_all[:, idx].astype(F32)
    vg = v_all[:, idx].astype(F32)
    s = jnp.einsum('bshd,bsjhd->bshj', q.astype(F32), kg) * ATTN_SCALE
    s = jnp.where(valid[None, :, None, :], s, -jnp.inf)
    m = jnp.max(s, axis=-1)
    p = jnp.exp(s - m[..., None])
    den = jnp.sum(p, axis=-1)
    o = jnp.einsum('bshj,bsjhd->bshd', p, vg) / den[..., None]
    return m, den, o


def mix_dilations(branches):
    m = jnp.stack([b[0] for b in branches])
    den = jnp.stack([b[1] for b in branches])
    o = jnp.stack([b[2] for b in branches])
    w = den * jnp.exp(m - jnp.max(m, axis=0))
    return jnp.einsum('ibsh,ibshd->bshd', w, o) / jnp.sum(w, axis=0)[..., None]


def _complex_affine_combine(e1, e2):
    ar1, ai1, br1, bi1 = e1
    ar2, ai2, br2, bi2 = e2
    return (ar1 * ar2 - ai1 * ai2, ar1 * ai2 + ai1 * ar2,
            ar2 * br1 - ai2 * bi1 + br2, ar2 * bi1 + ai2 * br1 + bi2)


def s5_mixer(c_in, h_re0, h_im0, p):
    Bn, L, _ = c_in.shape
    u = c_in.astype(F32).reshape(Bn, L, N_SSM_GROUPS, SSM_GROUP)
    a_re = p['c_a_re'].astype(F32)
    a_im = p['c_a_im'].astype(F32)
    dt = jnp.exp(p['c_log_dt'].astype(F32))[:, None]
    mag = jnp.exp(a_re * dt)
    ab_re, ab_im = mag * jnp.cos(a_im * dt), mag * jnp.sin(a_im * dt)
    nr, ni = ab_re - 1.0, ab_im
    inv = 1.0 / (a_re * a_re + a_im * a_im)
    f_re = (nr * a_re + ni * a_im) * inv
    f_im = (ni * a_re - nr * a_im) * inv
    b_re, b_im = p['c_b_re'].astype(F32), p['c_b_im'].astype(F32)
    bb_re = f_re[..., None] * b_re - f_im[..., None] * b_im
    bb_im = f_re[..., None] * b_im + f_im[..., None] * b_re
    bu_re = jnp.einsum('blgc,gnc->blgn', u, bb_re)
    bu_im = jnp.einsum('blgc,gnc->blgn', u, bb_im)
    if h_re0 is not None:
        h_re0, h_im0 = h_re0.astype(F32), h_im0.astype(F32)
        bu_re = bu_re.at[:, 0].add(ab_re * h_re0 - ab_im * h_im0)
        bu_im = bu_im.at[:, 0].add(ab_re * h_im0 + ab_im * h_re0)
    shp = bu_re.shape
    _, _, x_re, x_im = lax.associative_scan(
        _complex_affine_combine,
        (jnp.broadcast_to(ab_re, shp), jnp.broadcast_to(ab_im, shp), bu_re, bu_im), axis=1)
    y = (jnp.einsum('gcn,blgn->blgc', p['c_c_re'].astype(F32), x_re)
         - jnp.einsum('gcn,blgn->blgc', p['c_c_im'].astype(F32), x_im)
         + p['c_d'].astype(F32).reshape(N_SSM_GROUPS, SSM_GROUP) * u)
    z = jax.nn.gelu(y.reshape(Bn, L, C_C))
    out = z * jax.nn.sigmoid(z @ p['c_w_glu'].astype(F32) + p['c_b_glu'].astype(F32))
    return out.astype(c_in.dtype), x_re[:, -1], x_im[:, -1]


def token_mixers(u, p, past):
    Bn, L, _ = u.shape
    proj = u @ p['w_in']
    a_in, q, k, v, c_in = jnp.split(proj, IN_SPLITS, axis=-1)
    q = rmsnorm(q.reshape(Bn, L, N_HEADS_B, HEAD_DIM), p['b_q_g'])
    k = rmsnorm(k.reshape(Bn, L, N_HEADS_B, HEAD_DIM), p['b_k_g'])
    v = v.reshape(Bn, L, N_HEADS_B, HEAD_DIM)
    if past is None:
        conv_past, h_re0, h_im0 = None, None, None
        branches = [dilated_branch_banded(q, k, v, dil, win // dil) for win, dil in DIL_PATTERNS]
        keep = min(MAX_WINDOW, L)
        k_rows, v_rows = k[:, L - keep:], v[:, L - keep:]
    else:
        k_buf, v_buf, conv_past, h_re0, h_im0 = past
        k_all = jnp.concatenate([k_buf.astype(k.dtype), k], axis=1)
        v_all = jnp.concatenate([v_buf.astype(v.dtype), v], axis=1)
        branches = [dilated_branch_gather(q, k_all, v_all, dil, win // dil) for win, dil in DIL_PATTERNS]
        k_rows, v_rows = k, v
    y_a, conv_new = conv_module(a_in, conv_past, p['a_w_dw'], p['a_b_dw'], p['a_ln_g'], p['a_ln_b'], p['a_w_pw'])
    y_b = mix_dilations(branches).reshape(Bn, L, C_B).astype(u.dtype)
    y_c, h_re, h_im = s5_mixer(c_in, h_re0, h_im0, p)
    y = jnp.concatenate([y_a.astype(u.dtype), y_b, y_c], axis=-1) @ p['w_out']
    return y, (k_rows, v_rows, conv_new, h_re, h_im)


def swiglu(u, w1, w3, w2):
    return (jax.nn.silu(u @ w1) * (u @ w3)) @ w2


def moe_swiglu(u, w_router, b_router, w1, w3, w2):
    shp = u.shape
    t = u.reshape(-1, shp[-1])
    logits = t.astype(F32) @ w_router.astype(F32) + b_router.astype(F32)
    top_v, top_i = lax.top_k(logits, TOP_K)
    gates = jax.nn.softmax(top_v, axis=-1)
    gate_dense = jnp.einsum('tk,tke->te', gates, jax.nn.one_hot(top_i, N_EXPERTS, dtype=F32))
    out = jnp.zeros(t.shape, F32)
    for e in range(N_EXPERTS):
        out = out + gate_dense[:, e:e + 1] * swiglu(t, w1[e], w3[e], w2[e]).astype(F32)
    return out.astype(u.dtype).reshape(shp)


def setup_inputs(seed: int = 0) -> dict:
    key = jax.random.key(seed)
    ks = iter(jax.random.split(key, 40))

    def nrm(shape, scale):
        return jax.random.normal(next(ks), shape, F32) * scale

    wb = min(MAX_WINDOW, PAST_LEN)
    d = {}
    d['x_prompt'] = nrm((BATCH, SEQ, D_MODEL), 1.0)
    d['x_sample'] = nrm((DEC_BATCH, DEC_SEQ, D_MODEL), 1.0)
    d['cache_k'] = nrm((DEPTH, DEC_BATCH, wb, N_HEADS_B, HEAD_DIM), 1.0)
    d['cache_v'] = nrm((DEPTH, DEC_BATCH, wb, N_HEADS_B, HEAD_DIM), 1.0)
    d['state_conv'] = nrm((DEPTH, DEC_BATCH, CONV_WIDTH - 1, C_A), 0.5)
    d['state_ssm_re'] = nrm((DEPTH, DEC_BATCH, N_SSM_GROUPS, SSM_STATE), 0.3)
    d['state_ssm_im'] = nrm((DEPTH, DEC_BATCH, N_SSM_GROUPS, SSM_STATE), 0.3)
    d['norm1_g'] = 1.0 + nrm((DEPTH, D_MODEL), 0.02)
    d['w_in'] = nrm((DEPTH, D_MODEL, IN_COLS), D_MODEL ** -0.5)
    d['w_out'] = nrm((DEPTH, D_MIX, D_MODEL), D_MIX ** -0.5)
    d['a_w_dw'] = nrm((DEPTH, CONV_WIDTH, C_A), CONV_WIDTH ** -0.5)
    d['a_b_dw'] = nrm((DEPTH, C_A), 0.02)
    d['a_ln_g'] = 1.0 + nrm((DEPTH, C_A), 0.02)
    d['a_ln_b'] = nrm((DEPTH, C_A), 0.02)
    d['a_w_pw'] = nrm((DEPTH, C_A, C_A), C_A ** -0.5)
    d['b_q_g'] = 1.0 + nrm((DEPTH, HEAD_DIM), 0.02)
    d['b_k_g'] = 1.0 + nrm((DEPTH, HEAD_DIM), 0.02)
    d['c_a_re'] = -0.5 + nrm((DEPTH, N_SSM_GROUPS, SSM_STATE), 0.01)
    d['c_a_im'] = jnp.pi * jnp.arange(SSM_STATE, dtype=F32) + nrm((DEPTH, N_SSM_GROUPS, SSM_STATE), 0.01)
    d['c_log_dt'] = jax.random.uniform(next(ks), (DEPTH, N_SSM_GROUPS), F32, math.log(1e-3), math.log(1e-1))
    d['c_b_re'] = nrm((DEPTH, N_SSM_GROUPS, SSM_STATE, SSM_GROUP), (2 * SSM_GROUP) ** -0.5)
    d['c_b_im'] = nrm((DEPTH, N_SSM_GROUPS, SSM_STATE, SSM_GROUP), (2 * SSM_GROUP) ** -0.5)
    d['c_c_re'] = nrm((DEPTH, N_SSM_GROUPS, SSM_GROUP, SSM_STATE), (2 * SSM_STATE) ** -0.5)
    d['c_c_im'] = nrm((DEPTH, N_SSM_GROUPS, SSM_GROUP, SSM_STATE), (2 * SSM_STATE) ** -0.5)
    d['c_d'] = nrm((DEPTH, C_C), 1.0)
    d['c_w_glu'] = nrm((DEPTH, C_C, C_C), C_C ** -0.5)
    d['c_b_glu'] = nrm((DEPTH, C_C), 0.02)
    d['norm2_g'] = 1.0 + nrm((DEPTH, D_MODEL), 0.02)
    d['ffn_w1'] = nrm((N_DENSE, D_MODEL, D_FF_DENSE), D_MODEL ** -0.5)
    d['ffn_w3'] = nrm((N_DENSE, D_MODEL, D_FF_DENSE), D_MODEL ** -0.5)
    d['ffn_w2'] = nrm((N_DENSE, D_FF_DENSE, D_MODEL), D_FF_DENSE ** -0.5)
    d['moe_w_router'] = nrm((N_MOE, D_MODEL, N_EXPERTS), D_MODEL ** -0.5)
    d['moe_b_router'] = nrm((N_MOE, N_EXPERTS), 0.01)
    d['moe_w1'] = nrm((N_MOE, N_EXPERTS, D_MODEL, D_FF_EXPERT), D_MODEL ** -0.5)
    d['moe_w3'] = nrm((N_MOE, N_EXPERTS, D_MODEL, D_FF_EXPERT), D_MODEL ** -0.5)
    d['moe_w2'] = nrm((N_MOE, N_EXPERTS, D_FF_EXPERT, D_MODEL), D_FF_EXPERT ** -0.5)
    return d


def reference(x_prompt, x_sample, cache_k, cache_v, state_conv, state_ssm_re, state_ssm_im,
              norm1_g, w_in, w_out, a_w_dw, a_b_dw, a_ln_g, a_ln_b, a_w_pw, b_q_g, b_k_g,
              c_a_re, c_a_im, c_log_dt, c_b_re, c_b_im, c_c_re, c_c_im, c_d, c_w_glu, c_b_glu,
              norm2_g, ffn_w1, ffn_w3, ffn_w2, moe_w_router, moe_b_router, moe_w1, moe_w3, moe_w2):
    y_p, y_s = x_prompt, x_sample
    st_p, st_s = [], []
    for l in range(DEPTH):
        p = {'w_in': w_in[l], 'w_out': w_out[l], 'a_w_dw': a_w_dw[l], 'a_b_dw': a_b_dw[l],
             'a_ln_g': a_ln_g[l], 'a_ln_b': a_ln_b[l], 'a_w_pw': a_w_pw[l], 'b_q_g': b_q_g[l], 'b_k_g': b_k_g[l],
             'c_a_re': c_a_re[l], 'c_a_im': c_a_im[l], 'c_log_dt': c_log_dt[l], 'c_b_re': c_b_re[l],
             'c_b_im': c_b_im[l], 'c_c_re': c_c_re[l], 'c_c_im': c_c_im[l], 'c_d': c_d[l],
             'c_w_glu': c_w_glu[l], 'c_b_glu': c_b_glu[l]}
        past = (cache_k[l], cache_v[l], state_conv[l], state_ssm_re[l], state_ssm_im[l])
        m_p, s_p = token_mixers(rmsnorm(y_p, norm1_g[l]), p, None)
        m_s, s_s = token_mixers(rmsnorm(y_s, norm1_g[l]), p, past)
        h_p, h_s = y_p + m_p, y_s + m_s
        j = l // 2
        if l % 2 == 0:
            y_p = h_p + swiglu(rmsnorm(h_p, norm2_g[l]), ffn_w1[j], ffn_w3[j], ffn_w2[j])
            y_s = h_s + swiglu(rmsnorm(h_s, norm2_g[l]), ffn_w1[j], ffn_w3[j], ffn_w2[j])
        else:
            y_p = h_p + moe_swiglu(rmsnorm(h_p, norm2_g[l]), moe_w_router[j], moe_b_router[j], moe_w1[j], moe_w3[j], moe_w2[j])
            y_s = h_s + moe_swiglu(rmsnorm(h_s, norm2_g[l]), moe_w_router[j], moe_b_router[j], moe_w1[j], moe_w3[j], moe_w2[j])
        st_p.append(s_p)
        st_s.append(s_s)
    k_p, v_p, conv_p, re_p, im_p = [jnp.stack([s[i] for s in st_p]) for i in range(5)]
    k_s, v_s, conv_s, re_s, im_s = [jnp.stack([s[i] for s in st_s]) for i in range(5)]
    return (y_p, y_s, k_p, v_p, conv_p, re_p, im_p, k_s, v_s, conv_s, re_s, im_s)
```

```python
import functools
import math

import numpy as np
import jax
import jax.numpy as jnp
from jax import lax
from jax.experimental import pallas as pl
from jax.experimental.pallas import tpu as pltpu

F32 = jnp.float32
BF16 = jnp.bfloat16
EPS = 1e-6
NEG = -1e30

HEAD_DIM = 64
ATTN_SCALE = HEAD_DIM ** -0.5
ATTN_BLOCK = 128
DILATIONS = (1, 4, 16)
DIL_STEPS = 128
CONV_WIDTH = 31
CONV_PAST = CONV_WIDTH - 1
SSM_GROUP = 16
SSM_STATE = 64
N_EXPERTS = 8
TOP_K = 2
LANES = 128
SUBLANES = 8
VMEM_LIMIT = 56 * 1024 * 1024


def _cparams(sem, vmem=VMEM_LIMIT):
    return pltpu.CompilerParams(dimension_semantics=sem, vmem_limit_bytes=vmem)


def _nt_dot(a, b):
    return lax.dot_general(a, b, (((1,), (1,)), ((), ())), preferred_element_type=F32)


def _sigmoid(x):
    return 1.0 / (1.0 + jnp.exp(-x))


def _silu(x):
    return x * _sigmoid(x)


def _gelu_tanh(x):
    c = math.sqrt(2.0 / math.pi)
    return 0.5 * x * (1.0 + jnp.tanh(c * (x + 0.044715 * (x * x * x))))


def _norm_mm_kernel(x_ref, g_ref, w_ref, o_ref, u_scr):
    @pl.when(pl.program_id(1) == 0)
    def _():
        x = x_ref[...]
        ms = jnp.mean(x * x, axis=-1, keepdims=True)
        u_scr[...] = (x * lax.rsqrt(ms + EPS) * g_ref[...]).astype(BF16)

    o_ref[...] = jnp.dot(u_scr[...], w_ref[...].astype(BF16), preferred_element_type=F32)


def norm_matmul(x, g, w_stack, layer, tm, tn):
    T, D = x.shape
    N = w_stack.shape[-1]
    return pl.pallas_call(
        _norm_mm_kernel,
        grid=(T // tm, N // tn),
        in_specs=[
            pl.BlockSpec((tm, D), lambda i, j: (i, 0)),
            pl.BlockSpec((1, D), lambda i, j: (0, 0)),
            pl.BlockSpec((None, D, tn), lambda i, j: (layer, 0, j)),
        ],
        out_specs=pl.BlockSpec((tm, tn), lambda i, j: (i, j)),
        out_shape=jax.ShapeDtypeStruct((T, N), F32),
        scratch_shapes=[pltpu.VMEM((tm, D), BF16)],
        compiler_params=_cparams(("parallel", "arbitrary")),
        name="norm_matmul",
    )(x, g.reshape(1, D), w_stack)


def _qknorm_kernel(q_ref, k_ref, gq_ref, gk_ref, qo_ref, ko_ref):
    tm = q_ref.shape[0]
    lane = lax.broadcasted_iota(jnp.int32, (tm, LANES), 1)
    head0 = lane < HEAD_DIM

    def norm(x, g):
        sq = x * x
        s0 = jnp.sum(jnp.where(head0, sq, 0.0), axis=-1, keepdims=True)
        s1 = jnp.sum(jnp.where(head0, 0.0, sq), axis=-1, keepdims=True)
        ms = jnp.where(head0, s0, s1) * (1.0 / HEAD_DIM)
        return x * lax.rsqrt(ms + EPS) * g

    for t in range(q_ref.shape[1] // LANES):
        sl = slice(t * LANES, (t + 1) * LANES)
        qo_ref[:, sl] = norm(q_ref[:, sl], gq_ref[...]) * ATTN_SCALE
        ko_ref[:, sl] = norm(k_ref[:, sl], gk_ref[...])


def qk_norm(proj, gq, gk, c_b, tm):
    T = proj.shape[0]
    gq2 = jnp.concatenate([gq, gq]).reshape(1, LANES)
    gk2 = jnp.concatenate([gk, gk]).reshape(1, LANES)
    qblk = 1
    return pl.pallas_call(
        _qknorm_kernel,
        grid=(T // tm,),
        in_specs=[
            pl.BlockSpec((tm, c_b), lambda i: (i, qblk)),
            pl.BlockSpec((tm, c_b), lambda i: (i, qblk + 1)),
            pl.BlockSpec((1, LANES), lambda i: (0, 0)),
            pl.BlockSpec((1, LANES), lambda i: (0, 0)),
        ],
        out_specs=[pl.BlockSpec((tm, c_b), lambda i: (i, 0)),
                   pl.BlockSpec((tm, c_b), lambda i: (i, 0))],
        out_shape=[jax.ShapeDtypeStruct((T, c_b), F32)] * 2,
        compiler_params=_cparams(("parallel",)),
        name="qk_norm",
    )(proj, proj, gq2, gk2)


def _attn_prompt_kernel(q_ref, k_ref, v_ref, o_ref,
                        q0_s, q1_s, k_s, v0_s, v1_s,
                        m1, l1, a1, m4, l4, a4, m16, l16, a16):
    L = q_ref.shape[0]
    B = ATTN_BLOCK
    lane = lax.broadcasted_iota(jnp.int32, (B, LANES), 1)
    row = lax.broadcasted_iota(jnp.int32, (B, LANES), 0)
    head0 = lane < HEAD_DIM
    tri_cur = lane <= row
    tri_prev = lane >= row

    stats = {1: (m1, l1, a1), 4: (m4, l4, a4), 16: (m16, l16, a16)}

    for d in DILATIONS:
        n = L // d
        nb = n // B
        m_s, l_s, a_s = stats[d]
        for r in range(d):
            src = pl.ds(r, n, stride=d) if d > 1 else slice(None)
            dst = slice(r * n, (r + 1) * n)
            hm = lax.broadcasted_iota(jnp.int32, (n, LANES), 1) < HEAD_DIM
            q = q_ref[src, :]
            v = v_ref[src, :]
            q0_s[dst, :] = jnp.where(hm, q, 0.0).astype(BF16)
            q1_s[dst, :] = jnp.where(hm, 0.0, q).astype(BF16)
            k_s[dst, :] = k_ref[src, :].astype(BF16)
            v0_s[dst, :] = jnp.where(hm, v, 1.0).astype(BF16)
            v1_s[dst, :] = jnp.where(hm, 1.0, v).astype(BF16)

        def block(j, carry, nb=nb, m_s=m_s, l_s=l_s, a_s=a_s):
            cur = pl.ds(pl.multiple_of(j * B, B), B)
            kc = k_s[cur, :]
            if nb > 1:
                prev = pl.ds(pl.multiple_of(jnp.maximum(j - 1, 0) * B, B), B)
                kp = k_s[prev, :]
                has_prev = (j % nb) > 0
                mask_prev = jnp.logical_and(tri_prev, has_prev)
            pvs, ms = [], []
            for q_s, v_s in ((q0_s, v0_s), (q1_s, v1_s)):
                qh = q_s[cur, :]
                s_c = jnp.where(tri_cur, _nt_dot(qh, kc), NEG)
                m = jnp.max(s_c, axis=-1, keepdims=True)
                if nb > 1:
                    s_p = jnp.where(mask_prev, _nt_dot(qh, kp), NEG)
                    m = jnp.maximum(m, jnp.max(s_p, axis=-1, keepdims=True))
                pv = jnp.dot(jnp.exp(s_c - m).astype(BF16), v_s[cur, :],
                             preferred_element_type=F32)
                if nb > 1:
                    pv = pv + jnp.dot(jnp.exp(s_p - m).astype(BF16), v_s[prev, :],
                                      preferred_element_type=F32)
                pvs.append(pv)
                ms.append(m)
            m_s[cur, :] = jnp.where(head0, ms[0], ms[1])
            a_s[cur, :] = jnp.where(head0, pvs[0], pvs[1])
            l_s[cur, :] = pltpu.roll(jnp.where(head0, pvs[1], pvs[0]), HEAD_DIM, axis=1)
            return carry

        lax.fori_loop(0, L // B, block, 0)

    dmax = DILATIONS[-1]
    nrow = L // dmax
    for r in range(dmax):
        o1 = pl.ds(r, nrow, stride=dmax)
        o4 = pl.ds((r % 4) * (L // 4) + r // 4, nrow, stride=dmax // 4)
        o16 = slice(r * nrow, (r + 1) * nrow)
        mm1, mm4, mm16 = m1[o1, :], m4[o4, :], m16[o16, :]
        mx = jnp.maximum(jnp.maximum(mm1, mm4), mm16)
        w1, w4, w16 = jnp.exp(mm1 - mx), jnp.exp(mm4 - mx), jnp.exp(mm16 - mx)
        num = w1 * a1[o1, :] + w4 * a4[o4, :] + w16 * a16[o16, :]
        den = w1 * l1[o1, :] + w4 * l4[o4, :] + w16 * l16[o16, :]
        o_ref[o1, :] = num / den


def attn_prompt(qn, kn, proj, n_batch, L, c_b, v_col0):
    n_hp = c_b // LANES
    vblk = v_col0 // LANES
    stat = [pltpu.VMEM((L, LANES), F32)] * 9
    return pl.pallas_call(
        _attn_prompt_kernel,
        grid=(n_batch, n_hp),
        in_specs=[
            pl.BlockSpec((L, LANES), lambda b, h: (b, h)),
            pl.BlockSpec((L, LANES), lambda b, h: (b, h)),
            pl.BlockSpec((L, LANES), lambda b, h: (b, vblk + h)),
        ],
        out_specs=pl.BlockSpec((L, LANES), lambda b, h: (b, h)),
        out_shape=jax.ShapeDtypeStruct((n_batch * L, c_b), F32),
        scratch_shapes=[pltpu.VMEM((L, LANES), BF16)] * 5 + stat,
        compiler_params=_cparams(("parallel", "parallel")),
        name="attn_prompt",
    )(qn, kn, proj)


def _branch_multiplicity(dist):
    c = np.zeros(dist.shape, np.float32)
    for d in DILATIONS:
        c += ((dist >= 0) & (dist % d == 0) & (dist <= d * DIL_STEPS)).astype(np.float32)
    return c


def _sample_weights(n_new, past_len, pos, n_heads):
    half = n_heads // 2
    s = np.arange(n_new)[:, None, None, None]
    h = np.arange(n_heads)[None, :, None, None]
    p = np.asarray(pos)[None, None, :, None]
    hh = np.arange(half)[None, None, None, :]
    mult = _branch_multiplicity(past_len + s - p)
    w = mult * ((h % half) == hh)
    return w.reshape(n_new * n_heads, len(pos) * half).astype(np.float32)


def _attn_sample_kernel(q_ref, ka_ref, kb_ref, kn_ref, va_ref, vb_ref, vn_ref,
                        wa_ref, wb_ref, wn_ref, o_ref):
    n_heads = q_ref.shape[-2]
    half = n_heads // 2

    def pack(ref):
        x = ref[...]
        x = x.reshape((-1,) + x.shape[-2:])
        y = jnp.concatenate([x[:, :half, :], x[:, half:, :]], axis=-1)
        return y.reshape(-1, 2 * HEAD_DIM).astype(BF16)

    q = q_ref[...].reshape(-1, HEAD_DIM)
    nq = q.shape[0]
    q2 = jnp.concatenate([q, q], axis=-1)
    row_half = (lax.broadcasted_iota(jnp.int32, (nq, 2 * HEAD_DIM), 0) % n_heads) // half
    lane_half = lax.broadcasted_iota(jnp.int32, (nq, 2 * HEAD_DIM), 1) // HEAD_DIM
    qm = jnp.where(row_half == lane_half, q2, 0.0).astype(BF16)

    parts = []
    for k_ref, v_ref, w_ref in ((ka_ref, va_ref, wa_ref), (kb_ref, vb_ref, wb_ref),
                                (kn_ref, vn_ref, wn_ref)):
        w = w_ref[...]
        s = jnp.where(w > 0.0, _nt_dot(qm, pack(k_ref)), NEG)
        parts.append((s, w, v_ref))
    m = functools.reduce(jnp.maximum, [jnp.max(s, axis=-1, keepdims=True) for s, _, _ in parts])
    den = jnp.zeros((nq, 1), F32)
    acc = jnp.zeros((nq, 2 * HEAD_DIM), F32)
    for s, w, v_ref in parts:
        p = w * jnp.exp(s - m)
        den = den + jnp.sum(p, axis=-1, keepdims=True)
        acc = acc + jnp.dot(p.astype(BF16), pack(v_ref), preferred_element_type=F32)
    sel = (lax.broadcasted_iota(jnp.int32, (nq, HEAD_DIM), 0) % n_heads) // half == 0
    out = jnp.where(sel, acc[:, :HEAD_DIM], acc[:, HEAD_DIM:]) / den
    o_ref[...] = out.reshape(o_ref.shape)


def attn_sample(q_s, k_new, v_new, cache_k, cache_v, layer, wa, wb, wn):
    n_new, n_b, n_heads, _ = q_s.shape
    depth, _, window, _, _ = cache_k.shape
    dmax = DILATIONS[-1]
    n_grp = window // dmax
    near = DILATIONS[1] * DIL_STEPS
    far_grp = (window - near) // dmax
    near_grp = near // dmax
    ck = cache_k.reshape(depth, n_b, n_grp, dmax, n_heads, HEAD_DIM)
    cv = cache_v.reshape(depth, n_b, n_grp, dmax, n_heads, HEAD_DIM)
    far_spec = pl.BlockSpec((None, None, far_grp, n_new, n_heads, HEAD_DIM),
                            lambda b: (layer, b, 0, 0, 0, 0))
    near_spec = pl.BlockSpec((None, None, near_grp, dmax, n_heads, HEAD_DIM),
                             lambda b: (layer, b, far_grp // near_grp, 0, 0, 0))
    new_spec = pl.BlockSpec((n_new, None, n_heads, HEAD_DIM), lambda b: (0, b, 0, 0))

    def const_spec(a):
        return pl.BlockSpec(a.shape, lambda b: (0, 0))

    return pl.pallas_call(
        _attn_sample_kernel,
        grid=(n_b,),
        in_specs=[new_spec, far_spec, near_spec, new_spec, far_spec, near_spec, new_spec,
                  const_spec(wa), const_spec(wb), const_spec(wn)],
        out_specs=new_spec,
        out_shape=jax.ShapeDtypeStruct((n_new, n_b, n_heads, HEAD_DIM), F32),
        compiler_params=_cparams(("parallel",)),
        name="attn_sample",
    )(q_s, ck, ck, k_new, cv, cv, v_new, wa, wb, wn)


def _layernorm_silu(y, g, b):
    mu = jnp.mean(y, axis=-1, keepdims=True)
    yc = y - mu
    var = jnp.mean(yc * yc, axis=-1, keepdims=True)
    return _silu(yc * lax.rsqrt(var + EPS) * g + b)


CONV_HALO = 32
CONV_ROWS = 32


def _conv_prompt_kernel(val_ref, gate_ref, wdw_ref, bdw_ref, lng_ref, lnb_ref, wpw_ref,
                        y_ref, st_ref, xp_s, y_s):
    c = pl.program_id(1)
    lc, ca = val_ref.shape
    off = CONV_HALO - CONV_PAST

    @pl.when(c == 0)
    def _():
        xp_s[0:CONV_HALO, :] = jnp.zeros((CONV_HALO, ca), F32)

    @pl.when(c > 0)
    def _():
        xp_s[0:CONV_HALO, :] = xp_s[lc:lc + CONV_HALO, :]

    xp_s[CONV_HALO:CONV_HALO + lc, :] = val_ref[...] * _sigmoid(gate_ref[...])

    for rb in range(lc // CONV_ROWS):
        for lb in range(ca // LANES):
            ls = slice(lb * LANES, (lb + 1) * LANES)
            acc = jnp.zeros((CONV_ROWS, LANES), F32)
            for j in range(CONV_WIDTH):
                r0 = rb * CONV_ROWS + off + j
                acc = acc + wdw_ref[j:j + 1, ls] * xp_s[r0:r0 + CONV_ROWS, ls]
            y_s[rb * CONV_ROWS:(rb + 1) * CONV_ROWS, ls] = acc

    z = _layernorm_silu(y_s[...] + bdw_ref[...], lng_ref[...], lnb_ref[...])
    y_ref[...] = jnp.dot(z.astype(BF16), wpw_ref[...].astype(BF16), preferred_element_type=F32)

    @pl.when(c == pl.num_programs(1) - 1)
    def _():
        st_ref[...] = xp_s[lc + off:lc + CONV_HALO, :]


def _layer_vec(a, layer):
    return a[layer].reshape(1, -1)


def conv_prompt(proj, n_batch, L, c_a, p, layer, lc=256):
    nch = L // lc
    vec = pl.BlockSpec((1, c_a), lambda b, c: (0, 0))
    return pl.pallas_call(
        _conv_prompt_kernel,
        grid=(n_batch, nch),
        in_specs=[
            pl.BlockSpec((lc, c_a), lambda b, c: (b * nch + c, 0)),
            pl.BlockSpec((lc, c_a), lambda b, c: (b * nch + c, 1)),
            pl.BlockSpec((None, CONV_WIDTH, c_a), lambda b, c: (layer, 0, 0)),
            vec, vec, vec,
            pl.BlockSpec((None, c_a, c_a), lambda b, c: (layer, 0, 0)),
        ],
        out_specs=[pl.BlockSpec((lc, c_a), lambda b, c: (b * nch + c, 0)),
                   pl.BlockSpec((None, CONV_PAST, c_a), lambda b, c: (b, 0, 0))],
        out_shape=[jax.ShapeDtypeStruct((n_batch * L, c_a), F32),
                   jax.ShapeDtypeStruct((n_batch, CONV_PAST, c_a), F32)],
        scratch_shapes=[pltpu.VMEM((lc + CONV_HALO, c_a), F32), pltpu.VMEM((lc, c_a), F32)],
        compiler_params=_cparams(("parallel", "arbitrary")),
        name="conv_prompt",
    )(proj, proj, p['a_w_dw'], _layer_vec(p['a_b_dw'], layer), _layer_vec(p['a_ln_g'], layer),
      _layer_vec(p['a_ln_b'], layer), p['a_w_pw'])


def _conv_sample_kernel(val_ref, gate_ref, past_ref, wdw_ref, bdw_ref, lng_ref, lnb_ref, wpw_ref,
                        y_ref, st_ref, y_s):
    n_b = past_ref.shape[1]
    n_new = val_ref.shape[0] // n_b
    g = val_ref[...] * _sigmoid(gate_ref[...])

    def xp(t):
        if t < CONV_PAST:
            return past_ref[t]
        return g[(t - CONV_PAST) * n_b:(t - CONV_PAST + 1) * n_b, :]

    for s in range(n_new):
        acc = jnp.zeros_like(xp(0))
        for j in range(CONV_WIDTH):
            acc = acc + wdw_ref[j:j + 1, :] * xp(s + j)
        y_s[s * n_b:(s + 1) * n_b, :] = acc
    z = _layernorm_silu(y_s[...] + bdw_ref[...], lng_ref[...], lnb_ref[...])
    y_ref[...] = jnp.dot(z.astype(BF16), wpw_ref[...].astype(BF16), preferred_element_type=F32)
    for t in range(CONV_PAST):
        st_ref[t] = xp(t + n_new)


def conv_sample(proj, row0, n_new, n_b, c_a, past_tm, p, layer):
    ts = n_new * n_b
    vec = pl.BlockSpec((1, c_a), lambda i: (0, 0))
    return pl.pallas_call(
        _conv_sample_kernel,
        grid=(1,),
        in_specs=[
            pl.BlockSpec((ts, c_a), lambda i: (row0 // ts, 0)),
            pl.BlockSpec((ts, c_a), lambda i: (row0 // ts, 1)),
            pl.BlockSpec((None, CONV_PAST, n_b, c_a), lambda i: (layer, 0, 0, 0)),
            pl.BlockSpec((None, CONV_WIDTH, c_a), lambda i: (layer, 0, 0)),
            vec, vec, vec,
            pl.BlockSpec((None, c_a, c_a), lambda i: (layer, 0, 0)),
        ],
        out_specs=[pl.BlockSpec((ts, c_a), lambda i: (0, 0)),
                   pl.BlockSpec((CONV_PAST, n_b, c_a), lambda i: (0, 0, 0))],
        out_shape=[jax.ShapeDtypeStruct((ts, c_a), F32),
                   jax.ShapeDtypeStruct((CONV_PAST, n_b, c_a), F32)],
        scratch_shapes=[pltpu.VMEM((ts, c_a), F32)],
        compiler_params=_cparams(("arbitrary",)),
        name="conv_sample",
    )(proj, proj, past_tm, p['a_w_dw'], _layer_vec(p['a_b_dw'], layer),
      _layer_vec(p['a_ln_g'], layer), _layer_vec(p['a_ln_b'], layer), p['a_w_pw'])


def _s5_param_kernel(are_ref, aim_ref, ldt_ref, bre_ref, bim_ref,
                     pre_ref, pim_ref, bbre_ref, bbim_ref):
    a_re, a_im = are_ref[...], aim_ref[...]
    dt = jnp.exp(ldt_ref[...])
    mag = jnp.exp(a_re * dt)
    ab_re, ab_im = mag * jnp.cos(a_im * dt), mag * jnp.sin(a_im * dt)
    nr, ni = ab_re - 1.0, ab_im
    inv = 1.0 / (a_re * a_re + a_im * a_im)
    f_re = (nr * a_re + ni * a_im) * inv
    f_im = (ni * a_re - nr * a_im) * inv
    b_re, b_im = bre_ref[...], bim_ref[...]
    bbre_ref[...] = f_re * b_re - f_im * b_im
    bbim_ref[...] = f_re * b_im + f_im * b_re
    pr, pi = ab_re, ab_im
    pre_ref[0] = pr
    pim_ref[0] = pi
    for k in range(1, SUBLANES):
        pr, pi = pr * ab_re - pi * ab_im, pr * ab_im + pi * ab_re
        pre_ref[k] = pr
        pim_ref[k] = pi


def s5_params(p, layer):
    a_re, a_im = p['c_a_re'][layer], p['c_a_im'][layer]
    G, N = a_re.shape
    C = SSM_GROUP
    b_re_t = p['c_b_re'][layer].transpose(0, 2, 1)
    b_im_t = p['c_b_im'][layer].transpose(0, 2, 1)
    pre, pim, bbre, bbim = pl.pallas_call(
        _s5_param_kernel,
        out_shape=[jax.ShapeDtypeStruct((SUBLANES, G, 1, N), F32)] * 2
        + [jax.ShapeDtypeStruct((G, C, N), F32)] * 2,
        name="s5_params",
    )(a_re.reshape(G, 1, N), a_im.reshape(G, 1, N), p['c_log_dt'][layer].reshape(G, 1, 1),
      b_re_t, b_im_t)
    S = G * N
    pre, pim = pre.reshape(SUBLANES, S), pim.reshape(SUBLANES, S)
    t = np.arange(SUBLANES)[:, None]
    tabs = []
    for k in (1, 2, 4):
        keep = jnp.asarray(t >= k)
        tabs += [jnp.where(keep, pre[k - 1][None, :], 0.0), jnp.where(keep, pim[k - 1][None, :], 0.0)]
    tabs += [pre, pim]
    tab = jnp.stack(tabs)
    eye = jnp.eye(G, dtype=F32)

    def in_proj(bb):
        return (eye[:, None, :, None] * bb[:, :, None, :]).reshape(G * C, S).astype(BF16)

    def out_proj(c):
        return (eye[:, None, :, None] * c.transpose(0, 2, 1)[:, :, None, :]).reshape(S, G * C)

    c_cat = jnp.concatenate([out_proj(p['c_c_re'][layer]), -out_proj(p['c_c_im'][layer])]).astype(BF16)
    return dict(tab=tab, bb_re=in_proj(bbre), bb_im=in_proj(bbim), c_cat=c_cat,
                ab_re=pre[0:1], ab_im=pim[0:1])


def _s5_readout(u, xr, xi, ccat_ref, d_ref, wglu_ref, bglu_ref):
    S = xr.shape[-1]
    y = (jnp.dot(xr.astype(BF16), ccat_ref[0:S, :], preferred_element_type=F32)
         + jnp.dot(xi.astype(BF16), ccat_ref[S:2 * S, :], preferred_element_type=F32)
         + d_ref[...] * u)
    z = _gelu_tanh(y)
    gate = jnp.dot(z.astype(BF16), wglu_ref[...].astype(BF16), preferred_element_type=F32)
    return z * _sigmoid(gate + bglu_ref[...])


S5_LANE_GROUP = 256


def _s5_prompt_kernel(u_ref, bbre_ref, bbim_ref, tab_ref, ccat_ref, d_ref, wglu_ref, bglu_ref,
                      y_ref, hre_ref, him_ref, xr_s, xi_s, h_s):
    c = pl.program_id(1)
    tc = u_ref.shape[0]
    S = xr_s.shape[1]

    @pl.when(c == 0)
    def _():
        h_s[...] = jnp.zeros_like(h_s)

    u = u_ref[...]
    ub = u.astype(BF16)
    xr_s[...] = jnp.dot(ub, bbre_ref[...], preferred_element_type=F32)
    xi_s[...] = jnp.dot(ub, bbim_ref[...], preferred_element_type=F32)

    for jg in range(S // S5_LANE_GROUP):
        ls = slice(jg * S5_LANE_GROUP, (jg + 1) * S5_LANE_GROUP)
        levels = [(k, tab_ref[2 * i, :, ls], tab_ref[2 * i + 1, :, ls])
                  for i, k in enumerate((1, 2, 4))]
        cr, ci = tab_ref[6, :, ls], tab_ref[7, :, ls]

        def tile(i, carry, ls=ls, levels=levels, cr=cr, ci=ci):
            hr, hi = carry
            rows = pl.ds(pl.multiple_of(i * SUBLANES, SUBLANES), SUBLANES)
            br, bi = xr_s[rows, ls], xi_s[rows, ls]
            for k, pr, pi in levels:
                sr, si = pltpu.roll(br, k, axis=0), pltpu.roll(bi, k, axis=0)
                br, bi = br + pr * sr - pi * si, bi + pr * si + pi * sr
            br, bi = br + cr * hr - ci * hi, bi + cr * hi + ci * hr
            xr_s[rows, ls] = br
            xi_s[rows, ls] = bi
            last = slice(SUBLANES - 1, SUBLANES)
            return (jnp.broadcast_to(br[last, :], br.shape), jnp.broadcast_to(bi[last, :], bi.shape))

        hr, hi = lax.fori_loop(0, tc // SUBLANES, tile, (h_s[0, :, ls], h_s[1, :, ls]))
        h_s[0, :, ls] = hr
        h_s[1, :, ls] = hi

    y_ref[...] = _s5_readout(u, xr_s[...], xi_s[...], ccat_ref, d_ref, wglu_ref, bglu_ref)
    hre_ref[...] = h_s[0]
    him_ref[...] = h_s[1]


def s5_prompt(proj, n_batch, L, c_c, u_col0, sp, p, layer, tc=256):
    nch = L // tc
    S = sp['tab'].shape[-1]
    ublk = u_col0 // c_c

    def full(a):
        return pl.BlockSpec(a.shape, lambda b, c: (0,) * a.ndim)

    d = p['c_d'][layer].reshape(1, c_c)
    bglu = p['c_b_glu'][layer].reshape(1, c_c)
    return pl.pallas_call(
        _s5_prompt_kernel,
        grid=(n_batch, nch),
        in_specs=[
            pl.BlockSpec((tc, c_c), lambda b, c: (b * nch + c, ublk)),
            full(sp['bb_re']), full(sp['bb_im']), full(sp['tab']), full(sp['c_cat']), full(d),
            pl.BlockSpec((None, c_c, c_c), lambda b, c: (layer, 0, 0)),
            full(bglu),
        ],
        out_specs=[pl.BlockSpec((tc, c_c), lambda b, c: (b * nch + c, 0)),
                   pl.BlockSpec((None, SUBLANES, S), lambda b, c: (b, 0, 0)),
                   pl.BlockSpec((None, SUBLANES, S), lambda b, c: (b, 0, 0))],
        out_shape=[jax.ShapeDtypeStruct((n_batch * L, c_c), F32),
                   jax.ShapeDtypeStruct((n_batch, SUBLANES, S), F32),
                   jax.ShapeDtypeStruct((n_batch, SUBLANES, S), F32)],
        scratch_shapes=[pltpu.VMEM((tc, S), F32), pltpu.VMEM((tc, S), F32),
                        pltpu.VMEM((2, SUBLANES, S), F32)],
        compiler_params=_cparams(("parallel", "arbitrary")),
        name="s5_prompt",
    )(proj, sp['bb_re'], sp['bb_im'], sp['tab'], sp['c_cat'], d, p['c_w_glu'], bglu)


def _s5_sample_kernel(u_ref, h0re_ref, h0im_ref, bbre_ref, bbim_ref, abre_ref, abim_ref,
                      ccat_ref, d_ref, wglu_ref, bglu_ref,
                      y_ref, hre_ref, him_ref, xr_s, xi_s):
    n_b = h0re_ref.shape[0]
    n_new = u_ref.shape[0] // n_b
    u = u_ref[...]
    ub = u.astype(BF16)
    xr_s[...] = jnp.dot(ub, bbre_ref[...], preferred_element_type=F32)
    xi_s[...] = jnp.dot(ub, bbim_ref[...], preferred_element_type=F32)
    ar, ai = abre_ref[...], abim_ref[...]
    hr, hi = h0re_ref[...], h0im_ref[...]
    for s in range(n_new):
        rows = slice(s * n_b, (s + 1) * n_b)
        hr, hi = ar * hr - ai * hi + xr_s[rows, :], ar * hi + ai * hr + xi_s[rows, :]
        xr_s[rows, :] = hr
        xi_s[rows, :] = hi
    y_ref[...] = _s5_readout(u, xr_s[...], xi_s[...], ccat_ref, d_ref, wglu_ref, bglu_ref)
    hre_ref[...] = hr
    him_ref[...] = hi


def s5_sample(proj, row0, n_new, n_b, c_c, u_col0, h0_re, h0_im, sp, p, layer):
    ts = n_new * n_b
    S = sp['tab'].shape[-1]

    def full(a):
        return pl.BlockSpec(a.shape, lambda i: (0,) * a.ndim)

    d = p['c_d'][layer].reshape(1, c_c)
    bglu = p['c_b_glu'][layer].reshape(1, c_c)
    return pl.pallas_call(
        _s5_sample_kernel,
        grid=(1,),
        in_specs=[
            pl.BlockSpec((ts, c_c), lambda i: (row0 // ts, u_col0 // c_c)),
            full(h0_re), full(h0_im), full(sp['bb_re']), full(sp['bb_im']),
            full(sp['ab_re']), full(sp['ab_im']), full(sp['c_cat']), full(d),
            pl.BlockSpec((None, c_c, c_c), lambda i: (layer, 0, 0)),
            full(bglu),
        ],
        out_specs=[pl.BlockSpec((ts, c_c), lambda i: (0, 0)),
                   pl.BlockSpec((n_b, S), lambda i: (0, 0)),
                   pl.BlockSpec((n_b, S), lambda i: (0, 0))],
        out_shape=[jax.ShapeDtypeStruct((ts, c_c), F32),
                   jax.ShapeDtypeStruct((n_b, S), F32),
                   jax.ShapeDtypeStruct((n_b, S), F32)],
        scratch_shapes=[pltpu.VMEM((ts, S), F32), pltpu.VMEM((ts, S), F32)],
        compiler_params=_cparams(("arbitrary",)),
        name="s5_sample",
    )(proj, h0_re, h0_im, sp['bb_re'], sp['bb_im'], sp['ab_re'], sp['ab_im'], sp['c_cat'], d,
      p['c_w_glu'], bglu)


def _out_proj_kernel(ya_ref, yb_ref, yc_ref, w_ref, x_ref, o_ref, cat_s):
    @pl.when(pl.program_id(1) == 0)
    def _():
        ca, cb = ya_ref.shape[1], yb_ref.shape[1]
        cat_s[:, 0:ca] = ya_ref[...].astype(BF16)
        cat_s[:, ca:ca + cb] = yb_ref[...].astype(BF16)
        cat_s[:, ca + cb:] = yc_ref[...].astype(BF16)

    o_ref[...] = x_ref[...] + jnp.dot(cat_s[...], w_ref[...].astype(BF16),
                                      preferred_element_type=F32)


def out_proj(ya, yb, yc, w_stack, layer, x, tm, tn):
    T, D = x.shape
    dm = w_stack.shape[1]

    def lhs(a):
        return pl.BlockSpec((tm, a.shape[1]), lambda i, j: (i, 0))

    return pl.pallas_call(
        _out_proj_kernel,
        grid=(T // tm, D // tn),
        in_specs=[lhs(ya), lhs(yb), lhs(yc),
                  pl.BlockSpec((None, dm, tn), lambda i, j: (layer, 0, j)),
                  pl.BlockSpec((tm, tn), lambda i, j: (i, j))],
        out_specs=pl.BlockSpec((tm, tn), lambda i, j: (i, j)),
        out_shape=jax.ShapeDtypeStruct((T, D), F32),
        scratch_shapes=[pltpu.VMEM((tm, dm), BF16)],
        compiler_params=_cparams(("parallel", "arbitrary")),
        name="out_proj",
    )(ya, yb, yc, w_stack, x)


def _norm_cast_kernel(x_ref, g_ref, o_ref):
    x = x_ref[...]
    ms = jnp.mean(x * x, axis=-1, keepdims=True)
    o_ref[...] = (x * lax.rsqrt(ms + EPS) * g_ref[...]).astype(BF16)


def norm_cast(x, g, tm):
    T, D = x.shape
    return pl.pallas_call(
        _norm_cast_kernel,
        grid=(T // tm,),
        in_specs=[pl.BlockSpec((tm, D), lambda i: (i, 0)), pl.BlockSpec((1, D), lambda i: (0, 0))],
        out_specs=pl.BlockSpec((tm, D), lambda i: (i, 0)),
        out_shape=jax.ShapeDtypeStruct((T, D), BF16),
        compiler_params=_cparams(("parallel",)),
        name="norm_cast",
    )(x, g.reshape(1, D))


def _new_expert(te_ref, i):
    prev = te_ref[jnp.maximum(i - 1, 0)]
    return jnp.logical_or(i == 0, te_ref[i] != prev)


def _ffn_up_kernel(te_ref, nv_ref, x_ref, w1_ref, w3_ref, h_ref, w1_s, w3_s):
    i = pl.program_id(1)

    @pl.when(_new_expert(te_ref, i))
    def _():
        w1_s[...] = w1_ref[...].astype(BF16)
        w3_s[...] = w3_ref[...].astype(BF16)

    @pl.when(i < nv_ref[0])
    def _():
        x = x_ref[...]
        a = jnp.dot(x, w1_s[...], preferred_element_type=F32)
        b = jnp.dot(x, w3_s[...], preferred_element_type=F32)
        h_ref[...] = (_silu(a) * b).astype(BF16)

    @pl.when(i >= nv_ref[0])
    def _():
        h_ref[...] = jnp.zeros_like(h_ref)


def ffn_up(xs, w1, w3, tile_expert, n_valid, tm, fc):
    R, D = xs.shape
    F = w1.shape[-1]
    wspec = pl.BlockSpec((None, D, fc), lambda j, i, te, nv: (te[i], 0, j))
    return pl.pallas_call(
        _ffn_up_kernel,
        grid_spec=pltpu.PrefetchScalarGridSpec(
            num_scalar_prefetch=2,
            grid=(pl.cdiv(F, fc), R // tm),
            in_specs=[pl.BlockSpec((tm, D), lambda j, i, te, nv: (i, 0)), wspec, wspec],
            out_specs=pl.BlockSpec((tm, fc), lambda j, i, te, nv: (i, j)),
            scratch_shapes=[pltpu.VMEM((D, fc), BF16)] * 2,
        ),
        out_shape=jax.ShapeDtypeStruct((R, F), BF16),
        compiler_params=_cparams(("arbitrary", "arbitrary")),
        name="ffn_up",
    )(tile_expert, n_valid, xs, w1, w3)


def _ffn_down_kernel(te_ref, nv_ref, h_ref, w2_ref, *rest, residual):
    if residual:
        r_ref, o_ref, w2_s = rest
    else:
        o_ref, w2_s = rest
    i = pl.program_id(1)

    @pl.when(_new_expert(te_ref, i))
    def _():
        w2_s[...] = w2_ref[...].astype(BF16)

    @pl.when(i < nv_ref[0])
    def _():
        y = jnp.dot(h_ref[...], w2_s[...], preferred_element_type=F32)
        o_ref[...] = r_ref[...] + y if residual else y

    @pl.when(i >= nv_ref[0])
    def _():
        o_ref[...] = jnp.zeros_like(o_ref)


def ffn_down(h, w2, tile_expert, n_valid, tm, tn, residual=None):
    R, F = h.shape
    D = w2.shape[-1]
    tile = pl.BlockSpec((tm, tn), lambda j, i, te, nv: (i, j))
    in_specs = [pl.BlockSpec((tm, F), lambda j, i, te, nv: (i, 0)),
                pl.BlockSpec((None, F, tn), lambda j, i, te, nv: (te[i], 0, j))]
    args = [h, w2]
    if residual is not None:
        in_specs.append(tile)
        args.append(residual)
    return pl.pallas_call(
        functools.partial(_ffn_down_kernel, residual=residual is not None),
        grid_spec=pltpu.PrefetchScalarGridSpec(
            num_scalar_prefetch=2,
            grid=(D // tn, R // tm),
            in_specs=in_specs,
            out_specs=tile,
            scratch_shapes=[pltpu.VMEM((F, tn), BF16)],
        ),
        out_shape=jax.ShapeDtypeStruct((R, D), F32),
        compiler_params=_cparams(("arbitrary", "arbitrary")),
        name="ffn_down",
    )(tile_expert, n_valid, *args)


def _router_kernel(x_ref, g_ref, wr_ref, br_ref, idx_ref, gate_ref):
    x = x_ref[...]
    ms = jnp.mean(x * x, axis=-1, keepdims=True)
    u = x * lax.rsqrt(ms + EPS) * g_ref[...]
    logits = lax.dot_general(wr_ref[...], u, (((1,), (1,)), ((), ())),
                             precision=lax.Precision.HIGHEST,
                             preferred_element_type=F32) + br_ref[...]
    n_e = logits.shape[0]
    eid = lax.broadcasted_iota(jnp.int32, logits.shape, 0)
    m1 = jnp.max(logits, axis=0, keepdims=True)
    i1 = jnp.min(jnp.where(logits == m1, eid, n_e), axis=0, keepdims=True)
    rest = jnp.where(eid == i1, -jnp.inf, logits)
    m2 = jnp.max(rest, axis=0, keepdims=True)
    i2 = jnp.min(jnp.where(rest == m2, eid, n_e), axis=0, keepdims=True)
    e2 = jnp.exp(m2 - m1)
    g1 = 1.0 / (1.0 + e2)
    idx_ref[...] = jnp.where(eid == 0, i1, i2)
    gate_ref[...] = jnp.where(eid == 0, g1, e2 * g1)


def router(x, g, w_router, b_router, tm):
    T, D = x.shape
    E = w_router.shape[-1]
    return pl.pallas_call(
        _router_kernel,
        grid=(T // tm,),
        in_specs=[pl.BlockSpec((tm, D), lambda i: (i, 0)),
                  pl.BlockSpec((1, D), lambda i: (0, 0)),
                  pl.BlockSpec((E, D), lambda i: (0, 0)),
                  pl.BlockSpec((E, 1), lambda i: (0, 0))],
        out_specs=[pl.BlockSpec((E, tm), lambda i: (0, i)),
                   pl.BlockSpec((E, tm), lambda i: (0, i))],
        out_shape=[jax.ShapeDtypeStruct((E, T), jnp.int32), jax.ShapeDtypeStruct((E, T), F32)],
        compiler_params=_cparams(("parallel",)),
        name="moe_router",
    )(x, g.reshape(1, D), w_router.T, b_router.reshape(E, 1))


def _row_copy(src_hbm, row, dst, r, sem):
    return pltpu.make_async_copy(src_hbm.at[pl.ds(row, 1), :], dst.at[pl.ds(r, 1), :], sem)


def _gather_norm_kernel(src_ref, x_hbm, g_ref, o_ref, buf, sem):
    tm = buf.shape[0]
    base = pl.program_id(0) * tm

    def start(r, c):
        _row_copy(x_hbm, src_ref[base + r], buf, r, sem).start()
        return c

    def wait(r, c):
        _row_copy(x_hbm, 0, buf, r, sem).wait()
        return c

    lax.fori_loop(0, tm, start, 0)
    lax.fori_loop(0, tm, wait, 0)
    x = buf[...]
    ms = jnp.mean(x * x, axis=-1, keepdims=True)
    o_ref[...] = (x * lax.rsqrt(ms + EPS) * g_ref[...]).astype(BF16)


def gather_norm(x, g, src, tm):
    T, D = x.shape
    R = src.shape[0]
    return pl.pallas_call(
        _gather_norm_kernel,
        grid_spec=pltpu.PrefetchScalarGridSpec(
            num_scalar_prefetch=1,
            grid=(R // tm,),
            in_specs=[pl.BlockSpec(memory_space=pl.ANY),
                      pl.BlockSpec((1, D), lambda i, s: (0, 0))],
            out_specs=pl.BlockSpec((tm, D), lambda i, s: (i, 0)),
            scratch_shapes=[pltpu.VMEM((tm, D), F32), pltpu.SemaphoreType.DMA(())],
        ),
        out_shape=jax.ShapeDtypeStruct((R, D), BF16),
        compiler_params=_cparams(("arbitrary",)),
        name="moe_gather",
    )(src, x, g.reshape(1, D))


def _combine_kernel(p0_ref, p1_ref, h_ref, gate_ref, y_hbm, o_ref, buf0, buf1, sem0, sem1):
    tm = buf0.shape[0]
    base = pl.program_id(0) * tm

    def start(r, c):
        _row_copy(y_hbm, p0_ref[base + r], buf0, r, sem0).start()
        _row_copy(y_hbm, p1_ref[base + r], buf1, r, sem1).start()
        return c

    def wait(r, c):
        _row_copy(y_hbm, 0, buf0, r, sem0).wait()
        _row_copy(y_hbm, 0, buf1, r, sem1).wait()
        return c

    lax.fori_loop(0, tm, start, 0)
    lax.fori_loop(0, tm, wait, 0)
    gate = gate_ref[...]
    o_ref[...] = h_ref[...] + gate[:, 0:1] * buf0[...] + gate[:, 1:2] * buf1[...]


def moe_combine(h, gates, ys, pos0, pos1, tm):
    T, D = h.shape
    return pl.pallas_call(
        _combine_kernel,
        grid_spec=pltpu.PrefetchScalarGridSpec(
            num_scalar_prefetch=2,
            grid=(T // tm,),
            in_specs=[pl.BlockSpec((tm, D), lambda i, a, b: (i, 0)),
                      pl.BlockSpec((tm, TOP_K), lambda i, a, b: (i, 0)),
                      pl.BlockSpec(memory_space=pl.ANY)],
            out_specs=pl.BlockSpec((tm, D), lambda i, a, b: (i, 0)),
            scratch_shapes=[pltpu.VMEM((tm, D), F32), pltpu.VMEM((tm, D), F32),
                            pltpu.SemaphoreType.DMA(()), pltpu.SemaphoreType.DMA(())],
        ),
        out_shape=jax.ShapeDtypeStruct((T, D), F32),
        compiler_params=_cparams(("arbitrary",)),
        name="moe_combine",
    )(pos0, pos1, h, gates, ys)


MOE_TM = 256


def moe_layout(idx, tm):
    T, K = idx.shape
    flat = idx.reshape(-1)
    onehot = (flat[:, None] == jnp.arange(N_EXPERTS)[None, :]).astype(jnp.int32)
    rank = jnp.take_along_axis(jnp.cumsum(onehot, axis=0) - onehot, flat[:, None], axis=1)[:, 0]
    counts = jnp.sum(onehot, axis=0)
    tiles = (counts + tm - 1) // tm
    tile_end = jnp.cumsum(tiles)
    start = (tile_end - tiles) * tm
    pos = start[flat] + rank
    n_tiles = (T * K) // tm + N_EXPERTS
    src = jnp.zeros((n_tiles * tm,), jnp.int32).at[pos].set(jnp.arange(T * K, dtype=jnp.int32) // K)
    tile_expert = jnp.minimum(jnp.searchsorted(tile_end, jnp.arange(n_tiles), side='right'),
                              N_EXPERTS - 1).astype(jnp.int32)
    n_valid = tile_end[-1:].astype(jnp.int32)
    return pos.reshape(T, K).astype(jnp.int32), src, tile_expert, n_valid


def moe_ffn(h, g, w_router, b_router, w1, w3, w2):
    T, D = h.shape
    idx_t, gate_t = router(h, g, w_router, b_router, tm=512)
    idx = idx_t[:TOP_K].T
    gates = gate_t[:TOP_K].T
    pos, src, tile_expert, n_valid = moe_layout(idx, MOE_TM)
    xs = gather_norm(h, g, src, MOE_TM)
    hid = ffn_up(xs, w1, w3, tile_expert, n_valid, MOE_TM, fc=1024)
    ys = ffn_down(hid, w2, tile_expert, n_valid, MOE_TM, tn=512)
    return moe_combine(h, gates, ys, pos[:, 0], pos[:, 1], MOE_TM)


def dense_ffn(h, g, w1, w3, w2):
    T, D = h.shape
    tm = MOE_TM
    xs = norm_cast(h, g, tm=512)
    tile_expert = jnp.zeros((T // tm,), jnp.int32)
    n_valid = jnp.full((1,), T // tm, jnp.int32)
    hid = ffn_up(xs, w1, w3, tile_expert, n_valid, tm, fc=512)
    return ffn_down(hid, w2, tile_expert, n_valid, tm, tn=512, residual=h)


def kernel(x_prompt, x_sample, cache_k, cache_v, state_conv, state_ssm_re, state_ssm_im,
           norm1_g, w_in, w_out, a_w_dw, a_b_dw, a_ln_g, a_ln_b, a_w_pw, b_q_g, b_k_g,
           c_a_re, c_a_im, c_log_dt, c_b_re, c_b_im, c_c_re, c_c_im, c_d, c_w_glu, c_b_glu,
           norm2_g, ffn_w1, ffn_w3, ffn_w2, moe_w_router, moe_b_router, moe_w1, moe_w3, moe_w2):
    n_batch, L, D = x_prompt.shape
    n_b, n_new, _ = x_sample.shape
    depth = w_in.shape[0]
    c_a = a_w_pw.shape[-1]
    c_c = c_w_glu.shape[-1]
    c_b = D - c_a - c_c
    n_heads = c_b // HEAD_DIM
    window = cache_k.shape[2]
    tp, ts = n_batch * L, n_b * n_new
    T = tp + ts
    q_col0, v_col0, u_col0 = 2 * c_a, 2 * c_a + 2 * c_b, 2 * c_a + 3 * c_b
    assert q_col0 == c_b and window == DILATIONS[-1] * DIL_STEPS and L == window
    tm_big = T // 8

    p = dict(a_w_dw=a_w_dw, a_b_dw=a_b_dw, a_ln_g=a_ln_g, a_ln_b=a_ln_b, a_w_pw=a_w_pw,
             c_a_re=c_a_re, c_a_im=c_a_im, c_log_dt=c_log_dt, c_b_re=c_b_re, c_b_im=c_b_im,
             c_c_re=c_c_re, c_c_im=c_c_im, c_d=c_d, c_w_glu=c_w_glu, c_b_glu=c_b_glu)

    x = jnp.concatenate([x_prompt.reshape(tp, D),
                         x_sample.transpose(1, 0, 2).reshape(ts, D)], axis=0)
    conv_past_tm = state_conv.transpose(0, 2, 1, 3)

    dmax = DILATIONS[-1]
    near = DILATIONS[1] * DIL_STEPS
    far_pos = (np.arange((window - near) // dmax)[:, None] * dmax + np.arange(n_new)[None, :]).reshape(-1)
    wa = jnp.asarray(_sample_weights(n_new, window, far_pos, n_heads))
    wb = jnp.asarray(_sample_weights(n_new, window, np.arange(window - near, window), n_heads))
    wn = jnp.asarray(_sample_weights(n_new, window, window + np.arange(n_new), n_heads))

    st = {k: [] for k in ('k_p', 'v_p', 'conv_p', 're_p', 'im_p', 'k_s', 'v_s', 'conv_s', 're_s', 'im_s')}
    for l in range(depth):
        proj = norm_matmul(x, norm1_g[l], w_in, l, tm=tm_big, tn=512)
        qn, kn = qk_norm(proj, b_q_g[l], b_k_g[l], c_b, tm=512)
        v = proj[:, v_col0:v_col0 + c_b]

        ya_p, conv_p = conv_prompt(proj, n_batch, L, c_a, p, l)
        ya_s, conv_s_tm = conv_sample(proj, tp, n_new, n_b, c_a, conv_past_tm, p, l)
        yb_p = attn_prompt(qn, kn, proj, n_batch, L, c_b, v_col0)
        sm = lambda a: a[tp:].reshape(n_new, n_b, n_heads, HEAD_DIM)
        k_new, v_new = sm(kn), sm(v)
        yb_s = attn_sample(sm(qn), k_new, v_new, cache_k, cache_v, l, wa, wb, wn)
        sp = s5_params(p, l)
        yc_p, re_p, im_p = s5_prompt(proj, n_batch, L, c_c, u_col0, sp, p, l)
        h0_re = state_ssm_re[l].reshape(n_b, -1)
        h0_im = state_ssm_im[l].reshape(n_b, -1)
        yc_s, re_s, im_s = s5_sample(proj, tp, n_new, n_b, c_c, u_col0, h0_re, h0_im, sp, p, l)

        ya = jnp.concatenate([ya_p, ya_s], axis=0)
        yb = jnp.concatenate([yb_p, yb_s.reshape(ts, c_b)], axis=0)
        yc = jnp.concatenate([yc_p, yc_s], axis=0)
        h = out_proj(ya, yb, yc, w_out, l, x, tm=tm_big, tn=512)

        j = l // 2
        if l % 2 == 0:
            x = dense_ffn(h, norm2_g[l], ffn_w1[j:j + 1], ffn_w3[j:j + 1], ffn_w2[j:j + 1])
        else:
            x = moe_ffn(h, norm2_g[l], moe_w_router[j], moe_b_router[j],
                        moe_w1[j], moe_w3[j], moe_w2[j])

        g_shape = state_ssm_re.shape[2:]
        st['k_p'].append(kn[:tp].reshape(n_batch, L, n_heads, HEAD_DIM))
        st['v_p'].append(v[:tp].reshape(n_batch, L, n_heads, HEAD_DIM))
        st['conv_p'].append(conv_p)
        st['re_p'].append(re_p[:, 0].reshape((n_batch,) + g_shape))
        st['im_p'].append(im_p[:, 0].reshape((n_batch,) + g_shape))
        st['k_s'].append(k_new.transpose(1, 0, 2, 3))
        st['v_s'].append(v_new.transpose(1, 0, 2, 3))
        st['conv_s'].append(conv_s_tm.transpose(1, 0, 2))
        st['re_s'].append(re_s.reshape((n_b,) + g_shape))
        st['im_s'].append(im_s.reshape((n_b,) + g_shape))

    y_p = x[:tp].reshape(n_batch, L, D)
    y_s = x[tp:].reshape(n_new, n_b, D).transpose(1, 0, 2)
    stk = lambda k: jnp.stack(st[k])
    return (y_p, y_s, stk('k_p'), stk('v_p'), stk('conv_p'), stk('re_p'), stk('im_p'),
            stk('k_s'), stk('v_s'), stk('conv_s'), stk('re_s'), stk('im_s'))
```

```python
import functools
import math

import numpy as np
import jax
import jax.numpy as jnp
from jax import lax
from jax.experimental import pallas as pl
from jax.experimental.pallas import tpu as pltpu

F32 = jnp.float32
BF16 = jnp.bfloat16
EPS = 1e-6
NEG = -1e30

HEAD_DIM = 64
ATTN_SCALE = HEAD_DIM ** -0.5
ATTN_BLOCK = 128
DILATIONS = (1, 4, 16)
DIL_STEPS = 128
CONV_WIDTH = 31
CONV_PAST = CONV_WIDTH - 1
SSM_GROUP = 16
SSM_STATE = 64
N_EXPERTS = 8
TOP_K = 2
LANES = 128
SUBLANES = 8
VMEM_LIMIT = 56 * 1024 * 1024


def _cparams(sem, vmem=VMEM_LIMIT):
    return pltpu.CompilerParams(dimension_semantics=sem, vmem_limit_bytes=vmem)


def _nt_dot(a, b):
    return lax.dot_general(a, b, (((1,), (1,)), ((), ())), preferred_element_type=F32)


def _sigmoid(x):
    return 1.0 / (1.0 + jnp.exp(-x))


def _silu(x):
    return x * _sigmoid(x)


def _gelu_tanh(x):
    c = math.sqrt(2.0 / math.pi)
    return 0.5 * x * (1.0 + jnp.tanh(c * (x + 0.044715 * (x * x * x))))


def _norm_mm_kernel(x_ref, g_ref, w_ref, o_ref, u_scr):
    @pl.when(pl.program_id(1) == 0)
    def _():
        x = x_ref[...]
        ms = jnp.mean(x * x, axis=-1, keepdims=True)
        u_scr[...] = (x * lax.rsqrt(ms + EPS) * g_ref[...]).astype(BF16)

    o_ref[...] = jnp.dot(u_scr[...], w_ref[...].astype(BF16), preferred_element_type=F32)


def norm_matmul(x, g, w_stack, layer, tm, tn):
    T, D = x.shape
    N = w_stack.shape[-1]
    return pl.pallas_call(
        _norm_mm_kernel,
        grid=(T // tm, N // tn),
        in_specs=[
            pl.BlockSpec((tm, D), lambda i, j: (i, 0)),
            pl.BlockSpec((1, D), lambda i, j: (0, 0)),
            pl.BlockSpec((None, D, tn), lambda i, j: (layer, 0, j)),
        ],
        out_specs=pl.BlockSpec((tm, tn), lambda i, j: (i, j)),
        out_shape=jax.ShapeDtypeStruct((T, N), F32),
        scratch_shapes=[pltpu.VMEM((tm, D), BF16)],
        compiler_params=_cparams(("parallel", "arbitrary")),
        name="norm_matmul",
    )(x, g.reshape(1, D), w_stack)


def _qknorm_kernel(q_ref, k_ref, gq_ref, gk_ref, qo_ref, ko_ref):
    tm = q_ref.shape[0]
    lane = lax.broadcasted_iota(jnp.int32, (tm, LANES), 1)
    head0 = lane < HEAD_DIM

    def norm(x, g):
        sq = x * x
        s0 = jnp.sum(jnp.where(head0, sq, 0.0), axis=-1, keepdims=True)
        s1 = jnp.sum(jnp.where(head0, 0.0, sq), axis=-1, keepdims=True)
        ms = jnp.where(head0, s0, s1) * (1.0 / HEAD_DIM)
        return x * lax.rsqrt(ms + EPS) * g

    for t in range(q_ref.shape[1] // LANES):
        sl = slice(t * LANES, (t + 1) * LANES)
        qo_ref[:, sl] = norm(q_ref[:, sl], gq_ref[...]) * ATTN_SCALE
        ko_ref[:, sl] = norm(k_ref[:, sl], gk_ref[...])


def qk_norm(proj, gq, gk, c_b, tm):
    T = proj.shape[0]
    gq2 = jnp.concatenate([gq, gq]).reshape(1, LANES)
    gk2 = jnp.concatenate([gk, gk]).reshape(1, LANES)
    qblk = 1
    return pl.pallas_call(
        _qknorm_kernel,
        grid=(T // tm,),
        in_specs=[
            pl.BlockSpec((tm, c_b), lambda i: (i, qblk)),
            pl.BlockSpec((tm, c_b), lambda i: (i, qblk + 1)),
            pl.BlockSpec((1, LANES), lambda i: (0, 0)),
            pl.BlockSpec((1, LANES), lambda i: (0, 0)),
        ],
        out_specs=[pl.BlockSpec((tm, c_b), lambda i: (i, 0)),
                   pl.BlockSpec((tm, c_b), lambda i: (i, 0))],
        out_shape=[jax.ShapeDtypeStruct((T, c_b), F32)] * 2,
        compiler_params=_cparams(("parallel",)),
        name="qk_norm",
    )(proj, proj, gq2, gk2)


def _attn_prompt_kernel(q_ref, k_ref, v_ref, o_ref,
                        q0_s, q1_s, k_s, v_s,
                        m1, l1, a1, m4, l4, a4, m16, l16, a16):
    L = q_ref.shape[0]
    B = ATTN_BLOCK
    head0 = lax.broadcasted_iota(jnp.int32, (B, LANES), 1) < HEAD_DIM
    qi = lax.broadcasted_iota(jnp.int32, (2 * B, B), 0) % B
    ki = lax.broadcasted_iota(jnp.int32, (2 * B, B), 1)
    tri_cur = ki <= qi
    tri_prev = ki >= qi

    stats = {1: (m1, l1, a1), 4: (m4, l4, a4), 16: (m16, l16, a16)}

    for d in DILATIONS:
        n = L // d
        nb = n // B
        m_s, l_s, a_s = stats[d]
        for r in range(d):
            src = pl.ds(r, n, stride=d) if d > 1 else slice(None)
            dst = slice(r * n, (r + 1) * n)
            hm = lax.broadcasted_iota(jnp.int32, (n, LANES), 1) < HEAD_DIM
            q = q_ref[src, :]
            q0_s[dst, :] = jnp.where(hm, q, 0.0).astype(BF16)
            q1_s[dst, :] = jnp.where(hm, 0.0, q).astype(BF16)
            k_s[dst, :] = k_ref[src, :].astype(BF16)
            v_s[dst, :] = v_ref[src, :].astype(BF16)

        def block(j, carry, nb=nb, m_s=m_s, l_s=l_s, a_s=a_s):
            cur = pl.ds(pl.multiple_of(j * B, B), B)
            qb = jnp.concatenate([q0_s[cur, :], q1_s[cur, :]], axis=0)
            s_c = jnp.where(tri_cur, _nt_dot(qb, k_s[cur, :]), NEG)
            m = jnp.max(s_c, axis=-1, keepdims=True)
            if nb > 1:
                prev = pl.ds(pl.multiple_of(jnp.maximum(j - 1, 0) * B, B), B)
                mask_prev = jnp.logical_and(tri_prev, (j % nb) > 0)
                s_p = jnp.where(mask_prev, _nt_dot(qb, k_s[prev, :]), NEG)
                m = jnp.maximum(m, jnp.max(s_p, axis=-1, keepdims=True))
            p_c = jnp.exp(s_c - m)
            den = jnp.sum(p_c, axis=-1, keepdims=True)
            pv = jnp.dot(p_c.astype(BF16), v_s[cur, :], preferred_element_type=F32)
            if nb > 1:
                p_p = jnp.exp(s_p - m)
                den = den + jnp.sum(p_p, axis=-1, keepdims=True)
                pv = pv + jnp.dot(p_p.astype(BF16), v_s[prev, :], preferred_element_type=F32)
            m_s[cur, :] = jnp.where(head0, m[:B], m[B:])
            l_s[cur, :] = jnp.where(head0, den[:B], den[B:])
            a_s[cur, :] = jnp.where(head0, pv[:B], pv[B:])
            return carry

        lax.fori_loop(0, L // B, block, 0, unroll=2)

    dmax = DILATIONS[-1]
    nrow = L // dmax
    for r in range(dmax):
        o1 = pl.ds(r, nrow, stride=dmax)
        o4 = pl.ds((r % 4) * (L // 4) + r // 4, nrow, stride=dmax // 4)
        o16 = slice(r * nrow, (r + 1) * nrow)
        mm1, mm4, mm16 = m1[o1, :], m4[o4, :], m16[o16, :]
        mx = jnp.maximum(jnp.maximum(mm1, mm4), mm16)
        w1, w4, w16 = jnp.exp(mm1 - mx), jnp.exp(mm4 - mx), jnp.exp(mm16 - mx)
        num = w1 * a1[o1, :] + w4 * a4[o4, :] + w16 * a16[o16, :]
        den = w1 * l1[o1, :] + w4 * l4[o4, :] + w16 * l16[o16, :]
        o_ref[o1, :] = num / den


def attn_prompt(qn, kn, proj, n_batch, L, c_b, v_col0):
    n_hp = c_b // LANES
    vblk = v_col0 // LANES
    stat = [pltpu.VMEM((L, LANES), F32)] * 9
    return pl.pallas_call(
        _attn_prompt_kernel,
        grid=(n_batch, n_hp),
        in_specs=[
            pl.BlockSpec((L, LANES), lambda b, h: (b, h)),
            pl.BlockSpec((L, LANES), lambda b, h: (b, h)),
            pl.BlockSpec((L, LANES), lambda b, h: (b, vblk + h)),
        ],
        out_specs=pl.BlockSpec((L, LANES), lambda b, h: (b, h)),
        out_shape=jax.ShapeDtypeStruct((n_batch * L, c_b), F32),
        scratch_shapes=[pltpu.VMEM((L, LANES), BF16)] * 4 + stat,
        compiler_params=_cparams(("parallel", "parallel")),
        name="attn_prompt",
    )(qn, kn, proj)


def _branch_multiplicity(dist):
    c = np.zeros(dist.shape, np.float32)
    for d in DILATIONS:
        c += ((dist >= 0) & (dist % d == 0) & (dist <= d * DIL_STEPS)).astype(np.float32)
    return c


NEW_ROWS_PAD = 16


def _sample_weights(n_new, past_len, pos, n_heads):
    s = np.repeat(np.arange(n_new), n_heads)[:, None]
    return _branch_multiplicity(past_len + s - np.asarray(pos)[None, :]).astype(np.float32)


def _attn_sample_kernel(q_ref, kt_ref, vt_ref, kn_ref, vn_ref, w_ref, wn_ref, o_ref):
    n_heads, hd, window = kt_ref.shape
    width = n_heads * hd
    q = q_ref[...]
    nq = q.shape[0]
    q2 = jnp.concatenate([q, q], axis=-1)
    qt = jnp.concatenate([q2] * (width // (2 * hd)), axis=-1)
    own_head = (lax.broadcasted_iota(jnp.int32, (nq, width), 0) % n_heads
                == lax.broadcasted_iota(jnp.int32, (nq, width), 1) // hd)
    qbd = jnp.where(own_head, qt, 0.0).astype(BF16)

    w, wn = w_ref[...], wn_ref[...]
    s_c = jnp.dot(qbd, kt_ref[...].reshape(width, window).astype(BF16),
                  preferred_element_type=F32)
    s_n = _nt_dot(qbd, kn_ref[...].astype(BF16))
    s_c = jnp.where(w > 0.0, s_c, NEG)
    s_n = jnp.where(wn > 0.0, s_n, NEG)
    m = jnp.maximum(jnp.max(s_c, axis=-1, keepdims=True), jnp.max(s_n, axis=-1, keepdims=True))
    p_c = w * jnp.exp(s_c - m)
    p_n = wn * jnp.exp(s_n - m)
    den = jnp.sum(p_c, axis=-1, keepdims=True) + jnp.sum(p_n, axis=-1, keepdims=True)
    full = (_nt_dot(p_c.astype(BF16), vt_ref[...].reshape(width, window).astype(BF16))
            + jnp.dot(p_n.astype(BF16), vn_ref[...].astype(BF16), preferred_element_type=F32))
    full = jnp.where(own_head, full, 0.0)
    acc = full[:, 0:LANES]
    for t in range(1, width // LANES):
        acc = acc + full[:, t * LANES:(t + 1) * LANES]
    o_ref[...] = (acc[:, :hd] + acc[:, hd:]) / den


def attn_sample(q_b, k_new, v_new, cache_k, cache_v, layer, w, wn):
    n_b, nq, _ = q_b.shape
    depth, _, window, n_heads, _ = cache_k.shape
    width = n_heads * HEAD_DIM
    kt = cache_k.transpose(0, 1, 3, 4, 2)
    vt = cache_v.transpose(0, 1, 3, 4, 2)
    cache_spec = pl.BlockSpec((None, None, n_heads, HEAD_DIM, window), lambda b: (layer, b, 0, 0, 0))
    new_spec = pl.BlockSpec((None, NEW_ROWS_PAD, width), lambda b: (b, 0, 0))
    q_spec = pl.BlockSpec((None, nq, HEAD_DIM), lambda b: (b, 0, 0))

    def const_spec(a):
        return pl.BlockSpec(a.shape, lambda b: (0, 0))

    return pl.pallas_call(
        _attn_sample_kernel,
        grid=(n_b,),
        in_specs=[q_spec, cache_spec, cache_spec, new_spec, new_spec, const_spec(w), const_spec(wn)],
        out_specs=q_spec,
        out_shape=jax.ShapeDtypeStruct((n_b, nq, HEAD_DIM), F32),
        compiler_params=_cparams(("parallel",)),
        name="attn_sample",
    )(q_b, kt, vt, k_new, v_new, w, wn)


def _layernorm_silu(y, g, b):
    mu = jnp.mean(y, axis=-1, keepdims=True)
    yc = y - mu
    var = jnp.mean(yc * yc, axis=-1, keepdims=True)
    return _silu(yc * lax.rsqrt(var + EPS) * g + b)


CONV_HALO = 32
CONV_ROWS = 32


def _conv_prompt_kernel(val_ref, gate_ref, wdw_ref, bdw_ref, lng_ref, lnb_ref, wpw_ref,
                        y_ref, st_ref, xp_s, y_s):
    c = pl.program_id(1)
    lc, ca = val_ref.shape
    off = CONV_HALO - CONV_PAST

    @pl.when(c == 0)
    def _():
        xp_s[0:CONV_HALO, :] = jnp.zeros((CONV_HALO, ca), F32)

    @pl.when(c > 0)
    def _():
        xp_s[0:CONV_HALO, :] = xp_s[lc:lc + CONV_HALO, :]

    xp_s[CONV_HALO:CONV_HALO + lc, :] = val_ref[...] * _sigmoid(gate_ref[...])

    for rb in range(lc // CONV_ROWS):
        for lb in range(ca // LANES):
            ls = slice(lb * LANES, (lb + 1) * LANES)
            acc = jnp.zeros((CONV_ROWS, LANES), F32)
            for j in range(CONV_WIDTH):
                r0 = rb * CONV_ROWS + off + j
                acc = acc + wdw_ref[j:j + 1, ls] * xp_s[r0:r0 + CONV_ROWS, ls]
            y_s[rb * CONV_ROWS:(rb + 1) * CONV_ROWS, ls] = acc

    z = _layernorm_silu(y_s[...] + bdw_ref[...], lng_ref[...], lnb_ref[...])
    y_ref[...] = jnp.dot(z.astype(BF16), wpw_ref[...].astype(BF16), preferred_element_type=F32)

    @pl.when(c == pl.num_programs(1) - 1)
    def _():
        st_ref[...] = xp_s[lc + off:lc + CONV_HALO, :]


def _layer_vec(a, layer):
    return a[layer].reshape(1, -1)


def conv_prompt(proj, n_batch, L, c_a, p, layer, lc=256):
    nch = L // lc
    vec = pl.BlockSpec((1, c_a), lambda b, c: (0, 0))
    return pl.pallas_call(
        _conv_prompt_kernel,
        grid=(n_batch, nch),
        in_specs=[
            pl.BlockSpec((lc, c_a), lambda b, c: (b * nch + c, 0)),
            pl.BlockSpec((lc, c_a), lambda b, c: (b * nch + c, 1)),
            pl.BlockSpec((None, CONV_WIDTH, c_a), lambda b, c: (layer, 0, 0)),
            vec, vec, vec,
            pl.BlockSpec((None, c_a, c_a), lambda b, c: (layer, 0, 0)),
        ],
        out_specs=[pl.BlockSpec((lc, c_a), lambda b, c: (b * nch + c, 0)),
                   pl.BlockSpec((None, CONV_PAST, c_a), lambda b, c: (b, 0, 0))],
        out_shape=[jax.ShapeDtypeStruct((n_batch * L, c_a), F32),
                   jax.ShapeDtypeStruct((n_batch, CONV_PAST, c_a), F32)],
        scratch_shapes=[pltpu.VMEM((lc + CONV_HALO, c_a), F32), pltpu.VMEM((lc, c_a), F32)],
        compiler_params=_cparams(("parallel", "arbitrary")),
        name="conv_prompt",
    )(proj, proj, p['a_w_dw'], _layer_vec(p['a_b_dw'], layer), _layer_vec(p['a_ln_g'], layer),
      _layer_vec(p['a_ln_b'], layer), p['a_w_pw'])


def _conv_sample_kernel(val_ref, gate_ref, past_ref, wdw_ref, bdw_ref, lng_ref, lnb_ref, wpw_ref,
                        y_ref, st_ref, y_s):
    n_b = past_ref.shape[1]
    n_new = val_ref.shape[0] // n_b
    g = val_ref[...] * _sigmoid(gate_ref[...])

    def xp(t):
        if t < CONV_PAST:
            return past_ref[t]
        return g[(t - CONV_PAST) * n_b:(t - CONV_PAST + 1) * n_b, :]

    for s in range(n_new):
        acc = jnp.zeros_like(xp(0))
        for j in range(CONV_WIDTH):
            acc = acc + wdw_ref[j:j + 1, :] * xp(s + j)
        y_s[s * n_b:(s + 1) * n_b, :] = acc
    z = _layernorm_silu(y_s[...] + bdw_ref[...], lng_ref[...], lnb_ref[...])
    y_ref[...] = jnp.dot(z.astype(BF16), wpw_ref[...].astype(BF16), preferred_element_type=F32)
    for t in range(CONV_PAST):
        st_ref[t] = xp(t + n_new)


def conv_sample(proj, row0, n_new, n_b, c_a, past_tm, p, layer):
    ts = n_new * n_b
    vec = pl.BlockSpec((1, c_a), lambda i: (0, 0))
    return pl.pallas_call(
        _conv_sample_kernel,
        grid=(1,),
        in_specs=[
            pl.BlockSpec((ts, c_a), lambda i: (row0 // ts, 0)),
            pl.BlockSpec((ts, c_a), lambda i: (row0 // ts, 1)),
            pl.BlockSpec((None, CONV_PAST, n_b, c_a), lambda i: (layer, 0, 0, 0)),
            pl.BlockSpec((None, CONV_WIDTH, c_a), lambda i: (layer, 0, 0)),
            vec, vec, vec,
            pl.BlockSpec((None, c_a, c_a), lambda i: (layer, 0, 0)),
        ],
        out_specs=[pl.BlockSpec((ts, c_a), lambda i: (0, 0)),
                   pl.BlockSpec((CONV_PAST, n_b, c_a), lambda i: (0, 0, 0))],
        out_shape=[jax.ShapeDtypeStruct((ts, c_a), F32),
                   jax.ShapeDtypeStruct((CONV_PAST, n_b, c_a), F32)],
        scratch_shapes=[pltpu.VMEM((ts, c_a), F32)],
        compiler_params=_cparams(("arbitrary",)),
        name="conv_sample",
    )(proj, proj, past_tm, p['a_w_dw'], _layer_vec(p['a_b_dw'], layer),
      _layer_vec(p['a_ln_g'], layer), _layer_vec(p['a_ln_b'], layer), p['a_w_pw'])


def _s5_param_kernel(are_ref, aim_ref, ldt_ref, bre_ref, bim_ref,
                     pre_ref, pim_ref, bbre_ref, bbim_ref):
    a_re, a_im = are_ref[...], aim_ref[...]
    dt = jnp.exp(ldt_ref[...])
    mag = jnp.exp(a_re * dt)
    ab_re, ab_im = mag * jnp.cos(a_im * dt), mag * jnp.sin(a_im * dt)
    nr, ni = ab_re - 1.0, ab_im
    inv = 1.0 / (a_re * a_re + a_im * a_im)
    f_re = (nr * a_re + ni * a_im) * inv
    f_im = (ni * a_re - nr * a_im) * inv
    b_re, b_im = bre_ref[...], bim_ref[...]
    bbre_ref[...] = f_re * b_re - f_im * b_im
    bbim_ref[...] = f_re * b_im + f_im * b_re
    pr, pi = ab_re, ab_im
    pre_ref[0] = pr
    pim_ref[0] = pi
    for k in range(1, SUBLANES):
        pr, pi = pr * ab_re - pi * ab_im, pr * ab_im + pi * ab_re
        pre_ref[k] = pr
        pim_ref[k] = pi


def s5_params(p, layer):
    a_re, a_im = p['c_a_re'][layer], p['c_a_im'][layer]
    G, N = a_re.shape
    C = SSM_GROUP
    b_re_t = p['c_b_re'][layer].transpose(0, 2, 1)
    b_im_t = p['c_b_im'][layer].transpose(0, 2, 1)
    pre, pim, bbre, bbim = pl.pallas_call(
        _s5_param_kernel,
        out_shape=[jax.ShapeDtypeStruct((SUBLANES, G, 1, N), F32)] * 2
        + [jax.ShapeDtypeStruct((G, C, N), F32)] * 2,
        name="s5_params",
    )(a_re.reshape(G, 1, N), a_im.reshape(G, 1, N), p['c_log_dt'][layer].reshape(G, 1, 1),
      b_re_t, b_im_t)
    S = G * N
    pre, pim = pre.reshape(SUBLANES, S), pim.reshape(SUBLANES, S)
    t = np.arange(SUBLANES)[:, None]
    tabs = []
    for k in (1, 2, 4):
        keep = jnp.asarray(t >= k)
        tabs += [jnp.where(keep, pre[k - 1][None, :], 0.0), jnp.where(keep, pim[k - 1][None, :], 0.0)]
    tabs += [pre, pim]
    tab = jnp.stack(tabs)
    eye = jnp.eye(G, dtype=F32)

    def in_proj(bb):
        return (eye[:, None, :, None] * bb[:, :, None, :]).reshape(G * C, S).astype(BF16)

    def out_proj(c):
        return (eye[:, None, :, None] * c.transpose(0, 2, 1)[:, :, None, :]).reshape(S, G * C)

    c_cat = jnp.concatenate([out_proj(p['c_c_re'][layer]), -out_proj(p['c_c_im'][layer])]).astype(BF16)
    return dict(tab=tab, bb_re=in_proj(bbre), bb_im=in_proj(bbim), c_cat=c_cat,
                ab_re=pre[0:1], ab_im=pim[0:1])


def _s5_readout(u, xr, xi, ccat_ref, d_ref, wglu_ref, bglu_ref):
    S = xr.shape[-1]
    y = (jnp.dot(xr.astype(BF16), ccat_ref[0:S, :], preferred_element_type=F32)
         + jnp.dot(xi.astype(BF16), ccat_ref[S:2 * S, :], preferred_element_type=F32)
         + d_ref[...] * u)
    z = _gelu_tanh(y)
    gate = jnp.dot(z.astype(BF16), wglu_ref[...].astype(BF16), preferred_element_type=F32)
    return z * _sigmoid(gate + bglu_ref[...])


S5_LANE_GROUP = 256


def _s5_prompt_kernel(u_ref, bbre_ref, bbim_ref, tab_ref, ccat_ref, d_ref, wglu_ref, bglu_ref,
                      y_ref, hre_ref, him_ref, xr_s, xi_s, h_s):
    c = pl.program_id(1)
    tc = u_ref.shape[0]
    S = xr_s.shape[1]

    @pl.when(c == 0)
    def _():
        h_s[...] = jnp.zeros_like(h_s)

    u = u_ref[...]
    ub = u.astype(BF16)
    xr_s[...] = jnp.dot(ub, bbre_ref[...], preferred_element_type=F32)
    xi_s[...] = jnp.dot(ub, bbim_ref[...], preferred_element_type=F32)

    for jg in range(S // S5_LANE_GROUP):
        ls = slice(jg * S5_LANE_GROUP, (jg + 1) * S5_LANE_GROUP)
        levels = [(k, tab_ref[2 * i, :, ls], tab_ref[2 * i + 1, :, ls])
                  for i, k in enumerate((1, 2, 4))]
        cr, ci = tab_ref[6, :, ls], tab_ref[7, :, ls]

        def tile(i, carry, ls=ls, levels=levels, cr=cr, ci=ci):
            hr, hi = carry
            rows = pl.ds(pl.multiple_of(i * SUBLANES, SUBLANES), SUBLANES)
            br, bi = xr_s[rows, ls], xi_s[rows, ls]
            for k, pr, pi in levels:
                sr, si = pltpu.roll(br, k, axis=0), pltpu.roll(bi, k, axis=0)
                br, bi = br + pr * sr - pi * si, bi + pr * si + pi * sr
            br, bi = br + cr * hr - ci * hi, bi + cr * hi + ci * hr
            xr_s[rows, ls] = br
            xi_s[rows, ls] = bi
            last = slice(SUBLANES - 1, SUBLANES)
            return (jnp.broadcast_to(br[last, :], br.shape), jnp.broadcast_to(bi[last, :], bi.shape))

        hr, hi = lax.fori_loop(0, tc // SUBLANES, tile, (h_s[0, :, ls], h_s[1, :, ls]))
        h_s[0, :, ls] = hr
        h_s[1, :, ls] = hi

    y_ref[...] = _s5_readout(u, xr_s[...], xi_s[...], ccat_ref, d_ref, wglu_ref, bglu_ref)
    hre_ref[...] = h_s[0]
    him_ref[...] = h_s[1]


def s5_prompt(proj, n_batch, L, c_c, u_col0, sp, p, layer, tc=256):
    nch = L // tc
    S = sp['tab'].shape[-1]
    ublk = u_col0 // c_c

    def full(a):
        return pl.BlockSpec(a.shape, lambda b, c: (0,) * a.ndim)

    d = p['c_d'][layer].reshape(1, c_c)
    bglu = p['c_b_glu'][layer].reshape(1, c_c)
    return pl.pallas_call(
        _s5_prompt_kernel,
        grid=(n_batch, nch),
        in_specs=[
            pl.BlockSpec((tc, c_c), lambda b, c: (b * nch + c, ublk)),
            full(sp['bb_re']), full(sp['bb_im']), full(sp['tab']), full(sp['c_cat']), full(d),
            pl.BlockSpec((None, c_c, c_c), lambda b, c: (layer, 0, 0)),
            full(bglu),
        ],
        out_specs=[pl.BlockSpec((tc, c_c), lambda b, c: (b * nch + c, 0)),
                   pl.BlockSpec((None, SUBLANES, S), lambda b, c: (b, 0, 0)),
                   pl.BlockSpec((None, SUBLANES, S), lambda b, c: (b, 0, 0))],
        out_shape=[jax.ShapeDtypeStruct((n_batch * L, c_c), F32),
                   jax.ShapeDtypeStruct((n_batch, SUBLANES, S), F32),
                   jax.ShapeDtypeStruct((n_batch, SUBLANES, S), F32)],
        scratch_shapes=[pltpu.VMEM((tc, S), F32), pltpu.VMEM((tc, S), F32),
                        pltpu.VMEM((2, SUBLANES, S), F32)],
        compiler_params=_cparams(("parallel", "arbitrary")),
        name="s5_prompt",
    )(proj, sp['bb_re'], sp['bb_im'], sp['tab'], sp['c_cat'], d, p['c_w_glu'], bglu)


def _s5_sample_kernel(u_ref, h0re_ref, h0im_ref, bbre_ref, bbim_ref, abre_ref, abim_ref,
                      ccat_ref, d_ref, wglu_ref, bglu_ref,
                      y_ref, hre_ref, him_ref, xr_s, xi_s):
    n_b = h0re_ref.shape[0]
    n_new = u_ref.shape[0] // n_b
    u = u_ref[...]
    ub = u.astype(BF16)
    xr_s[...] = jnp.dot(ub, bbre_ref[...], preferred_element_type=F32)
    xi_s[...] = jnp.dot(ub, bbim_ref[...], preferred_element_type=F32)
    ar, ai = abre_ref[...], abim_ref[...]
    hr, hi = h0re_ref[...], h0im_ref[...]
    for s in range(n_new):
        rows = slice(s * n_b, (s + 1) * n_b)
        hr, hi = ar * hr - ai * hi + xr_s[rows, :], ar * hi + ai * hr + xi_s[rows, :]
        xr_s[rows, :] = hr
        xi_s[rows, :] = hi
    y_ref[...] = _s5_readout(u, xr_s[...], xi_s[...], ccat_ref, d_ref, wglu_ref, bglu_ref)
    hre_ref[...] = hr
    him_ref[...] = hi


def s5_sample(proj, row0, n_new, n_b, c_c, u_col0, h0_re, h0_im, sp, p, layer):
    ts = n_new * n_b
    S = sp['tab'].shape[-1]

    def full(a):
        return pl.BlockSpec(a.shape, lambda i: (0,) * a.ndim)

    d = p['c_d'][layer].reshape(1, c_c)
    bglu = p['c_b_glu'][layer].reshape(1, c_c)
    return pl.pallas_call(
        _s5_sample_kernel,
        grid=(1,),
        in_specs=[
            pl.BlockSpec((ts, c_c), lambda i: (row0 // ts, u_col0 // c_c)),
            full(h0_re), full(h0_im), full(sp['bb_re']), full(sp['bb_im']),
            full(sp['ab_re']), full(sp['ab_im']), full(sp['c_cat']), full(d),
            pl.BlockSpec((None, c_c, c_c), lambda i: (layer, 0, 0)),
            full(bglu),
        ],
        out_specs=[pl.BlockSpec((ts, c_c), lambda i: (0, 0)),
                   pl.BlockSpec((n_b, S), lambda i: (0, 0)),
                   pl.BlockSpec((n_b, S), lambda i: (0, 0))],
        out_shape=[jax.ShapeDtypeStruct((ts, c_c), F32),
                   jax.ShapeDtypeStruct((n_b, S), F32),
                   jax.ShapeDtypeStruct((n_b, S), F32)],
        scratch_shapes=[pltpu.VMEM((ts, S), F32), pltpu.VMEM((ts, S), F32)],
        compiler_params=_cparams(("arbitrary",)),
        name="s5_sample",
    )(proj, h0_re, h0_im, sp['bb_re'], sp['bb_im'], sp['ab_re'], sp['ab_im'], sp['c_cat'], d,
      p['c_w_glu'], bglu)


def _out_proj_kernel(ya_ref, yb_ref, yc_ref, w_ref, x_ref, o_ref, cat_s):
    @pl.when(pl.program_id(1) == 0)
    def _():
        ca, cb = ya_ref.shape[1], yb_ref.shape[1]
        cat_s[:, 0:ca] = ya_ref[...].astype(BF16)
        cat_s[:, ca:ca + cb] = yb_ref[...].astype(BF16)
        cat_s[:, ca + cb:] = yc_ref[...].astype(BF16)

    o_ref[...] = x_ref[...] + jnp.dot(cat_s[...], w_ref[...].astype(BF16),
                                      preferred_element_type=F32)


def out_proj(ya, yb, yc, w_stack, layer, x, tm, tn):
    T, D = x.shape
    dm = w_stack.shape[1]

    def lhs(a):
        return pl.BlockSpec((tm, a.shape[1]), lambda i, j: (i, 0))

    return pl.pallas_call(
        _out_proj_kernel,
        grid=(T // tm, D // tn),
        in_specs=[lhs(ya), lhs(yb), lhs(yc),
                  pl.BlockSpec((None, dm, tn), lambda i, j: (layer, 0, j)),
                  pl.BlockSpec((tm, tn), lambda i, j: (i, j))],
        out_specs=pl.BlockSpec((tm, tn), lambda i, j: (i, j)),
        out_shape=jax.ShapeDtypeStruct((T, D), F32),
        scratch_shapes=[pltpu.VMEM((tm, dm), BF16)],
        compiler_params=_cparams(("parallel", "arbitrary")),
        name="out_proj",
    )(ya, yb, yc, w_stack, x)


def _norm_cast_kernel(x_ref, g_ref, o_ref):
    x = x_ref[...]
    ms = jnp.mean(x * x, axis=-1, keepdims=True)
    o_ref[...] = (x * lax.rsqrt(ms + EPS) * g_ref[...]).astype(BF16)


def norm_cast(x, g, tm):
    T, D = x.shape
    return pl.pallas_call(
        _norm_cast_kernel,
        grid=(T // tm,),
        in_specs=[pl.BlockSpec((tm, D), lambda i: (i, 0)), pl.BlockSpec((1, D), lambda i: (0, 0))],
        out_specs=pl.BlockSpec((tm, D), lambda i: (i, 0)),
        out_shape=jax.ShapeDtypeStruct((T, D), BF16),
        compiler_params=_cparams(("parallel",)),
        name="norm_cast",
    )(x, g.reshape(1, D))


def _new_expert(te_ref, i):
    prev = te_ref[jnp.maximum(i - 1, 0)]
    return jnp.logical_or(i == 0, te_ref[i] != prev)


def _ffn_up_kernel(te_ref, nv_ref, x_ref, w1_ref, w3_ref, h_ref, w1_s, w3_s):
    i = pl.program_id(1)

    @pl.when(_new_expert(te_ref, i))
    def _():
        w1_s[...] = w1_ref[...].astype(BF16)
        w3_s[...] = w3_ref[...].astype(BF16)

    @pl.when(i < nv_ref[0])
    def _():
        x = x_ref[...]
        a = jnp.dot(x, w1_s[...], preferred_element_type=F32)
        b = jnp.dot(x, w3_s[...], preferred_element_type=F32)
        h_ref[...] = (_silu(a) * b).astype(BF16)

    @pl.when(i >= nv_ref[0])
    def _():
        h_ref[...] = jnp.zeros_like(h_ref)


def ffn_up(xs, w1, w3, tile_expert, n_valid, tm, fc):
    R, D = xs.shape
    F = w1.shape[-1]
    wspec = pl.BlockSpec((None, D, fc), lambda j, i, te, nv: (te[i], 0, j))
    return pl.pallas_call(
        _ffn_up_kernel,
        grid_spec=pltpu.PrefetchScalarGridSpec(
            num_scalar_prefetch=2,
            grid=(pl.cdiv(F, fc), R // tm),
            in_specs=[pl.BlockSpec((tm, D), lambda j, i, te, nv: (i, 0)), wspec, wspec],
            out_specs=pl.BlockSpec((tm, fc), lambda j, i, te, nv: (i, j)),
            scratch_shapes=[pltpu.VMEM((D, fc), BF16)] * 2,
        ),
        out_shape=jax.ShapeDtypeStruct((R, F), BF16),
        compiler_params=_cparams(("arbitrary", "arbitrary")),
        name="ffn_up",
    )(tile_expert, n_valid, xs, w1, w3)


def _ffn_down_kernel(te_ref, nv_ref, h_ref, w2_ref, *rest, residual):
    if residual:
        r_ref, o_ref, w2_s = rest
    else:
        o_ref, w2_s = rest
    i = pl.program_id(1)

    @pl.when(_new_expert(te_ref, i))
    def _():
        w2_s[...] = w2_ref[...].astype(BF16)

    @pl.when(i < nv_ref[0])
    def _():
        y = jnp.dot(h_ref[...], w2_s[...], preferred_element_type=F32)
        o_ref[...] = r_ref[...] + y if residual else y

    @pl.when(i >= nv_ref[0])
    def _():
        o_ref[...] = jnp.zeros_like(o_ref)


def ffn_down(h, w2, tile_expert, n_valid, tm, tn, residual=None):
    R, F = h.shape
    D = w2.shape[-1]
    tile = pl.BlockSpec((tm, tn), lambda j, i, te, nv: (i, j))
    in_specs = [pl.BlockSpec((tm, F), lambda j, i, te, nv: (i, 0)),
                pl.BlockSpec((None, F, tn), lambda j, i, te, nv: (te[i], 0, j))]
    args = [h, w2]
    if residual is not None:
        in_specs.append(tile)
        args.append(residual)
    return pl.pallas_call(
        functools.partial(_ffn_down_kernel, residual=residual is not None),
        grid_spec=pltpu.PrefetchScalarGridSpec(
            num_scalar_prefetch=2,
            grid=(D // tn, R // tm),
            in_specs=in_specs,
            out_specs=tile,
            scratch_shapes=[pltpu.VMEM((F, tn), BF16)],
        ),
        out_shape=jax.ShapeDtypeStruct((R, D), F32),
        compiler_params=_cparams(("arbitrary", "arbitrary")),
        name="ffn_down",
    )(tile_expert, n_valid, *args)


def _router_kernel(x_ref, g_ref, wr_ref, br_ref, idx_ref, gate_ref):
    x = x_ref[...]
    ms = jnp.mean(x * x, axis=-1, keepdims=True)
    u = x * lax.rsqrt(ms + EPS) * g_ref[...]
    logits = lax.dot_general(wr_ref[...], u, (((1,), (1,)), ((), ())),
                             precision=lax.Precision.HIGHEST,
                             preferred_element_type=F32) + br_ref[...]
    n_e = logits.shape[0]
    eid = lax.broadcasted_iota(jnp.int32, logits.shape, 0)
    m1 = jnp.max(logits, axis=0, keepdims=True)
    i1 = jnp.min(jnp.where(logits == m1, eid, n_e), axis=0, keepdims=True)
    rest = jnp.where(eid == i1, -jnp.inf, logits)
    m2 = jnp.max(rest, axis=0, keepdims=True)
    i2 = jnp.min(jnp.where(rest == m2, eid, n_e), axis=0, keepdims=True)
    e2 = jnp.exp(m2 - m1)
    g1 = 1.0 / (1.0 + e2)
    idx_ref[...] = jnp.where(eid == 0, i1, i2)
    gate_ref[...] = jnp.where(eid == 0, g1, e2 * g1)


def router(x, g, w_router, b_router, tm):
    T, D = x.shape
    E = w_router.shape[-1]
    return pl.pallas_call(
        _router_kernel,
        grid=(T // tm,),
        in_specs=[pl.BlockSpec((tm, D), lambda i: (i, 0)),
                  pl.BlockSpec((1, D), lambda i: (0, 0)),
                  pl.BlockSpec((E, D), lambda i: (0, 0)),
                  pl.BlockSpec((E, 1), lambda i: (0, 0))],
        out_specs=[pl.BlockSpec((E, tm), lambda i: (0, i)),
                   pl.BlockSpec((E, tm), lambda i: (0, i))],
        out_shape=[jax.ShapeDtypeStruct((E, T), jnp.int32), jax.ShapeDtypeStruct((E, T), F32)],
        compiler_params=_cparams(("parallel",)),
        name="moe_router",
    )(x, g.reshape(1, D), w_router.T, b_router.reshape(E, 1))


def _row_copy(src_hbm, row, dst, r, sem):
    return pltpu.make_async_copy(src_hbm.at[pl.ds(row, 1), :], dst.at[pl.ds(r, 1), :], sem)


def _gather_norm_kernel(src_ref, x_hbm, g_ref, o_ref, buf, sem):
    tm = buf.shape[0]
    base = pl.program_id(0) * tm

    def start(r, c):
        _row_copy(x_hbm, src_ref[base + r], buf, r, sem).start()
        return c

    def wait(r, c):
        _row_copy(x_hbm, 0, buf, r, sem).wait()
        return c

    lax.fori_loop(0, tm, start, 0)
    lax.fori_loop(0, tm, wait, 0)
    x = buf[...]
    ms = jnp.mean(x * x, axis=-1, keepdims=True)
    o_ref[...] = (x * lax.rsqrt(ms + EPS) * g_ref[...]).astype(BF16)


def gather_norm(x, g, src, tm):
    T, D = x.shape
    R = src.shape[0]
    return pl.pallas_call(
        _gather_norm_kernel,
        grid_spec=pltpu.PrefetchScalarGridSpec(
            num_scalar_prefetch=1,
            grid=(R // tm,),
            in_specs=[pl.BlockSpec(memory_space=pl.ANY),
                      pl.BlockSpec((1, D), lambda i, s: (0, 0))],
            out_specs=pl.BlockSpec((tm, D), lambda i, s: (i, 0)),
            scratch_shapes=[pltpu.VMEM((tm, D), F32), pltpu.SemaphoreType.DMA(())],
        ),
        out_shape=jax.ShapeDtypeStruct((R, D), BF16),
        compiler_params=_cparams(("arbitrary",)),
        name="moe_gather",
    )(src, x, g.reshape(1, D))


def _combine_kernel(p0_ref, p1_ref, h_ref, gate_ref, y_hbm, o_ref, buf0, buf1, sem0, sem1):
    tm = buf0.shape[0]
    base = pl.program_id(0) * tm

    def start(r, c):
        _row_copy(y_hbm, p0_ref[base + r], buf0, r, sem0).start()
        _row_copy(y_hbm, p1_ref[base + r], buf1, r, sem1).start()
        return c

    def wait(r, c):
        _row_copy(y_hbm, 0, buf0, r, sem0).wait()
        _row_copy(y_hbm, 0, buf1, r, sem1).wait()
        return c

    lax.fori_loop(0, tm, start, 0)
    lax.fori_loop(0, tm, wait, 0)
    gate = gate_ref[...]
    o_ref[...] = h_ref[...] + gate[:, 0:1] * buf0[...] + gate[:, 1:2] * buf1[...]


def moe_combine(h, gates, ys, pos0, pos1, tm):
    T, D = h.shape
    return pl.pallas_call(
        _combine_kernel,
        grid_spec=pltpu.PrefetchScalarGridSpec(
            num_scalar_prefetch=2,
            grid=(T // tm,),
            in_specs=[pl.BlockSpec((tm, D), lambda i, a, b: (i, 0)),
                      pl.BlockSpec((tm, TOP_K), lambda i, a, b: (i, 0)),
                      pl.BlockSpec(memory_space=pl.ANY)],
            out_specs=pl.BlockSpec((tm, D), lambda i, a, b: (i, 0)),
            scratch_shapes=[pltpu.VMEM((tm, D), F32), pltpu.VMEM((tm, D), F32),
                            pltpu.SemaphoreType.DMA(()), pltpu.SemaphoreType.DMA(())],
        ),
        out_shape=jax.ShapeDtypeStruct((T, D), F32),
        compiler_params=_cparams(("arbitrary",)),
        name="moe_combine",
    )(pos0, pos1, h, gates, ys)


MOE_TM = 256


def moe_layout(idx, tm):
    T, K = idx.shape
    flat = idx.reshape(-1)
    onehot = (flat[:, None] == jnp.arange(N_EXPERTS)[None, :]).astype(jnp.int32)
    rank = jnp.take_along_axis(jnp.cumsum(onehot, axis=0) - onehot, flat[:, None], axis=1)[:, 0]
    counts = jnp.sum(onehot, axis=0)
    tiles = (counts + tm - 1) // tm
    tile_end = jnp.cumsum(tiles)
    start = (tile_end - tiles) * tm
    pos = start[flat] + rank
    n_tiles = (T * K) // tm + N_EXPERTS
    src = jnp.zeros((n_tiles * tm,), jnp.int32).at[pos].set(jnp.arange(T * K, dtype=jnp.int32) // K)
    owner = jnp.sum((jnp.arange(n_tiles)[:, None] >= tile_end[None, :]).astype(jnp.int32), axis=1)
    tile_expert = jnp.minimum(owner, N_EXPERTS - 1).astype(jnp.int32)
    n_valid = tile_end[-1:].astype(jnp.int32)
    return pos.reshape(T, K).astype(jnp.int32), src, tile_expert, n_valid


def moe_ffn(h, g, w_router, b_router, w1, w3, w2):
    T, D = h.shape
    idx_t, gate_t = router(h, g, w_router, b_router, tm=512)
    idx = idx_t[:TOP_K].T
    gates = gate_t[:TOP_K].T
    pos, src, tile_expert, n_valid = moe_layout(idx, MOE_TM)
    xs = gather_norm(h, g, src, MOE_TM)
    hid = ffn_up(xs, w1, w3, tile_expert, n_valid, MOE_TM, fc=1024)
    ys = ffn_down(hid, w2, tile_expert, n_valid, MOE_TM, tn=512)
    return moe_combine(h, gates, ys, pos[:, 0], pos[:, 1], MOE_TM)


def dense_ffn(h, g, w1, w3, w2):
    T, D = h.shape
    tm = MOE_TM
    xs = norm_cast(h, g, tm=512)
    tile_expert = jnp.zeros((T // tm,), jnp.int32)
    n_valid = jnp.full((1,), T // tm, jnp.int32)
    hid = ffn_up(xs, w1, w3, tile_expert, n_valid, tm, fc=512)
    return ffn_down(hid, w2, tile_expert, n_valid, tm, tn=512, residual=h)


def kernel(x_prompt, x_sample, cache_k, cache_v, state_conv, state_ssm_re, state_ssm_im,
           norm1_g, w_in, w_out, a_w_dw, a_b_dw, a_ln_g, a_ln_b, a_w_pw, b_q_g, b_k_g,
           c_a_re, c_a_im, c_log_dt, c_b_re, c_b_im, c_c_re, c_c_im, c_d, c_w_glu, c_b_glu,
           norm2_g, ffn_w1, ffn_w3, ffn_w2, moe_w_router, moe_b_router, moe_w1, moe_w3, moe_w2):
    n_batch, L, D = x_prompt.shape
    n_b, n_new, _ = x_sample.shape
    depth = w_in.shape[0]
    c_a = a_w_pw.shape[-1]
    c_c = c_w_glu.shape[-1]
    c_b = D - c_a - c_c
    n_heads = c_b // HEAD_DIM
    window = cache_k.shape[2]
    tp, ts = n_batch * L, n_b * n_new
    T = tp + ts
    q_col0, v_col0, u_col0 = 2 * c_a, 2 * c_a + 2 * c_b, 2 * c_a + 3 * c_b
    assert q_col0 == c_b and window == DILATIONS[-1] * DIL_STEPS and L == window
    tm_big = T // 8

    p = dict(a_w_dw=a_w_dw, a_b_dw=a_b_dw, a_ln_g=a_ln_g, a_ln_b=a_ln_b, a_w_pw=a_w_pw,
             c_a_re=c_a_re, c_a_im=c_a_im, c_log_dt=c_log_dt, c_b_re=c_b_re, c_b_im=c_b_im,
             c_c_re=c_c_re, c_c_im=c_c_im, c_d=c_d, c_w_glu=c_w_glu, c_b_glu=c_b_glu)

    x = jnp.concatenate([x_prompt.reshape(tp, D),
                         x_sample.transpose(1, 0, 2).reshape(ts, D)], axis=0)
    conv_past_tm = state_conv.transpose(0, 2, 1, 3)

    w_win = jnp.asarray(_sample_weights(n_new, window, np.arange(window), n_heads))
    w_new = jnp.asarray(_sample_weights(n_new, window, window + np.arange(NEW_ROWS_PAD), n_heads))
    pad_new = lambda a: jnp.pad(a, ((0, 0), (0, NEW_ROWS_PAD - n_new), (0, 0)))

    st = {k: [] for k in ('k_p', 'v_p', 'conv_p', 're_p', 'im_p', 'k_s', 'v_s', 'conv_s', 're_s', 'im_s')}
    for l in range(depth):
        proj = norm_matmul(x, norm1_g[l], w_in, l, tm=tm_big, tn=512)
        qn, kn = qk_norm(proj, b_q_g[l], b_k_g[l], c_b, tm=512)
        v = proj[:, v_col0:v_col0 + c_b]

        ya_p, conv_p = conv_prompt(proj, n_batch, L, c_a, p, l)
        ya_s, conv_s_tm = conv_sample(proj, tp, n_new, n_b, c_a, conv_past_tm, p, l)
        yb_p = attn_prompt(qn, kn, proj, n_batch, L, c_b, v_col0)
        bm = lambda a: a[tp:].reshape(n_new, n_b, c_b).transpose(1, 0, 2)
        k_new, v_new = bm(kn), bm(v)
        q_b = bm(qn).reshape(n_b, n_new * n_heads, HEAD_DIM)
        yb_s = attn_sample(q_b, pad_new(k_new), pad_new(v_new), cache_k, cache_v, l, w_win, w_new)
        yb_s = yb_s.reshape(n_b, n_new, c_b).transpose(1, 0, 2).reshape(ts, c_b)
        sp = s5_params(p, l)
        yc_p, re_p, im_p = s5_prompt(proj, n_batch, L, c_c, u_col0, sp, p, l)
        h0_re = state_ssm_re[l].reshape(n_b, -1)
        h0_im = state_ssm_im[l].reshape(n_b, -1)
        yc_s, re_s, im_s = s5_sample(proj, tp, n_new, n_b, c_c, u_col0, h0_re, h0_im, sp, p, l)

        ya = jnp.concatenate([ya_p, ya_s], axis=0)
        yb = jnp.concatenate([yb_p, yb_s], axis=0)
        yc = jnp.concatenate([yc_p, yc_s], axis=0)
        h = out_proj(ya, yb, yc, w_out, l, x, tm=tm_big, tn=512)

        j = l // 2
        if l % 2 == 0:
            x = dense_ffn(h, norm2_g[l], ffn_w1[j:j + 1], ffn_w3[j:j + 1], ffn_w2[j:j + 1])
        else:
            x = moe_ffn(h, norm2_g[l], moe_w_router[j], moe_b_router[j],
                        moe_w1[j], moe_w3[j], moe_w2[j])

        g_shape = state_ssm_re.shape[2:]
        st['k_p'].append(kn[:tp].reshape(n_batch, L, n_heads, HEAD_DIM))
        st['v_p'].append(v[:tp].reshape(n_batch, L, n_heads, HEAD_DIM))
        st['conv_p'].append(conv_p)
        st['re_p'].append(re_p[:, 0].reshape((n_batch,) + g_shape))
        st['im_p'].append(im_p[:, 0].reshape((n_batch,) + g_shape))
        st['k_s'].append(k_new.reshape(n_b, n_new, n_heads, HEAD_DIM))
        st['v_s'].append(v_new.reshape(n_b, n_new, n_heads, HEAD_DIM))
        st['conv_s'].append(conv_s_tm.transpose(1, 0, 2))
        st['re_s'].append(re_s.reshape((n_b,) + g_shape))
        st['im_s'].append(im_s.reshape((n_b,) + g_shape))

    y_p = x[:tp].reshape(n_batch, L, D)
    y_s = x[tp:].reshape(n_new, n_b, D).transpose(1, 0, 2)
    stk = lambda k: jnp.stack(st[k])
    return (y_p, y_s, stk('k_p'), stk('v_p'), stk('conv_p'), stk('re_p'), stk('im_p'),
            stk('k_s'), stk('v_s'), stk('conv_s'), stk('re_s'), stk('im_s'))
```

```python
import functools
import math

import numpy as np
import jax
import jax.numpy as jnp
from jax import lax
from jax.experimental import pallas as pl
from jax.experimental.pallas import tpu as pltpu

F32 = jnp.float32
BF16 = jnp.bfloat16
EPS = 1e-6
NEG = -1e30

HEAD_DIM = 64
ATTN_SCALE = HEAD_DIM ** -0.5
ATTN_BLOCK = 128
DILATIONS = (1, 4, 16)
DIL_STEPS = 128
CONV_WIDTH = 31
CONV_PAST = CONV_WIDTH - 1
SSM_GROUP = 16
SSM_STATE = 64
N_EXPERTS = 8
TOP_K = 2
LANES = 128
SUBLANES = 8
VMEM_LIMIT = 56 * 1024 * 1024


def _cparams(sem, vmem=VMEM_LIMIT):
    return pltpu.CompilerParams(dimension_semantics=sem, vmem_limit_bytes=vmem)


def _nt_dot(a, b):
    return lax.dot_general(a, b, (((1,), (1,)), ((), ())), preferred_element_type=F32)


def _sigmoid(x):
    return 1.0 / (1.0 + jnp.exp(-x))


def _silu(x):
    return x * _sigmoid(x)


def _gelu_tanh(x):
    c = math.sqrt(2.0 / math.pi)
    return 0.5 * x * (1.0 + jnp.tanh(c * (x + 0.044715 * (x * x * x))))


def _norm_mm_kernel(x_ref, g_ref, w_ref, o_ref, u_scr):
    @pl.when(pl.program_id(1) == 0)
    def _():
        x = x_ref[...]
        ms = jnp.mean(x * x, axis=-1, keepdims=True)
        u_scr[...] = (x * lax.rsqrt(ms + EPS) * g_ref[...]).astype(BF16)

    o_ref[...] = jnp.dot(u_scr[...], w_ref[...].astype(BF16), preferred_element_type=F32)


def norm_matmul(x, g, w_stack, layer, tm, tn):
    T, D = x.shape
    N = w_stack.shape[-1]
    return pl.pallas_call(
        _norm_mm_kernel,
        grid=(T // tm, N // tn),
        in_specs=[
            pl.BlockSpec((tm, D), lambda i, j: (i, 0)),
            pl.BlockSpec((1, D), lambda i, j: (0, 0)),
            pl.BlockSpec((None, D, tn), lambda i, j: (layer, 0, j)),
        ],
        out_specs=pl.BlockSpec((tm, tn), lambda i, j: (i, j)),
        out_shape=jax.ShapeDtypeStruct((T, N), F32),
        scratch_shapes=[pltpu.VMEM((tm, D), BF16)],
        compiler_params=_cparams(("parallel", "arbitrary")),
        name="norm_matmul",
    )(x, g.reshape(1, D), w_stack)


def _qknorm_kernel(q_ref, k_ref, gq_ref, gk_ref, qo_ref, ko_ref):
    tm = q_ref.shape[0]
    lane = lax.broadcasted_iota(jnp.int32, (tm, LANES), 1)
    head0 = lane < HEAD_DIM

    def norm(x, g):
        sq = x * x
        s0 = jnp.sum(jnp.where(head0, sq, 0.0), axis=-1, keepdims=True)
        s1 = jnp.sum(jnp.where(head0, 0.0, sq), axis=-1, keepdims=True)
        ms = jnp.where(head0, s0, s1) * (1.0 / HEAD_DIM)
        return x * lax.rsqrt(ms + EPS) * g

    for t in range(q_ref.shape[1] // LANES):
        sl = slice(t * LANES, (t + 1) * LANES)
        qo_ref[:, sl] = norm(q_ref[:, sl], gq_ref[...]) * ATTN_SCALE
        ko_ref[:, sl] = norm(k_ref[:, sl], gk_ref[...])


def qk_norm(proj, gq, gk, c_b, tm):
    T = proj.shape[0]
    gq2 = jnp.concatenate([gq, gq]).reshape(1, LANES)
    gk2 = jnp.concatenate([gk, gk]).reshape(1, LANES)
    qblk = 1
    return pl.pallas_call(
        _qknorm_kernel,
        grid=(T // tm,),
        in_specs=[
            pl.BlockSpec((tm, c_b), lambda i: (i, qblk)),
            pl.BlockSpec((tm, c_b), lambda i: (i, qblk + 1)),
            pl.BlockSpec((1, LANES), lambda i: (0, 0)),
            pl.BlockSpec((1, LANES), lambda i: (0, 0)),
        ],
        out_specs=[pl.BlockSpec((tm, c_b), lambda i: (i, 0)),
                   pl.BlockSpec((tm, c_b), lambda i: (i, 0))],
        out_shape=[jax.ShapeDtypeStruct((T, c_b), F32)] * 2,
        compiler_params=_cparams(("parallel",)),
        name="qk_norm",
    )(proj, proj, gq2, gk2)


def _attn_prompt_kernel(q_ref, k_ref, v_ref, o_ref,
                        q0_s, q1_s, k_s, v_s,
                        m1, l1, a1, m4, l4, a4, m16, l16, a16):
    L = q_ref.shape[0]
    B = ATTN_BLOCK
    head0 = lax.broadcasted_iota(jnp.int32, (B, LANES), 1) < HEAD_DIM
    qi = lax.broadcasted_iota(jnp.int32, (2 * B, B), 0) % B
    ki = lax.broadcasted_iota(jnp.int32, (2 * B, B), 1)
    tri_cur = ki <= qi
    tri_prev = ki >= qi

    stats = {1: (m1, l1, a1), 4: (m4, l4, a4), 16: (m16, l16, a16)}

    for d in DILATIONS:
        n = L // d
        nb = n // B
        m_s, l_s, a_s = stats[d]
        for r in range(d):
            src = pl.ds(r, n, stride=d) if d > 1 else slice(None)
            dst = slice(r * n, (r + 1) * n)
            hm = lax.broadcasted_iota(jnp.int32, (n, LANES), 1) < HEAD_DIM
            q = q_ref[src, :]
            q0_s[dst, :] = jnp.where(hm, q, 0.0).astype(BF16)
            q1_s[dst, :] = jnp.where(hm, 0.0, q).astype(BF16)
            k_s[dst, :] = k_ref[src, :].astype(BF16)
            v_s[dst, :] = v_ref[src, :].astype(BF16)

        def block(j, carry, nb=nb, m_s=m_s, l_s=l_s, a_s=a_s):
            cur = pl.ds(pl.multiple_of(j * B, B), B)
            qb = jnp.concatenate([q0_s[cur, :], q1_s[cur, :]], axis=0)
            s_c = jnp.where(tri_cur, _nt_dot(qb, k_s[cur, :]), NEG)
            m = jnp.max(s_c, axis=-1, keepdims=True)
            if nb > 1:
                prev = pl.ds(pl.multiple_of(jnp.maximum(j - 1, 0) * B, B), B)
                mask_prev = jnp.logical_and(tri_prev, (j % nb) > 0)
                s_p = jnp.where(mask_prev, _nt_dot(qb, k_s[prev, :]), NEG)
                m = jnp.maximum(m, jnp.max(s_p, axis=-1, keepdims=True))
            p_c = jnp.exp(s_c - m)
            den = jnp.sum(p_c, axis=-1, keepdims=True)
            pv = jnp.dot(p_c.astype(BF16), v_s[cur, :], preferred_element_type=F32)
            if nb > 1:
                p_p = jnp.exp(s_p - m)
                den = den + jnp.sum(p_p, axis=-1, keepdims=True)
                pv = pv + jnp.dot(p_p.astype(BF16), v_s[prev, :], preferred_element_type=F32)
            m_s[cur, :] = jnp.where(head0, m[:B], m[B:])
            l_s[cur, :] = jnp.where(head0, den[:B], den[B:])
            a_s[cur, :] = jnp.where(head0, pv[:B], pv[B:])
            return carry

        lax.fori_loop(0, L // B, block, 0, unroll=2)

    dmax = DILATIONS[-1]
    nrow = L // dmax
    for r in range(dmax):
        o1 = pl.ds(r, nrow, stride=dmax)
        o4 = pl.ds((r % 4) * (L // 4) + r // 4, nrow, stride=dmax // 4)
        o16 = slice(r * nrow, (r + 1) * nrow)
        mm1, mm4, mm16 = m1[o1, :], m4[o4, :], m16[o16, :]
        mx = jnp.maximum(jnp.maximum(mm1, mm4), mm16)
        w1, w4, w16 = jnp.exp(mm1 - mx), jnp.exp(mm4 - mx), jnp.exp(mm16 - mx)
        num = w1 * a1[o1, :] + w4 * a4[o4, :] + w16 * a16[o16, :]
        den = w1 * l1[o1, :] + w4 * l4[o4, :] + w16 * l16[o16, :]
        o_ref[o1, :] = num / den


def attn_prompt(qn, kn, proj, n_batch, L, c_b, v_col0):
    n_hp = c_b // LANES
    vblk = v_col0 // LANES
    stat = [pltpu.VMEM((L, LANES), F32)] * 9
    return pl.pallas_call(
        _attn_prompt_kernel,
        grid=(n_batch, n_hp),
        in_specs=[
            pl.BlockSpec((L, LANES), lambda b, h: (b, h)),
            pl.BlockSpec((L, LANES), lambda b, h: (b, h)),
            pl.BlockSpec((L, LANES), lambda b, h: (b, vblk + h)),
        ],
        out_specs=pl.BlockSpec((L, LANES), lambda b, h: (b, h)),
        out_shape=jax.ShapeDtypeStruct((n_batch * L, c_b), F32),
        scratch_shapes=[pltpu.VMEM((L, LANES), BF16)] * 4 + stat,
        compiler_params=_cparams(("parallel", "parallel")),
        name="attn_prompt",
    )(qn, kn, proj)


def _branch_multiplicity(dist):
    c = np.zeros(dist.shape, np.float32)
    for d in DILATIONS:
        c += ((dist >= 0) & (dist % d == 0) & (dist <= d * DIL_STEPS)).astype(np.float32)
    return c


NEW_ROWS_PAD = 16


def _sample_weights(n_new, past_len, pos, n_heads):
    s = np.repeat(np.arange(n_new), n_heads)[:, None]
    return _branch_multiplicity(past_len + s - np.asarray(pos)[None, :]).astype(np.float32)


def _attn_sample_kernel(q_ref, kt_ref, vt_ref, kn_ref, vn_ref, w_ref, wn_ref, o_ref):
    n_heads, hd, window = kt_ref.shape
    width = n_heads * hd
    q = q_ref[...]
    nq = q.shape[0]
    q2 = jnp.concatenate([q, q], axis=-1)
    qt = jnp.concatenate([q2] * (width // (2 * hd)), axis=-1)
    own_head = (lax.broadcasted_iota(jnp.int32, (nq, width), 0) % n_heads
                == lax.broadcasted_iota(jnp.int32, (nq, width), 1) // hd)
    qbd = jnp.where(own_head, qt, 0.0).astype(BF16)

    w, wn = w_ref[...], wn_ref[...]
    s_c = jnp.dot(qbd, kt_ref[...].reshape(width, window).astype(BF16),
                  preferred_element_type=F32)
    s_n = _nt_dot(qbd, kn_ref[...].astype(BF16))
    s_c = jnp.where(w > 0.0, s_c, NEG)
    s_n = jnp.where(wn > 0.0, s_n, NEG)
    m = jnp.maximum(jnp.max(s_c, axis=-1, keepdims=True), jnp.max(s_n, axis=-1, keepdims=True))
    p_c = w * jnp.exp(s_c - m)
    p_n = wn * jnp.exp(s_n - m)
    den = jnp.sum(p_c, axis=-1, keepdims=True) + jnp.sum(p_n, axis=-1, keepdims=True)
    full = (_nt_dot(p_c.astype(BF16), vt_ref[...].reshape(width, window).astype(BF16))
            + jnp.dot(p_n.astype(BF16), vn_ref[...].astype(BF16), preferred_element_type=F32))
    full = jnp.where(own_head, full, 0.0)
    acc = full[:, 0:LANES]
    for t in range(1, width // LANES):
        acc = acc + full[:, t * LANES:(t + 1) * LANES]
    o_ref[...] = (acc[:, :hd] + acc[:, hd:]) / den


def attn_sample(q_b, k_new, v_new, cache_k, cache_v, layer, w, wn):
    n_b, nq, _ = q_b.shape
    depth, _, window, n_heads, _ = cache_k.shape
    width = n_heads * HEAD_DIM
    kt = cache_k.transpose(0, 1, 3, 4, 2)
    vt = cache_v.transpose(0, 1, 3, 4, 2)
    cache_spec = pl.BlockSpec((None, None, n_heads, HEAD_DIM, window), lambda b: (layer, b, 0, 0, 0))
    new_spec = pl.BlockSpec((None, NEW_ROWS_PAD, width), lambda b: (b, 0, 0))
    q_spec = pl.BlockSpec((None, nq, HEAD_DIM), lambda b: (b, 0, 0))

    def const_spec(a):
        return pl.BlockSpec(a.shape, lambda b: (0, 0))

    return pl.pallas_call(
        _attn_sample_kernel,
        grid=(n_b,),
        in_specs=[q_spec, cache_spec, cache_spec, new_spec, new_spec, const_spec(w), const_spec(wn)],
        out_specs=q_spec,
        out_shape=jax.ShapeDtypeStruct((n_b, nq, HEAD_DIM), F32),
        compiler_params=_cparams(("parallel",)),
        name="attn_sample",
    )(q_b, kt, vt, k_new, v_new, w, wn)


def _layernorm_silu(y, g, b):
    mu = jnp.mean(y, axis=-1, keepdims=True)
    yc = y - mu
    var = jnp.mean(yc * yc, axis=-1, keepdims=True)
    return _silu(yc * lax.rsqrt(var + EPS) * g + b)


CONV_HALO = 32
CONV_ROWS = 32


def _conv_prompt_kernel(val_ref, gate_ref, wdw_ref, bdw_ref, lng_ref, lnb_ref, wpw_ref,
                        y_ref, st_ref, xp_s, y_s):
    c = pl.program_id(1)
    lc, ca = val_ref.shape
    off = CONV_HALO - CONV_PAST

    @pl.when(c == 0)
    def _():
        xp_s[0:CONV_HALO, :] = jnp.zeros((CONV_HALO, ca), F32)

    @pl.when(c > 0)
    def _():
        xp_s[0:CONV_HALO, :] = xp_s[lc:lc + CONV_HALO, :]

    xp_s[CONV_HALO:CONV_HALO + lc, :] = val_ref[...] * _sigmoid(gate_ref[...])

    for rb in range(lc // CONV_ROWS):
        for lb in range(ca // LANES):
            ls = slice(lb * LANES, (lb + 1) * LANES)
            acc = jnp.zeros((CONV_ROWS, LANES), F32)
            for j in range(CONV_WIDTH):
                r0 = rb * CONV_ROWS + off + j
                acc = acc + wdw_ref[j:j + 1, ls] * xp_s[r0:r0 + CONV_ROWS, ls]
            y_s[rb * CONV_ROWS:(rb + 1) * CONV_ROWS, ls] = acc

    z = _layernorm_silu(y_s[...] + bdw_ref[...], lng_ref[...], lnb_ref[...])
    y_ref[...] = jnp.dot(z.astype(BF16), wpw_ref[...].astype(BF16), preferred_element_type=F32)

    @pl.when(c == pl.num_programs(1) - 1)
    def _():
        st_ref[...] = xp_s[lc + off:lc + CONV_HALO, :]


def _layer_vec(a, layer):
    return a[layer].reshape(1, -1)


def conv_prompt(proj, n_batch, L, c_a, p, layer, lc=256):
    nch = L // lc
    vec = pl.BlockSpec((1, c_a), lambda b, c: (0, 0))
    return pl.pallas_call(
        _conv_prompt_kernel,
        grid=(n_batch, nch),
        in_specs=[
            pl.BlockSpec((lc, c_a), lambda b, c: (b * nch + c, 0)),
            pl.BlockSpec((lc, c_a), lambda b, c: (b * nch + c, 1)),
            pl.BlockSpec((None, CONV_WIDTH, c_a), lambda b, c: (layer, 0, 0)),
            vec, vec, vec,
            pl.BlockSpec((None, c_a, c_a), lambda b, c: (layer, 0, 0)),
        ],
        out_specs=[pl.BlockSpec((lc, c_a), lambda b, c: (b * nch + c, 0)),
                   pl.BlockSpec((None, CONV_PAST, c_a), lambda b, c: (b, 0, 0))],
        out_shape=[jax.ShapeDtypeStruct((n_batch * L, c_a), F32),
                   jax.ShapeDtypeStruct((n_batch, CONV_PAST, c_a), F32)],
        scratch_shapes=[pltpu.VMEM((lc + CONV_HALO, c_a), F32), pltpu.VMEM((lc, c_a), F32)],
        compiler_params=_cparams(("parallel", "arbitrary")),
        name="conv_prompt",
    )(proj, proj, p['a_w_dw'], _layer_vec(p['a_b_dw'], layer), _layer_vec(p['a_ln_g'], layer),
      _layer_vec(p['a_ln_b'], layer), p['a_w_pw'])


def _conv_sample_kernel(val_ref, gate_ref, past_ref, wdw_ref, bdw_ref, lng_ref, lnb_ref, wpw_ref,
                        y_ref, st_ref, y_s):
    n_b = past_ref.shape[1]
    n_new = val_ref.shape[0] // n_b
    g = val_ref[...] * _sigmoid(gate_ref[...])

    def xp(t):
        if t < CONV_PAST:
            return past_ref[t]
        return g[(t - CONV_PAST) * n_b:(t - CONV_PAST + 1) * n_b, :]

    for s in range(n_new):
        acc = jnp.zeros_like(xp(0))
        for j in range(CONV_WIDTH):
            acc = acc + wdw_ref[j:j + 1, :] * xp(s + j)
        y_s[s * n_b:(s + 1) * n_b, :] = acc
    z = _layernorm_silu(y_s[...] + bdw_ref[...], lng_ref[...], lnb_ref[...])
    y_ref[...] = jnp.dot(z.astype(BF16), wpw_ref[...].astype(BF16), preferred_element_type=F32)
    for t in range(CONV_PAST):
        st_ref[t] = xp(t + n_new)


def conv_sample(proj, row0, n_new, n_b, c_a, past_tm, p, layer):
    ts = n_new * n_b
    vec = pl.BlockSpec((1, c_a), lambda i: (0, 0))
    return pl.pallas_call(
        _conv_sample_kernel,
        grid=(1,),
        in_specs=[
            pl.BlockSpec((ts, c_a), lambda i: (row0 // ts, 0)),
            pl.BlockSpec((ts, c_a), lambda i: (row0 // ts, 1)),
            pl.BlockSpec((None, CONV_PAST, n_b, c_a), lambda i: (layer, 0, 0, 0)),
            pl.BlockSpec((None, CONV_WIDTH, c_a), lambda i: (layer, 0, 0)),
            vec, vec, vec,
            pl.BlockSpec((None, c_a, c_a), lambda i: (layer, 0, 0)),
        ],
        out_specs=[pl.BlockSpec((ts, c_a), lambda i: (0, 0)),
                   pl.BlockSpec((CONV_PAST, n_b, c_a), lambda i: (0, 0, 0))],
        out_shape=[jax.ShapeDtypeStruct((ts, c_a), F32),
                   jax.ShapeDtypeStruct((CONV_PAST, n_b, c_a), F32)],
        scratch_shapes=[pltpu.VMEM((ts, c_a), F32)],
        compiler_params=_cparams(("arbitrary",)),
        name="conv_sample",
    )(proj, proj, past_tm, p['a_w_dw'], _layer_vec(p['a_b_dw'], layer),
      _layer_vec(p['a_ln_g'], layer), _layer_vec(p['a_ln_b'], layer), p['a_w_pw'])


def _s5_param_kernel(are_ref, aim_ref, ldt_ref, bre_ref, bim_ref,
                     pre_ref, pim_ref, bbre_ref, bbim_ref):
    a_re, a_im = are_ref[...], aim_ref[...]
    dt = jnp.exp(ldt_ref[...])
    mag = jnp.exp(a_re * dt)
    ab_re, ab_im = mag * jnp.cos(a_im * dt), mag * jnp.sin(a_im * dt)
    nr, ni = ab_re - 1.0, ab_im
    inv = 1.0 / (a_re * a_re + a_im * a_im)
    f_re = (nr * a_re + ni * a_im) * inv
    f_im = (ni * a_re - nr * a_im) * inv
    b_re, b_im = bre_ref[...], bim_ref[...]
    bbre_ref[...] = f_re * b_re - f_im * b_im
    bbim_ref[...] = f_re * b_im + f_im * b_re
    pr, pi = ab_re, ab_im
    pre_ref[0] = pr
    pim_ref[0] = pi
    for k in range(1, SUBLANES):
        pr, pi = pr * ab_re - pi * ab_im, pr * ab_im + pi * ab_re
        pre_ref[k] = pr
        pim_ref[k] = pi


def s5_params(p, layer):
    a_re, a_im = p['c_a_re'][layer], p['c_a_im'][layer]
    G, N = a_re.shape
    C = SSM_GROUP
    b_re_t = p['c_b_re'][layer].transpose(0, 2, 1)
    b_im_t = p['c_b_im'][layer].transpose(0, 2, 1)
    pre, pim, bbre, bbim = pl.pallas_call(
        _s5_param_kernel,
        out_shape=[jax.ShapeDtypeStruct((SUBLANES, G, 1, N), F32)] * 2
        + [jax.ShapeDtypeStruct((G, C, N), F32)] * 2,
        name="s5_params",
    )(a_re.reshape(G, 1, N), a_im.reshape(G, 1, N), p['c_log_dt'][layer].reshape(G, 1, 1),
      b_re_t, b_im_t)
    S = G * N
    pre, pim = pre.reshape(SUBLANES, S), pim.reshape(SUBLANES, S)
    t = np.arange(SUBLANES)[:, None]
    tabs = []
    for k in (1, 2, 4):
        keep = jnp.asarray(t >= k)
        tabs += [jnp.where(keep, pre[k - 1][None, :], 0.0), jnp.where(keep, pim[k - 1][None, :], 0.0)]
    tabs += [pre, pim]
    tab = jnp.stack(tabs)
    eye = jnp.eye(G, dtype=F32)

    def in_proj(bb):
        return (eye[:, None, :, None] * bb[:, :, None, :]).reshape(G * C, S).astype(BF16)

    def out_proj(c):
        return (eye[:, None, :, None] * c.transpose(0, 2, 1)[:, :, None, :]).reshape(S, G * C)

    c_cat = jnp.concatenate([out_proj(p['c_c_re'][layer]), -out_proj(p['c_c_im'][layer])]).astype(BF16)
    return dict(tab=tab, bb_re=in_proj(bbre), bb_im=in_proj(bbim), c_cat=c_cat,
                ab_re=pre[0:1], ab_im=pim[0:1])


def _s5_readout(u, xr, xi, ccat_ref, d_ref, wglu_ref, bglu_ref):
    S = xr.shape[-1]
    y = (jnp.dot(xr.astype(BF16), ccat_ref[0:S, :], preferred_element_type=F32)
         + jnp.dot(xi.astype(BF16), ccat_ref[S:2 * S, :], preferred_element_type=F32)
         + d_ref[...] * u)
    z = _gelu_tanh(y)
    gate = jnp.dot(z.astype(BF16), wglu_ref[...].astype(BF16), preferred_element_type=F32)
    return z * _sigmoid(gate + bglu_ref[...])


S5_LANE_GROUP = 256


def _s5_prompt_kernel(u_ref, bbre_ref, bbim_ref, tab_ref, ccat_ref, d_ref, wglu_ref, bglu_ref,
                      y_ref, hre_ref, him_ref, xr_s, xi_s, h_s):
    c = pl.program_id(1)
    tc = u_ref.shape[0]
    S = xr_s.shape[1]

    @pl.when(c == 0)
    def _():
        h_s[...] = jnp.zeros_like(h_s)

    u = u_ref[...]
    ub = u.astype(BF16)
    xr_s[...] = jnp.dot(ub, bbre_ref[...], preferred_element_type=F32)
    xi_s[...] = jnp.dot(ub, bbim_ref[...], preferred_element_type=F32)

    for jg in range(S // S5_LANE_GROUP):
        ls = slice(jg * S5_LANE_GROUP, (jg + 1) * S5_LANE_GROUP)
        levels = [(k, tab_ref[2 * i, :, ls], tab_ref[2 * i + 1, :, ls])
                  for i, k in enumerate((1, 2, 4))]
        cr, ci = tab_ref[6, :, ls], tab_ref[7, :, ls]

        def tile(i, carry, ls=ls, levels=levels, cr=cr, ci=ci):
            hr, hi = carry
            rows = pl.ds(pl.multiple_of(i * SUBLANES, SUBLANES), SUBLANES)
            br, bi = xr_s[rows, ls], xi_s[rows, ls]
            for k, pr, pi in levels:
                sr, si = pltpu.roll(br, k, axis=0), pltpu.roll(bi, k, axis=0)
                br, bi = br + pr * sr - pi * si, bi + pr * si + pi * sr
            br, bi = br + cr * hr - ci * hi, bi + cr * hi + ci * hr
            xr_s[rows, ls] = br
            xi_s[rows, ls] = bi
            last = slice(SUBLANES - 1, SUBLANES)
            return (jnp.broadcast_to(br[last, :], br.shape), jnp.broadcast_to(bi[last, :], bi.shape))

        hr, hi = lax.fori_loop(0, tc // SUBLANES, tile, (h_s[0, :, ls], h_s[1, :, ls]))
        h_s[0, :, ls] = hr
        h_s[1, :, ls] = hi

    y_ref[...] = _s5_readout(u, xr_s[...], xi_s[...], ccat_ref, d_ref, wglu_ref, bglu_ref)
    hre_ref[...] = h_s[0]
    him_ref[...] = h_s[1]


def s5_prompt(proj, n_batch, L, c_c, u_col0, sp, p, layer, tc=256):
    nch = L // tc
    S = sp['tab'].shape[-1]
    ublk = u_col0 // c_c

    def full(a):
        return pl.BlockSpec(a.shape, lambda b, c: (0,) * a.ndim)

    d = p['c_d'][layer].reshape(1, c_c)
    bglu = p['c_b_glu'][layer].reshape(1, c_c)
    return pl.pallas_call(
        _s5_prompt_kernel,
        grid=(n_batch, nch),
        in_specs=[
            pl.BlockSpec((tc, c_c), lambda b, c: (b * nch + c, ublk)),
            full(sp['bb_re']), full(sp['bb_im']), full(sp['tab']), full(sp['c_cat']), full(d),
            pl.BlockSpec((None, c_c, c_c), lambda b, c: (layer, 0, 0)),
            full(bglu),
        ],
        out_specs=[pl.BlockSpec((tc, c_c), lambda b, c: (b * nch + c, 0)),
                   pl.BlockSpec((None, SUBLANES, S), lambda b, c: (b, 0, 0)),
                   pl.BlockSpec((None, SUBLANES, S), lambda b, c: (b, 0, 0))],
        out_shape=[jax.ShapeDtypeStruct((n_batch * L, c_c), F32),
                   jax.ShapeDtypeStruct((n_batch, SUBLANES, S), F32),
                   jax.ShapeDtypeStruct((n_batch, SUBLANES, S), F32)],
        scratch_shapes=[pltpu.VMEM((tc, S), F32), pltpu.VMEM((tc, S), F32),
                        pltpu.VMEM((2, SUBLANES, S), F32)],
        compiler_params=_cparams(("parallel", "arbitrary")),
        name="s5_prompt",
    )(proj, sp['bb_re'], sp['bb_im'], sp['tab'], sp['c_cat'], d, p['c_w_glu'], bglu)


def _s5_sample_kernel(u_ref, h0re_ref, h0im_ref, bbre_ref, bbim_ref, abre_ref, abim_ref,
                      ccat_ref, d_ref, wglu_ref, bglu_ref,
                      y_ref, hre_ref, him_ref, xr_s, xi_s):
    n_b = h0re_ref.shape[0]
    n_new = u_ref.shape[0] // n_b
    u = u_ref[...]
    ub = u.astype(BF16)
    xr_s[...] = jnp.dot(ub, bbre_ref[...], preferred_element_type=F32)
    xi_s[...] = jnp.dot(ub, bbim_ref[...], preferred_element_type=F32)
    ar, ai = abre_ref[...], abim_ref[...]
    hr, hi = h0re_ref[...], h0im_ref[...]
    for s in range(n_new):
        rows = slice(s * n_b, (s + 1) * n_b)
        hr, hi = ar * hr - ai * hi + xr_s[rows, :], ar * hi + ai * hr + xi_s[rows, :]
        xr_s[rows, :] = hr
        xi_s[rows, :] = hi
    y_ref[...] = _s5_readout(u, xr_s[...], xi_s[...], ccat_ref, d_ref, wglu_ref, bglu_ref)
    hre_ref[...] = hr
    him_ref[...] = hi


def s5_sample(proj, row0, n_new, n_b, c_c, u_col0, h0_re, h0_im, sp, p, layer):
    ts = n_new * n_b
    S = sp['tab'].shape[-1]

    def full(a):
        return pl.BlockSpec(a.shape, lambda i: (0,) * a.ndim)

    d = p['c_d'][layer].reshape(1, c_c)
    bglu = p['c_b_glu'][layer].reshape(1, c_c)
    return pl.pallas_call(
        _s5_sample_kernel,
        grid=(1,),
        in_specs=[
            pl.BlockSpec((ts, c_c), lambda i: (row0 // ts, u_col0 // c_c)),
            full(h0_re), full(h0_im), full(sp['bb_re']), full(sp['bb_im']),
            full(sp['ab_re']), full(sp['ab_im']), full(sp['c_cat']), full(d),
            pl.BlockSpec((None, c_c, c_c), lambda i: (layer, 0, 0)),
            full(bglu),
        ],
        out_specs=[pl.BlockSpec((ts, c_c), lambda i: (0, 0)),
                   pl.BlockSpec((n_b, S), lambda i: (0, 0)),
                   pl.BlockSpec((n_b, S), lambda i: (0, 0))],
        out_shape=[jax.ShapeDtypeStruct((ts, c_c), F32),
                   jax.ShapeDtypeStruct((n_b, S), F32),
                   jax.ShapeDtypeStruct((n_b, S), F32)],
        scratch_shapes=[pltpu.VMEM((ts, S), F32), pltpu.VMEM((ts, S), F32)],
        compiler_params=_cparams(("arbitrary",)),
        name="s5_sample",
    )(proj, h0_re, h0_im, sp['bb_re'], sp['bb_im'], sp['ab_re'], sp['ab_im'], sp['c_cat'], d,
      p['c_w_glu'], bglu)


def _out_proj_kernel(yap_ref, ybp_ref, ycp_ref, yas_ref, ybs_ref, ycs_ref, w_ref, x_ref,
                     o_ref, cat_s, *, n_prompt_tiles):
    def fill(ya_ref, yb_ref, yc_ref):
        ca, cb = ya_ref.shape[1], yb_ref.shape[1]
        cat_s[:, 0:ca] = ya_ref[...].astype(BF16)
        cat_s[:, ca:ca + cb] = yb_ref[...].astype(BF16)
        cat_s[:, ca + cb:] = yc_ref[...].astype(BF16)

    first_col = pl.program_id(1) == 0
    is_prompt = pl.program_id(0) < n_prompt_tiles

    @pl.when(jnp.logical_and(first_col, is_prompt))
    def _():
        fill(yap_ref, ybp_ref, ycp_ref)

    @pl.when(jnp.logical_and(first_col, jnp.logical_not(is_prompt)))
    def _():
        fill(yas_ref, ybs_ref, ycs_ref)

    o_ref[...] = x_ref[...] + jnp.dot(cat_s[...], w_ref[...].astype(BF16),
                                      preferred_element_type=F32)


def out_proj(prompt_parts, sample_parts, w_stack, layer, x, tm, tn):
    T, D = x.shape
    dm = w_stack.shape[1]
    n_p = prompt_parts[0].shape[0] // tm
    assert prompt_parts[0].shape[0] % tm == 0 and sample_parts[0].shape[0] % tm == 0

    def prompt_spec(a):
        return pl.BlockSpec((tm, a.shape[1]), lambda i, j: (jnp.minimum(i, n_p - 1), 0))

    def sample_spec(a):
        return pl.BlockSpec((tm, a.shape[1]), lambda i, j: (jnp.maximum(i - n_p, 0), 0))

    return pl.pallas_call(
        functools.partial(_out_proj_kernel, n_prompt_tiles=n_p),
        grid=(T // tm, D // tn),
        in_specs=[prompt_spec(a) for a in prompt_parts] + [sample_spec(a) for a in sample_parts]
        + [pl.BlockSpec((None, dm, tn), lambda i, j: (layer, 0, j)),
           pl.BlockSpec((tm, tn), lambda i, j: (i, j))],
        out_specs=pl.BlockSpec((tm, tn), lambda i, j: (i, j)),
        out_shape=jax.ShapeDtypeStruct((T, D), F32),
        scratch_shapes=[pltpu.VMEM((tm, dm), BF16)],
        compiler_params=_cparams(("parallel", "arbitrary")),
        name="out_proj",
    )(*prompt_parts, *sample_parts, w_stack, x)


def _norm_cast_kernel(x_ref, g_ref, o_ref):
    x = x_ref[...]
    ms = jnp.mean(x * x, axis=-1, keepdims=True)
    o_ref[...] = (x * lax.rsqrt(ms + EPS) * g_ref[...]).astype(BF16)


def norm_cast(x, g, tm):
    T, D = x.shape
    return pl.pallas_call(
        _norm_cast_kernel,
        grid=(T // tm,),
        in_specs=[pl.BlockSpec((tm, D), lambda i: (i, 0)), pl.BlockSpec((1, D), lambda i: (0, 0))],
        out_specs=pl.BlockSpec((tm, D), lambda i: (i, 0)),
        out_shape=jax.ShapeDtypeStruct((T, D), BF16),
        compiler_params=_cparams(("parallel",)),
        name="norm_cast",
    )(x, g.reshape(1, D))


def _new_expert(te_ref, i):
    prev = te_ref[jnp.maximum(i - 1, 0)]
    return jnp.logical_or(i == 0, te_ref[i] != prev)


def _ffn_up_kernel(te_ref, nv_ref, x_ref, w1_ref, w3_ref, h_ref, w1_s, w3_s):
    i = pl.program_id(1)

    @pl.when(_new_expert(te_ref, i))
    def _():
        w1_s[...] = w1_ref[...].astype(BF16)
        w3_s[...] = w3_ref[...].astype(BF16)

    @pl.when(i < nv_ref[0])
    def _():
        x = x_ref[...]
        a = jnp.dot(x, w1_s[...], preferred_element_type=F32)
        b = jnp.dot(x, w3_s[...], preferred_element_type=F32)
        h_ref[...] = (_silu(a) * b).astype(BF16)

    @pl.when(i >= nv_ref[0])
    def _():
        h_ref[...] = jnp.zeros_like(h_ref)


def ffn_up(xs, w1, w3, tile_expert, n_valid, tm, fc):
    R, D = xs.shape
    F = w1.shape[-1]
    wspec = pl.BlockSpec((None, D, fc), lambda j, i, te, nv: (te[i], 0, j))
    return pl.pallas_call(
        _ffn_up_kernel,
        grid_spec=pltpu.PrefetchScalarGridSpec(
            num_scalar_prefetch=2,
            grid=(pl.cdiv(F, fc), R // tm),
            in_specs=[pl.BlockSpec((tm, D), lambda j, i, te, nv: (i, 0)), wspec, wspec],
            out_specs=pl.BlockSpec((tm, fc), lambda j, i, te, nv: (i, j)),
            scratch_shapes=[pltpu.VMEM((D, fc), BF16)] * 2,
        ),
        out_shape=jax.ShapeDtypeStruct((R, F), BF16),
        compiler_params=_cparams(("arbitrary", "arbitrary")),
        name="ffn_up",
    )(tile_expert, n_valid, xs, w1, w3)


def _ffn_down_kernel(te_ref, nv_ref, h_ref, w2_ref, *rest, residual):
    if residual:
        r_ref, o_ref, w2_s = rest
    else:
        o_ref, w2_s = rest
    i = pl.program_id(1)

    @pl.when(_new_expert(te_ref, i))
    def _():
        w2_s[...] = w2_ref[...].astype(BF16)

    @pl.when(i < nv_ref[0])
    def _():
        y = jnp.dot(h_ref[...], w2_s[...], preferred_element_type=F32)
        o_ref[...] = r_ref[...] + y if residual else y

    @pl.when(i >= nv_ref[0])
    def _():
        o_ref[...] = jnp.zeros_like(o_ref)


def ffn_down(h, w2, tile_expert, n_valid, tm, tn, residual=None):
    R, F = h.shape
    D = w2.shape[-1]
    tile = pl.BlockSpec((tm, tn), lambda j, i, te, nv: (i, j))
    in_specs = [pl.BlockSpec((tm, F), lambda j, i, te, nv: (i, 0)),
                pl.BlockSpec((None, F, tn), lambda j, i, te, nv: (te[i], 0, j))]
    args = [h, w2]
    if residual is not None:
        in_specs.append(tile)
        args.append(residual)
    return pl.pallas_call(
        functools.partial(_ffn_down_kernel, residual=residual is not None),
        grid_spec=pltpu.PrefetchScalarGridSpec(
            num_scalar_prefetch=2,
            grid=(D // tn, R // tm),
            in_specs=in_specs,
            out_specs=tile,
            scratch_shapes=[pltpu.VMEM((F, tn), BF16)],
        ),
        out_shape=jax.ShapeDtypeStruct((R, D), F32),
        compiler_params=_cparams(("arbitrary", "arbitrary")),
        name="ffn_down",
    )(tile_expert, n_valid, *args)


def _router_kernel(x_ref, g_ref, wr_ref, br_ref, idx_ref, gate_ref):
    x = x_ref[...]
    ms = jnp.mean(x * x, axis=-1, keepdims=True)
    u = x * lax.rsqrt(ms + EPS) * g_ref[...]
    logits = lax.dot_general(wr_ref[...], u, (((1,), (1,)), ((), ())),
                             precision=lax.Precision.HIGHEST,
                             preferred_element_type=F32) + br_ref[...]
    n_e = logits.shape[0]
    eid = lax.broadcasted_iota(jnp.int32, logits.shape, 0)
    m1 = jnp.max(logits, axis=0, keepdims=True)
    i1 = jnp.min(jnp.where(logits == m1, eid, n_e), axis=0, keepdims=True)
    rest = jnp.where(eid == i1, -jnp.inf, logits)
    m2 = jnp.max(rest, axis=0, keepdims=True)
    i2 = jnp.min(jnp.where(rest == m2, eid, n_e), axis=0, keepdims=True)
    e2 = jnp.exp(m2 - m1)
    g1 = 1.0 / (1.0 + e2)
    idx_ref[...] = jnp.where(eid == 0, i1, i2)
    gate_ref[...] = jnp.where(eid == 0, g1, e2 * g1)


def router(x, g, w_router, b_router, tm):
    T, D = x.shape
    E = w_router.shape[-1]
    return pl.pallas_call(
        _router_kernel,
        grid=(T // tm,),
        in_specs=[pl.BlockSpec((tm, D), lambda i: (i, 0)),
                  pl.BlockSpec((1, D), lambda i: (0, 0)),
                  pl.BlockSpec((E, D), lambda i: (0, 0)),
                  pl.BlockSpec((E, 1), lambda i: (0, 0))],
        out_specs=[pl.BlockSpec((E, tm), lambda i: (0, i)),
                   pl.BlockSpec((E, tm), lambda i: (0, i))],
        out_shape=[jax.ShapeDtypeStruct((E, T), jnp.int32), jax.ShapeDtypeStruct((E, T), F32)],
        compiler_params=_cparams(("parallel",)),
        name="moe_router",
    )(x, g.reshape(1, D), w_router.T, b_router.reshape(E, 1))


def _row_copy(src_hbm, row, dst, r, sem):
    return pltpu.make_async_copy(src_hbm.at[pl.ds(row, 1), :], dst.at[pl.ds(r, 1), :], sem)


GATHER_UNROLL = 8


def _rows_loop(tm, fn):
    def body(it, c):
        for u in range(GATHER_UNROLL):
            fn(it * GATHER_UNROLL + u)
        return c

    lax.fori_loop(0, tm // GATHER_UNROLL, body, 0)


def _gather_norm_kernel(src_ref, x_hbm, g_ref, o_ref, buf, sem):
    tm = buf.shape[1]
    i = pl.program_id(0)

    def fetch(tile, slot):
        _rows_loop(tm, lambda r: _row_copy(x_hbm, src_ref[tile * tm + r], buf.at[slot], r,
                                           sem.at[slot]).start())

    @pl.when(i == 0)
    def _():
        fetch(0, 0)

    @pl.when(i + 1 < pl.num_programs(0))
    def _():
        fetch(i + 1, (i + 1) % 2)

    slot = i % 2
    _rows_loop(tm, lambda r: _row_copy(x_hbm, 0, buf.at[slot], r, sem.at[slot]).wait())
    x = buf[slot]
    ms = jnp.mean(x * x, axis=-1, keepdims=True)
    o_ref[...] = (x * lax.rsqrt(ms + EPS) * g_ref[...]).astype(BF16)


def gather_norm(x, g, src, tm):
    T, D = x.shape
    R = src.shape[0]
    return pl.pallas_call(
        _gather_norm_kernel,
        grid_spec=pltpu.PrefetchScalarGridSpec(
            num_scalar_prefetch=1,
            grid=(R // tm,),
            in_specs=[pl.BlockSpec(memory_space=pl.ANY),
                      pl.BlockSpec((1, D), lambda i, s: (0, 0))],
            out_specs=pl.BlockSpec((tm, D), lambda i, s: (i, 0)),
            scratch_shapes=[pltpu.VMEM((2, tm, D), F32), pltpu.SemaphoreType.DMA((2,))],
        ),
        out_shape=jax.ShapeDtypeStruct((R, D), BF16),
        compiler_params=_cparams(("arbitrary",)),
        name="moe_gather",
    )(src, x, g.reshape(1, D))


def _combine_kernel(p0_ref, p1_ref, h_ref, gate_ref, y_hbm, o_ref, buf, sem):
    tm = buf.shape[2]
    i = pl.program_id(0)

    def fetch(tile, slot):
        def one(r):
            _row_copy(y_hbm, p0_ref[tile * tm + r], buf.at[slot, 0], r, sem.at[slot, 0]).start()
            _row_copy(y_hbm, p1_ref[tile * tm + r], buf.at[slot, 1], r, sem.at[slot, 1]).start()

        _rows_loop(tm, one)

    @pl.when(i == 0)
    def _():
        fetch(0, 0)

    @pl.when(i + 1 < pl.num_programs(0))
    def _():
        fetch(i + 1, (i + 1) % 2)

    slot = i % 2

    def wait(r):
        _row_copy(y_hbm, 0, buf.at[slot, 0], r, sem.at[slot, 0]).wait()
        _row_copy(y_hbm, 0, buf.at[slot, 1], r, sem.at[slot, 1]).wait()

    _rows_loop(tm, wait)
    gate = gate_ref[...]
    o_ref[...] = h_ref[...] + gate[:, 0:1] * buf[slot, 0] + gate[:, 1:2] * buf[slot, 1]


def moe_combine(h, gates, ys, pos0, pos1, tm):
    T, D = h.shape
    return pl.pallas_call(
        _combine_kernel,
        grid_spec=pltpu.PrefetchScalarGridSpec(
            num_scalar_prefetch=2,
            grid=(T // tm,),
            in_specs=[pl.BlockSpec((tm, D), lambda i, a, b: (i, 0)),
                      pl.BlockSpec((tm, TOP_K), lambda i, a, b: (i, 0)),
                      pl.BlockSpec(memory_space=pl.ANY)],
            out_specs=pl.BlockSpec((tm, D), lambda i, a, b: (i, 0)),
            scratch_shapes=[pltpu.VMEM((2, TOP_K, tm, D), F32), pltpu.SemaphoreType.DMA((2, TOP_K))],
        ),
        out_shape=jax.ShapeDtypeStruct((T, D), F32),
        compiler_params=_cparams(("arbitrary",)),
        name="moe_combine",
    )(pos0, pos1, h, gates, ys)


MOE_TM = 512
MOE_TM_DOWN = 256
GATHER_TM = 256


def moe_layout(idx, tm):
    T, K = idx.shape
    flat = idx.reshape(-1)
    onehot = (flat[:, None] == jnp.arange(N_EXPERTS)[None, :]).astype(jnp.int32)
    rank = jnp.take_along_axis(jnp.cumsum(onehot, axis=0) - onehot, flat[:, None], axis=1)[:, 0]
    counts = jnp.sum(onehot, axis=0)
    tiles = (counts + tm - 1) // tm
    tile_end = jnp.cumsum(tiles)
    start = (tile_end - tiles) * tm
    pos = start[flat] + rank
    n_tiles = (T * K) // tm + N_EXPERTS
    src = jnp.zeros((n_tiles * tm,), jnp.int32).at[pos].set(jnp.arange(T * K, dtype=jnp.int32) // K)
    owner = jnp.sum((jnp.arange(n_tiles)[:, None] >= tile_end[None, :]).astype(jnp.int32), axis=1)
    tile_expert = jnp.minimum(owner, N_EXPERTS - 1).astype(jnp.int32)
    n_valid = tile_end[-1:].astype(jnp.int32)
    return pos.reshape(T, K).astype(jnp.int32), src, tile_expert, n_valid


def moe_ffn(h, g, w_router, b_router, w1, w3, w2):
    T, D = h.shape
    idx_t, gate_t = router(h, g, w_router, b_router, tm=512)
    idx = idx_t[:TOP_K].T
    gates = gate_t[:TOP_K].T
    pos, src, tile_expert, n_valid = moe_layout(idx, MOE_TM)
    xs = gather_norm(h, g, src, GATHER_TM)
    hid = ffn_up(xs, w1, w3, tile_expert, n_valid, MOE_TM, fc=1024)
    split = MOE_TM // MOE_TM_DOWN
    ys = ffn_down(hid, w2, jnp.repeat(tile_expert, split), n_valid * split, MOE_TM_DOWN, tn=512)
    return moe_combine(h, gates, ys, pos[:, 0], pos[:, 1], GATHER_TM)


def dense_ffn(h, g, w1, w3, w2):
    T, D = h.shape
    tm_up, tm_down = T // 8, T // 16

    def one_group(tm):
        return jnp.zeros((T // tm,), jnp.int32), jnp.full((1,), T // tm, jnp.int32)

    xs = norm_cast(h, g, tm=512)
    hid = ffn_up(xs, w1, w3, *one_group(tm_up), tm_up, fc=512)
    return ffn_down(hid, w2, *one_group(tm_down), tm_down, tn=512, residual=h)


def kernel(x_prompt, x_sample, cache_k, cache_v, state_conv, state_ssm_re, state_ssm_im,
           norm1_g, w_in, w_out, a_w_dw, a_b_dw, a_ln_g, a_ln_b, a_w_pw, b_q_g, b_k_g,
           c_a_re, c_a_im, c_log_dt, c_b_re, c_b_im, c_c_re, c_c_im, c_d, c_w_glu, c_b_glu,
           norm2_g, ffn_w1, ffn_w3, ffn_w2, moe_w_router, moe_b_router, moe_w1, moe_w3, moe_w2):
    n_batch, L, D = x_prompt.shape
    n_b, n_new, _ = x_sample.shape
    depth = w_in.shape[0]
    c_a = a_w_pw.shape[-1]
    c_c = c_w_glu.shape[-1]
    c_b = D - c_a - c_c
    n_heads = c_b // HEAD_DIM
    window = cache_k.shape[2]
    tp, ts = n_batch * L, n_b * n_new
    T = tp + ts
    q_col0, v_col0, u_col0 = 2 * c_a, 2 * c_a + 2 * c_b, 2 * c_a + 3 * c_b
    assert q_col0 == c_b and window == DILATIONS[-1] * DIL_STEPS and L == window
    tm_big = T // 8

    p = dict(a_w_dw=a_w_dw, a_b_dw=a_b_dw, a_ln_g=a_ln_g, a_ln_b=a_ln_b, a_w_pw=a_w_pw,
             c_a_re=c_a_re, c_a_im=c_a_im, c_log_dt=c_log_dt, c_b_re=c_b_re, c_b_im=c_b_im,
             c_c_re=c_c_re, c_c_im=c_c_im, c_d=c_d, c_w_glu=c_w_glu, c_b_glu=c_b_glu)

    x = jnp.concatenate([x_prompt.reshape(tp, D),
                         x_sample.transpose(1, 0, 2).reshape(ts, D)], axis=0)
    conv_past_tm = state_conv.transpose(0, 2, 1, 3)

    w_win = jnp.asarray(_sample_weights(n_new, window, np.arange(window), n_heads))
    w_new = jnp.asarray(_sample_weights(n_new, window, window + np.arange(NEW_ROWS_PAD), n_heads))
    pad_new = lambda a: jnp.pad(a, ((0, 0), (0, NEW_ROWS_PAD - n_new), (0, 0)))

    st = {k: [] for k in ('k_p', 'v_p', 'conv_p', 're_p', 'im_p', 'k_s', 'v_s', 'conv_s', 're_s', 'im_s')}
    for l in range(depth):
        proj = norm_matmul(x, norm1_g[l], w_in, l, tm=tm_big, tn=512)
        qn, kn = qk_norm(proj, b_q_g[l], b_k_g[l], c_b, tm=512)
        v_p = proj[:tp, v_col0:v_col0 + c_b]

        ya_p, conv_p = conv_prompt(proj, n_batch, L, c_a, p, l)
        ya_s, conv_s_tm = conv_sample(proj, tp, n_new, n_b, c_a, conv_past_tm, p, l)
        yb_p = attn_prompt(qn, kn, proj, n_batch, L, c_b, v_col0)
        bm = lambda a: a.reshape(n_new, n_b, c_b).transpose(1, 0, 2)
        k_new, v_new = bm(kn[tp:]), bm(proj[tp:, v_col0:v_col0 + c_b])
        q_b = bm(qn[tp:]).reshape(n_b, n_new * n_heads, HEAD_DIM)
        yb_s = attn_sample(q_b, pad_new(k_new), pad_new(v_new), cache_k, cache_v, l, w_win, w_new)
        yb_s = yb_s.reshape(n_b, n_new, c_b).transpose(1, 0, 2).reshape(ts, c_b)
        sp = s5_params(p, l)
        yc_p, re_p, im_p = s5_prompt(proj, n_batch, L, c_c, u_col0, sp, p, l)
        h0_re = state_ssm_re[l].reshape(n_b, -1)
        h0_im = state_ssm_im[l].reshape(n_b, -1)
        yc_s, re_s, im_s = s5_sample(proj, tp, n_new, n_b, c_c, u_col0, h0_re, h0_im, sp, p, l)

        h = out_proj((ya_p, yb_p, yc_p), (ya_s, yb_s, yc_s), w_out, l, x, tm=ts, tn=512)

        j = l // 2
        if l % 2 == 0:
            x = dense_ffn(h, norm2_g[l], ffn_w1[j:j + 1], ffn_w3[j:j + 1], ffn_w2[j:j + 1])
        else:
            x = moe_ffn(h, norm2_g[l], moe_w_router[j], moe_b_router[j],
                        moe_w1[j], moe_w3[j], moe_w2[j])

        g_shape = state_ssm_re.shape[2:]
        st['k_p'].append(kn[:tp].reshape(n_batch, L, n_heads, HEAD_DIM))
        st['v_p'].append(v_p.reshape(n_batch, L, n_heads, HEAD_DIM))
        st['conv_p'].append(conv_p)
        st['re_p'].append(re_p[:, 0].reshape((n_batch,) + g_shape))
        st['im_p'].append(im_p[:, 0].reshape((n_batch,) + g_shape))
        st['k_s'].append(k_new.reshape(n_b, n_new, n_heads, HEAD_DIM))
        st['v_s'].append(v_new.reshape(n_b, n_new, n_heads, HEAD_DIM))
        st['conv_s'].append(conv_s_tm.transpose(1, 0, 2))
        st['re_s'].append(re_s.reshape((n_b,) + g_shape))
        st['im_s'].append(im_s.reshape((n_b,) + g_shape))

    y_p = x[:tp].reshape(n_batch, L, D)
    y_s = x[tp:].reshape(n_new, n_b, D).transpose(1, 0, 2)
    stk = lambda k: jnp.stack(st[k])
    return (y_p, y_s, stk('k_p'), stk('v_p'), stk('conv_p'), stk('re_p'), stk('im_p'),
            stk('k_s'), stk('v_s'), stk('conv_s'), stk('re_s'), stk('im_s'))
```

```python
import functools
import math

import numpy as np
import jax
import jax.numpy as jnp
from jax import lax
from jax.experimental import pallas as pl
from jax.experimental.pallas import tpu as pltpu

F32 = jnp.float32
BF16 = jnp.bfloat16
EPS = 1e-6
NEG = -1e30

HEAD_DIM = 64
ATTN_SCALE = HEAD_DIM ** -0.5
ATTN_BLOCK = 128
DILATIONS = (1, 4, 16)
DIL_STEPS = 128
CONV_WIDTH = 31
CONV_PAST = CONV_WIDTH - 1
SSM_GROUP = 16
SSM_STATE = 64
N_EXPERTS = 8
TOP_K = 2
LANES = 128
SUBLANES = 8
VMEM_LIMIT = 56 * 1024 * 1024


def _cparams(sem, vmem=VMEM_LIMIT):
    return pltpu.CompilerParams(dimension_semantics=sem, vmem_limit_bytes=vmem)


def _nt_dot(a, b):
    return lax.dot_general(a, b, (((1,), (1,)), ((), ())), preferred_element_type=F32)


def _sigmoid(x):
    return 1.0 / (1.0 + jnp.exp(-x))


def _silu(x):
    return x * _sigmoid(x)


def _gelu_tanh(x):
    c = math.sqrt(2.0 / math.pi)
    return 0.5 * x * (1.0 + jnp.tanh(c * (x + 0.044715 * (x * x * x))))


def _norm_mm_kernel(x_ref, g_ref, w_ref, o_ref, u_scr):
    @pl.when(pl.program_id(1) == 0)
    def _():
        x = x_ref[...]
        ms = jnp.mean(x * x, axis=-1, keepdims=True)
        u_scr[...] = (x * lax.rsqrt(ms + EPS) * g_ref[...]).astype(BF16)

    o_ref[...] = jnp.dot(u_scr[...], w_ref[...].astype(BF16), preferred_element_type=F32)


def norm_matmul(x, g, w_stack, layer, tm, tn):
    T, D = x.shape
    N = w_stack.shape[-1]
    return pl.pallas_call(
        _norm_mm_kernel,
        grid=(T // tm, N // tn),
        in_specs=[
            pl.BlockSpec((tm, D), lambda i, j: (i, 0)),
            pl.BlockSpec((1, D), lambda i, j: (0, 0)),
            pl.BlockSpec((None, D, tn), lambda i, j: (layer, 0, j)),
        ],
        out_specs=pl.BlockSpec((tm, tn), lambda i, j: (i, j)),
        out_shape=jax.ShapeDtypeStruct((T, N), F32),
        scratch_shapes=[pltpu.VMEM((tm, D), BF16)],
        compiler_params=_cparams(("parallel", "arbitrary")),
        name="norm_matmul",
    )(x, g.reshape(1, D), w_stack)


def _qknorm_kernel(q_ref, k_ref, gq_ref, gk_ref, qo_ref, ko_ref):
    tm = q_ref.shape[0]
    lane = lax.broadcasted_iota(jnp.int32, (tm, LANES), 1)
    head0 = lane < HEAD_DIM

    def norm(x, g):
        sq = x * x
        s0 = jnp.sum(jnp.where(head0, sq, 0.0), axis=-1, keepdims=True)
        s1 = jnp.sum(jnp.where(head0, 0.0, sq), axis=-1, keepdims=True)
        ms = jnp.where(head0, s0, s1) * (1.0 / HEAD_DIM)
        return x * lax.rsqrt(ms + EPS) * g

    for t in range(q_ref.shape[1] // LANES):
        sl = slice(t * LANES, (t + 1) * LANES)
        qo_ref[:, sl] = norm(q_ref[:, sl], gq_ref[...]) * ATTN_SCALE
        ko_ref[:, sl] = norm(k_ref[:, sl], gk_ref[...])


def qk_norm(proj, gq, gk, c_b, tm):
    T = proj.shape[0]
    gq2 = jnp.concatenate([gq, gq]).reshape(1, LANES)
    gk2 = jnp.concatenate([gk, gk]).reshape(1, LANES)
    qblk = 1
    return pl.pallas_call(
        _qknorm_kernel,
        grid=(T // tm,),
        in_specs=[
            pl.BlockSpec((tm, c_b), lambda i: (i, qblk)),
            pl.BlockSpec((tm, c_b), lambda i: (i, qblk + 1)),
            pl.BlockSpec((1, LANES), lambda i: (0, 0)),
            pl.BlockSpec((1, LANES), lambda i: (0, 0)),
        ],
        out_specs=[pl.BlockSpec((tm, c_b), lambda i: (i, 0)),
                   pl.BlockSpec((tm, c_b), lambda i: (i, 0))],
        out_shape=[jax.ShapeDtypeStruct((T, c_b), F32)] * 2,
        compiler_params=_cparams(("parallel",)),
        name="qk_norm",
    )(proj, proj, gq2, gk2)


def _attn_prompt_kernel(q_ref, k_ref, v_ref, o_ref,
                        q0_s, q1_s, k_s, v_s,
                        m1, l1, a1, m4, l4, a4, m16, l16, a16):
    L = q_ref.shape[0]
    B = ATTN_BLOCK
    head0 = lax.broadcasted_iota(jnp.int32, (B, LANES), 1) < HEAD_DIM
    qi = lax.broadcasted_iota(jnp.int32, (2 * B, B), 0) % B
    ki = lax.broadcasted_iota(jnp.int32, (2 * B, B), 1)
    tri_cur = ki <= qi
    tri_prev = ki >= qi

    stats = {1: (m1, l1, a1), 4: (m4, l4, a4), 16: (m16, l16, a16)}

    for d in DILATIONS:
        n = L // d
        nb = n // B
        m_s, l_s, a_s = stats[d]
        for r in range(d):
            src = pl.ds(r, n, stride=d) if d > 1 else slice(None)
            dst = slice(r * n, (r + 1) * n)
            hm = lax.broadcasted_iota(jnp.int32, (n, LANES), 1) < HEAD_DIM
            q = q_ref[src, :]
            q0_s[dst, :] = jnp.where(hm, q, 0.0).astype(BF16)
            q1_s[dst, :] = jnp.where(hm, 0.0, q).astype(BF16)
            k_s[dst, :] = k_ref[src, :].astype(BF16)
            v_s[dst, :] = v_ref[src, :].astype(BF16)

        def block(j, carry, nb=nb, m_s=m_s, l_s=l_s, a_s=a_s):
            cur = pl.ds(pl.multiple_of(j * B, B), B)
            qb = jnp.concatenate([q0_s[cur, :], q1_s[cur, :]], axis=0)
            s_c = jnp.where(tri_cur, _nt_dot(qb, k_s[cur, :]), NEG)
            if nb > 1:
                prev = pl.ds(pl.multiple_of(jnp.maximum(j - 1, 0) * B, B), B)
                mask_prev = jnp.logical_and(tri_prev, (j % nb) > 0)
                s_p = jnp.where(mask_prev, _nt_dot(qb, k_s[prev, :]), NEG)
                m = jnp.max(jnp.maximum(s_c, s_p), axis=-1, keepdims=True)
                p_c, p_p = jnp.exp(s_c - m), jnp.exp(s_p - m)
                den = jnp.sum(p_c + p_p, axis=-1, keepdims=True)
                pv = (jnp.dot(p_c.astype(BF16), v_s[cur, :], preferred_element_type=F32)
                      + jnp.dot(p_p.astype(BF16), v_s[prev, :], preferred_element_type=F32))
            else:
                m = jnp.max(s_c, axis=-1, keepdims=True)
                p_c = jnp.exp(s_c - m)
                den = jnp.sum(p_c, axis=-1, keepdims=True)
                pv = jnp.dot(p_c.astype(BF16), v_s[cur, :], preferred_element_type=F32)
            m_s[cur, :] = jnp.where(head0, m[:B], m[B:])
            l_s[cur, :] = jnp.where(head0, den[:B], den[B:])
            a_s[cur, :] = jnp.where(head0, pv[:B], pv[B:])
            return carry

        lax.fori_loop(0, L // B, block, 0, unroll=8)

    dmax = DILATIONS[-1]
    nrow = L // dmax
    for r in range(dmax):
        o1 = pl.ds(r, nrow, stride=dmax)
        o4 = pl.ds((r % 4) * (L // 4) + r // 4, nrow, stride=dmax // 4)
        o16 = slice(r * nrow, (r + 1) * nrow)
        mm1, mm4, mm16 = m1[o1, :], m4[o4, :], m16[o16, :]
        mx = jnp.maximum(jnp.maximum(mm1, mm4), mm16)
        w1, w4, w16 = jnp.exp(mm1 - mx), jnp.exp(mm4 - mx), jnp.exp(mm16 - mx)
        num = w1 * a1[o1, :] + w4 * a4[o4, :] + w16 * a16[o16, :]
        den = w1 * l1[o1, :] + w4 * l4[o4, :] + w16 * l16[o16, :]
        o_ref[o1, :] = num / den


def attn_prompt(qn, kn, proj, n_batch, L, c_b, v_col0):
    n_hp = c_b // LANES
    vblk = v_col0 // LANES
    stat = [pltpu.VMEM((L, LANES), F32)] * 9
    return pl.pallas_call(
        _attn_prompt_kernel,
        grid=(n_batch, n_hp),
        in_specs=[
            pl.BlockSpec((L, LANES), lambda b, h: (b, h)),
            pl.BlockSpec((L, LANES), lambda b, h: (b, h)),
            pl.BlockSpec((L, LANES), lambda b, h: (b, vblk + h)),
        ],
        out_specs=pl.BlockSpec((L, LANES), lambda b, h: (b, h)),
        out_shape=jax.ShapeDtypeStruct((n_batch * L, c_b), F32),
        scratch_shapes=[pltpu.VMEM((L, LANES), BF16)] * 4 + stat,
        compiler_params=_cparams(("parallel", "parallel")),
        name="attn_prompt",
    )(qn, kn, proj)


def _kv_window_kernel(*refs, depth):
    k_refs, v_refs = refs[:depth], refs[depth:2 * depth]
    kt_ref, vt_ref = refs[2 * depth:]
    layer = pl.program_id(0)
    for l in range(depth):
        @pl.when(layer == l)
        def _(l=l):
            kt_ref[...] = k_refs[l][...].T
            vt_ref[...] = v_refs[l][...].T


def kv_window(kns, projs, n_batch, L, c_b, v_col0):
    depth = len(kns)
    n_hp = c_b // LANES
    vblk = v_col0 // LANES

    def spec(l, col0):
        def imap(layer, b, h):
            return jnp.where(layer == l, b, 0), col0 + jnp.where(layer == l, h, 0)
        return pl.BlockSpec((L, LANES), imap)

    out_spec = pl.BlockSpec((None, None, LANES, L), lambda layer, b, h: (layer, b, h, 0))
    out = jax.ShapeDtypeStruct((depth, n_batch, c_b, L), F32)
    return pl.pallas_call(
        functools.partial(_kv_window_kernel, depth=depth),
        grid=(depth, n_batch, n_hp),
        in_specs=[spec(l, 0) for l in range(depth)] + [spec(l, vblk) for l in range(depth)],
        out_specs=[out_spec, out_spec],
        out_shape=[out, out],
        compiler_params=_cparams(("arbitrary", "arbitrary", "arbitrary")),
        name="kv_window",
    )(*kns, *projs)


def _branch_multiplicity(dist):
    c = np.zeros(dist.shape, np.float32)
    for d in DILATIONS:
        c += ((dist >= 0) & (dist % d == 0) & (dist <= d * DIL_STEPS)).astype(np.float32)
    return c


NEW_ROWS_PAD = 16


def _sample_weights(n_new, past_len, pos, n_heads):
    s = np.repeat(np.arange(n_new), n_heads)[:, None]
    return _branch_multiplicity(past_len + s - np.asarray(pos)[None, :]).astype(np.float32)


def _attn_sample_kernel(q_ref, kt_ref, vt_ref, kn_ref, vn_ref, w_ref, wn_ref, o_ref):
    n_heads, hd, window = kt_ref.shape
    width = n_heads * hd
    q = q_ref[...]
    nq = q.shape[0]
    q2 = jnp.concatenate([q, q], axis=-1)
    qt = jnp.concatenate([q2] * (width // (2 * hd)), axis=-1)
    own_head = (lax.broadcasted_iota(jnp.int32, (nq, width), 0) % n_heads
                == lax.broadcasted_iota(jnp.int32, (nq, width), 1) // hd)
    qbd = jnp.where(own_head, qt, 0.0).astype(BF16)

    w, wn = w_ref[...], wn_ref[...]
    s_c = jnp.dot(qbd, kt_ref[...].reshape(width, window).astype(BF16),
                  preferred_element_type=F32)
    s_n = _nt_dot(qbd, kn_ref[...].astype(BF16))
    s_c = jnp.where(w > 0.0, s_c, NEG)
    s_n = jnp.where(wn > 0.0, s_n, NEG)
    m = jnp.maximum(jnp.max(s_c, axis=-1, keepdims=True), jnp.max(s_n, axis=-1, keepdims=True))
    p_c = w * jnp.exp(s_c - m)
    p_n = wn * jnp.exp(s_n - m)
    den = jnp.sum(p_c, axis=-1, keepdims=True) + jnp.sum(p_n, axis=-1, keepdims=True)
    full = (_nt_dot(p_c.astype(BF16), vt_ref[...].reshape(width, window).astype(BF16))
            + jnp.dot(p_n.astype(BF16), vn_ref[...].astype(BF16), preferred_element_type=F32))
    full = jnp.where(own_head, full, 0.0)
    acc = full[:, 0:LANES]
    for t in range(1, width // LANES):
        acc = acc + full[:, t * LANES:(t + 1) * LANES]
    o_ref[...] = (acc[:, :hd] + acc[:, hd:]) / den


def attn_sample(q_b, k_new, v_new, cache_k, cache_v, layer, w, wn):
    n_b, nq, _ = q_b.shape
    depth, _, window, n_heads, _ = cache_k.shape
    width = n_heads * HEAD_DIM
    kt = cache_k.transpose(0, 1, 3, 4, 2)
    vt = cache_v.transpose(0, 1, 3, 4, 2)
    cache_spec = pl.BlockSpec((None, None, n_heads, HEAD_DIM, window), lambda b: (layer, b, 0, 0, 0))
    new_spec = pl.BlockSpec((None, NEW_ROWS_PAD, width), lambda b: (b, 0, 0))
    q_spec = pl.BlockSpec((None, nq, HEAD_DIM), lambda b: (b, 0, 0))

    def const_spec(a):
        return pl.BlockSpec(a.shape, lambda b: (0, 0))

    return pl.pallas_call(
        _attn_sample_kernel,
        grid=(n_b,),
        in_specs=[q_spec, cache_spec, cache_spec, new_spec, new_spec, const_spec(w), const_spec(wn)],
        out_specs=q_spec,
        out_shape=jax.ShapeDtypeStruct((n_b, nq, HEAD_DIM), F32),
        compiler_params=_cparams(("parallel",)),
        name="attn_sample",
    )(q_b, kt, vt, k_new, v_new, w, wn)


def _layernorm_silu(y, g, b):
    mu = jnp.mean(y, axis=-1, keepdims=True)
    yc = y - mu
    var = jnp.mean(yc * yc, axis=-1, keepdims=True)
    return _silu(yc * lax.rsqrt(var + EPS) * g + b)


CONV_HALO = 32
CONV_ROWS = 32


def _conv_prompt_kernel(val_ref, gate_ref, wdw_ref, bdw_ref, lng_ref, lnb_ref, wpw_ref,
                        y_ref, st_ref, xp_s, y_s):
    c = pl.program_id(1)
    lc, ca = val_ref.shape
    off = CONV_HALO - CONV_PAST

    @pl.when(c == 0)
    def _():
        xp_s[0:CONV_HALO, :] = jnp.zeros((CONV_HALO, ca), F32)

    @pl.when(c > 0)
    def _():
        xp_s[0:CONV_HALO, :] = xp_s[lc:lc + CONV_HALO, :]

    xp_s[CONV_HALO:CONV_HALO + lc, :] = val_ref[...] * _sigmoid(gate_ref[...])

    for rb in range(lc // CONV_ROWS):
        for lb in range(ca // LANES):
            ls = slice(lb * LANES, (lb + 1) * LANES)
            acc = jnp.zeros((CONV_ROWS, LANES), F32)
            for j in range(CONV_WIDTH):
                r0 = rb * CONV_ROWS + off + j
                acc = acc + wdw_ref[j:j + 1, ls] * xp_s[r0:r0 + CONV_ROWS, ls]
            y_s[rb * CONV_ROWS:(rb + 1) * CONV_ROWS, ls] = acc

    z = _layernorm_silu(y_s[...] + bdw_ref[...], lng_ref[...], lnb_ref[...])
    y_ref[...] = jnp.dot(z.astype(BF16), wpw_ref[...].astype(BF16), preferred_element_type=F32)

    @pl.when(c == pl.num_programs(1) - 1)
    def _():
        st_ref[...] = xp_s[lc + off:lc + CONV_HALO, :]


def _layer_vec(a, layer):
    return a[layer].reshape(1, -1)


def conv_prompt(proj, n_batch, L, c_a, p, layer, lc=256):
    nch = L // lc
    vec = pl.BlockSpec((1, c_a), lambda b, c: (0, 0))
    return pl.pallas_call(
        _conv_prompt_kernel,
        grid=(n_batch, nch),
        in_specs=[
            pl.BlockSpec((lc, c_a), lambda b, c: (b * nch + c, 0)),
            pl.BlockSpec((lc, c_a), lambda b, c: (b * nch + c, 1)),
            pl.BlockSpec((None, CONV_WIDTH, c_a), lambda b, c: (layer, 0, 0)),
            vec, vec, vec,
            pl.BlockSpec((None, c_a, c_a), lambda b, c: (layer, 0, 0)),
        ],
        out_specs=[pl.BlockSpec((lc, c_a), lambda b, c: (b * nch + c, 0)),
                   pl.BlockSpec((None, CONV_PAST, c_a), lambda b, c: (b, 0, 0))],
        out_shape=[jax.ShapeDtypeStruct((n_batch * L, c_a), F32),
                   jax.ShapeDtypeStruct((n_batch, CONV_PAST, c_a), F32)],
        scratch_shapes=[pltpu.VMEM((lc + CONV_HALO, c_a), F32), pltpu.VMEM((lc, c_a), F32)],
        compiler_params=_cparams(("parallel", "arbitrary")),
        name="conv_prompt",
    )(proj, proj, p['a_w_dw'], _layer_vec(p['a_b_dw'], layer), _layer_vec(p['a_ln_g'], layer),
      _layer_vec(p['a_ln_b'], layer), p['a_w_pw'])


def _conv_sample_kernel(val_ref, gate_ref, past_ref, wdw_ref, bdw_ref, lng_ref, lnb_ref, wpw_ref,
                        y_ref, st_ref, y_s):
    n_b = past_ref.shape[1]
    n_new = val_ref.shape[0] // n_b
    g = val_ref[...] * _sigmoid(gate_ref[...])

    def xp(t):
        if t < CONV_PAST:
            return past_ref[t]
        return g[(t - CONV_PAST) * n_b:(t - CONV_PAST + 1) * n_b, :]

    for s in range(n_new):
        acc = jnp.zeros_like(xp(0))
        for j in range(CONV_WIDTH):
            acc = acc + wdw_ref[j:j + 1, :] * xp(s + j)
        y_s[s * n_b:(s + 1) * n_b, :] = acc
    z = _layernorm_silu(y_s[...] + bdw_ref[...], lng_ref[...], lnb_ref[...])
    y_ref[...] = jnp.dot(z.astype(BF16), wpw_ref[...].astype(BF16), preferred_element_type=F32)
    for t in range(CONV_PAST):
        st_ref[t] = xp(t + n_new)


def conv_sample(proj, row0, n_new, n_b, c_a, past_tm, p, layer):
    ts = n_new * n_b
    vec = pl.BlockSpec((1, c_a), lambda i: (0, 0))
    return pl.pallas_call(
        _conv_sample_kernel,
        grid=(1,),
        in_specs=[
            pl.BlockSpec((ts, c_a), lambda i: (row0 // ts, 0)),
            pl.BlockSpec((ts, c_a), lambda i: (row0 // ts, 1)),
            pl.BlockSpec((None, CONV_PAST, n_b, c_a), lambda i: (layer, 0, 0, 0)),
            pl.BlockSpec((None, CONV_WIDTH, c_a), lambda i: (layer, 0, 0)),
            vec, vec, vec,
            pl.BlockSpec((None, c_a, c_a), lambda i: (layer, 0, 0)),
        ],
        out_specs=[pl.BlockSpec((ts, c_a), lambda i: (0, 0)),
                   pl.BlockSpec((CONV_PAST, n_b, c_a), lambda i: (0, 0, 0))],
        out_shape=[jax.ShapeDtypeStruct((ts, c_a), F32),
                   jax.ShapeDtypeStruct((CONV_PAST, n_b, c_a), F32)],
        scratch_shapes=[pltpu.VMEM((ts, c_a), F32)],
        compiler_params=_cparams(("arbitrary",)),
        name="conv_sample",
    )(proj, proj, past_tm, p['a_w_dw'], _layer_vec(p['a_b_dw'], layer),
      _layer_vec(p['a_ln_g'], layer), _layer_vec(p['a_ln_b'], layer), p['a_w_pw'])


def _s5_param_kernel(are_ref, aim_ref, ldt_ref, bre_ref, bim_ref,
                     pre_ref, pim_ref, bbre_ref, bbim_ref):
    a_re, a_im = are_ref[...], aim_ref[...]
    dt = jnp.exp(ldt_ref[...])
    mag = jnp.exp(a_re * dt)
    ab_re, ab_im = mag * jnp.cos(a_im * dt), mag * jnp.sin(a_im * dt)
    nr, ni = ab_re - 1.0, ab_im
    inv = 1.0 / (a_re * a_re + a_im * a_im)
    f_re = (nr * a_re + ni * a_im) * inv
    f_im = (ni * a_re - nr * a_im) * inv
    b_re, b_im = bre_ref[...], bim_ref[...]
    bbre_ref[...] = f_re * b_re - f_im * b_im
    bbim_ref[...] = f_re * b_im + f_im * b_re
    pr, pi = ab_re, ab_im
    pre_ref[0] = pr
    pim_ref[0] = pi
    for k in range(1, SUBLANES):
        pr, pi = pr * ab_re - pi * ab_im, pr * ab_im + pi * ab_re
        pre_ref[k] = pr
        pim_ref[k] = pi


def s5_params(p, layer):
    a_re, a_im = p['c_a_re'][layer], p['c_a_im'][layer]
    G, N = a_re.shape
    C = SSM_GROUP
    b_re_t = p['c_b_re'][layer].transpose(0, 2, 1)
    b_im_t = p['c_b_im'][layer].transpose(0, 2, 1)
    pre, pim, bbre, bbim = pl.pallas_call(
        _s5_param_kernel,
        out_shape=[jax.ShapeDtypeStruct((SUBLANES, G, 1, N), F32)] * 2
        + [jax.ShapeDtypeStruct((G, C, N), F32)] * 2,
        name="s5_params",
    )(a_re.reshape(G, 1, N), a_im.reshape(G, 1, N), p['c_log_dt'][layer].reshape(G, 1, 1),
      b_re_t, b_im_t)
    S = G * N
    pre, pim = pre.reshape(SUBLANES, S), pim.reshape(SUBLANES, S)
    t = np.arange(SUBLANES)[:, None]
    tabs = []
    for k in (1, 2, 4):
        keep = jnp.asarray(t >= k)
        tabs += [jnp.where(keep, pre[k - 1][None, :], 0.0), jnp.where(keep, pim[k - 1][None, :], 0.0)]
    tabs += [pre, pim]
    tab = jnp.stack(tabs)
    eye = jnp.eye(G, dtype=F32)

    def in_proj(bb):
        return (eye[:, None, :, None] * bb[:, :, None, :]).reshape(G * C, S).astype(BF16)

    def out_proj(c):
        return (eye[:, None, :, None] * c.transpose(0, 2, 1)[:, :, None, :]).reshape(S, G * C)

    c_cat = jnp.concatenate([out_proj(p['c_c_re'][layer]), -out_proj(p['c_c_im'][layer])]).astype(BF16)
    return dict(tab=tab, bb_re=in_proj(bbre), bb_im=in_proj(bbim), c_cat=c_cat,
                ab_re=pre[0:1], ab_im=pim[0:1])


def _s5_readout(u, xr, xi, ccat_ref, d_ref, wglu_ref, bglu_ref):
    S = xr.shape[-1]
    y = (jnp.dot(xr.astype(BF16), ccat_ref[0:S, :], preferred_element_type=F32)
         + jnp.dot(xi.astype(BF16), ccat_ref[S:2 * S, :], preferred_element_type=F32)
         + d_ref[...] * u)
    z = _gelu_tanh(y)
    gate = jnp.dot(z.astype(BF16), wglu_ref[...].astype(BF16), preferred_element_type=F32)
    return z * _sigmoid(gate + bglu_ref[...])


S5_LANE_GROUP = 512


def _s5_prompt_kernel(u_ref, bbre_ref, bbim_ref, tab_ref, ccat_ref, d_ref, wglu_ref, bglu_ref,
                      y_ref, hre_ref, him_ref, xr_s, xi_s, h_s):
    c = pl.program_id(1)
    tc = u_ref.shape[0]
    S = xr_s.shape[1]

    @pl.when(c == 0)
    def _():
        h_s[...] = jnp.zeros_like(h_s)

    u = u_ref[...]
    ub = u.astype(BF16)
    xr_s[...] = jnp.dot(ub, bbre_ref[...], preferred_element_type=F32)
    xi_s[...] = jnp.dot(ub, bbim_ref[...], preferred_element_type=F32)

    for jg in range(S // S5_LANE_GROUP):
        ls = slice(jg * S5_LANE_GROUP, (jg + 1) * S5_LANE_GROUP)
        levels = [(k, tab_ref[2 * i, :, ls], tab_ref[2 * i + 1, :, ls])
                  for i, k in enumerate((1, 2, 4))]
        cr, ci = tab_ref[6, :, ls], tab_ref[7, :, ls]

        def tile(i, carry, ls=ls, levels=levels, cr=cr, ci=ci):
            hr, hi = carry
            rows = pl.ds(pl.multiple_of(i * SUBLANES, SUBLANES), SUBLANES)
            br, bi = xr_s[rows, ls], xi_s[rows, ls]
            for k, pr, pi in levels:
                sr, si = pltpu.roll(br, k, axis=0), pltpu.roll(bi, k, axis=0)
                br, bi = br + pr * sr - pi * si, bi + pr * si + pi * sr
            br, bi = br + cr * hr - ci * hi, bi + cr * hi + ci * hr
            xr_s[rows, ls] = br
            xi_s[rows, ls] = bi
            last = slice(SUBLANES - 1, SUBLANES)
            return (jnp.broadcast_to(br[last, :], br.shape), jnp.broadcast_to(bi[last, :], bi.shape))

        hr, hi = lax.fori_loop(0, tc // SUBLANES, tile, (h_s[0, :, ls], h_s[1, :, ls]))
        h_s[0, :, ls] = hr
        h_s[1, :, ls] = hi

    y_ref[...] = _s5_readout(u, xr_s[...], xi_s[...], ccat_ref, d_ref, wglu_ref, bglu_ref)
    hre_ref[...] = h_s[0]
    him_ref[...] = h_s[1]


def s5_prompt(proj, n_batch, L, c_c, u_col0, sp, p, layer, tc=256):
    nch = L // tc
    S = sp['tab'].shape[-1]
    ublk = u_col0 // c_c

    def full(a):
        return pl.BlockSpec(a.shape, lambda b, c: (0,) * a.ndim)

    d = p['c_d'][layer].reshape(1, c_c)
    bglu = p['c_b_glu'][layer].reshape(1, c_c)
    return pl.pallas_call(
        _s5_prompt_kernel,
        grid=(n_batch, nch),
        in_specs=[
            pl.BlockSpec((tc, c_c), lambda b, c: (b * nch + c, ublk)),
            full(sp['bb_re']), full(sp['bb_im']), full(sp['tab']), full(sp['c_cat']), full(d),
            pl.BlockSpec((None, c_c, c_c), lambda b, c: (layer, 0, 0)),
            full(bglu),
        ],
        out_specs=[pl.BlockSpec((tc, c_c), lambda b, c: (b * nch + c, 0)),
                   pl.BlockSpec((None, SUBLANES, S), lambda b, c: (b, 0, 0)),
                   pl.BlockSpec((None, SUBLANES, S), lambda b, c: (b, 0, 0))],
        out_shape=[jax.ShapeDtypeStruct((n_batch * L, c_c), F32),
                   jax.ShapeDtypeStruct((n_batch, SUBLANES, S), F32),
                   jax.ShapeDtypeStruct((n_batch, SUBLANES, S), F32)],
        scratch_shapes=[pltpu.VMEM((tc, S), F32), pltpu.VMEM((tc, S), F32),
                        pltpu.VMEM((2, SUBLANES, S), F32)],
        compiler_params=_cparams(("parallel", "arbitrary")),
        name="s5_prompt",
    )(proj, sp['bb_re'], sp['bb_im'], sp['tab'], sp['c_cat'], d, p['c_w_glu'], bglu)


def _s5_sample_kernel(u_ref, h0re_ref, h0im_ref, bbre_ref, bbim_ref, abre_ref, abim_ref,
                      ccat_ref, d_ref, wglu_ref, bglu_ref,
                      y_ref, hre_ref, him_ref, xr_s, xi_s):
    n_b = h0re_ref.shape[0]
    n_new = u_ref.shape[0] // n_b
    u = u_ref[...]
    ub = u.astype(BF16)
    xr_s[...] = jnp.dot(ub, bbre_ref[...], preferred_element_type=F32)
    xi_s[...] = jnp.dot(ub, bbim_ref[...], preferred_element_type=F32)
    ar, ai = abre_ref[...], abim_ref[...]
    hr, hi = h0re_ref[...], h0im_ref[...]
    for s in range(n_new):
        rows = slice(s * n_b, (s + 1) * n_b)
        hr, hi = ar * hr - ai * hi + xr_s[rows, :], ar * hi + ai * hr + xi_s[rows, :]
        xr_s[rows, :] = hr
        xi_s[rows, :] = hi
    y_ref[...] = _s5_readout(u, xr_s[...], xi_s[...], ccat_ref, d_ref, wglu_ref, bglu_ref)
    hre_ref[...] = hr
    him_ref[...] = hi


def s5_sample(proj, row0, n_new, n_b, c_c, u_col0, h0_re, h0_im, sp, p, layer):
    ts = n_new * n_b
    S = sp['tab'].shape[-1]

    def full(a):
        return pl.BlockSpec(a.shape, lambda i: (0,) * a.ndim)

    d = p['c_d'][layer].reshape(1, c_c)
    bglu = p['c_b_glu'][layer].reshape(1, c_c)
    return pl.pallas_call(
        _s5_sample_kernel,
        grid=(1,),
        in_specs=[
            pl.BlockSpec((ts, c_c), lambda i: (row0 // ts, u_col0 // c_c)),
            full(h0_re), full(h0_im), full(sp['bb_re']), full(sp['bb_im']),
            full(sp['ab_re']), full(sp['ab_im']), full(sp['c_cat']), full(d),
            pl.BlockSpec((None, c_c, c_c), lambda i: (layer, 0, 0)),
            full(bglu),
        ],
        out_specs=[pl.BlockSpec((ts, c_c), lambda i: (0, 0)),
                   pl.BlockSpec((n_b, S), lambda i: (0, 0)),
                   pl.BlockSpec((n_b, S), lambda i: (0, 0))],
        out_shape=[jax.ShapeDtypeStruct((ts, c_c), F32),
                   jax.ShapeDtypeStruct((n_b, S), F32),
                   jax.ShapeDtypeStruct((n_b, S), F32)],
        scratch_shapes=[pltpu.VMEM((ts, S), F32), pltpu.VMEM((ts, S), F32)],
        compiler_params=_cparams(("arbitrary",)),
        name="s5_sample",
    )(proj, h0_re, h0_im, sp['bb_re'], sp['bb_im'], sp['ab_re'], sp['ab_im'], sp['c_cat'], d,
      p['c_w_glu'], bglu)


def _out_proj_kernel(yap_ref, ybp_ref, ycp_ref, yas_ref, ybs_ref, ycs_ref, w_ref, x_ref,
                     o_ref, cat_s, *, n_prompt_tiles):
    def fill(ya_ref, yb_ref, yc_ref):
        ca, cb = ya_ref.shape[1], yb_ref.shape[1]
        cat_s[:, 0:ca] = ya_ref[...].astype(BF16)
        cat_s[:, ca:ca + cb] = yb_ref[...].astype(BF16)
        cat_s[:, ca + cb:] = yc_ref[...].astype(BF16)

    first_col = pl.program_id(1) == 0
    is_prompt = pl.program_id(0) < n_prompt_tiles

    @pl.when(jnp.logical_and(first_col, is_prompt))
    def _():
        fill(yap_ref, ybp_ref, ycp_ref)

    @pl.when(jnp.logical_and(first_col, jnp.logical_not(is_prompt)))
    def _():
        fill(yas_ref, ybs_ref, ycs_ref)

    o_ref[...] = x_ref[...] + jnp.dot(cat_s[...], w_ref[...].astype(BF16),
                                      preferred_element_type=F32)


def out_proj(prompt_parts, sample_parts, w_stack, layer, x, tm, tn):
    T, D = x.shape
    dm = w_stack.shape[1]
    n_p = prompt_parts[0].shape[0] // tm
    assert prompt_parts[0].shape[0] % tm == 0 and sample_parts[0].shape[0] % tm == 0

    def prompt_spec(a):
        return pl.BlockSpec((tm, a.shape[1]), lambda i, j: (jnp.minimum(i, n_p - 1), 0))

    def sample_spec(a):
        return pl.BlockSpec((tm, a.shape[1]), lambda i, j: (jnp.maximum(i - n_p, 0), 0))

    return pl.pallas_call(
        functools.partial(_out_proj_kernel, n_prompt_tiles=n_p),
        grid=(T // tm, D // tn),
        in_specs=[prompt_spec(a) for a in prompt_parts] + [sample_spec(a) for a in sample_parts]
        + [pl.BlockSpec((None, dm, tn), lambda i, j: (layer, 0, j)),
           pl.BlockSpec((tm, tn), lambda i, j: (i, j))],
        out_specs=pl.BlockSpec((tm, tn), lambda i, j: (i, j)),
        out_shape=jax.ShapeDtypeStruct((T, D), F32),
        scratch_shapes=[pltpu.VMEM((tm, dm), BF16)],
        compiler_params=_cparams(("parallel", "arbitrary")),
        name="out_proj",
    )(*prompt_parts, *sample_parts, w_stack, x)


def _norm_cast_kernel(x_ref, g_ref, o_ref):
    x = x_ref[...]
    ms = jnp.mean(x * x, axis=-1, keepdims=True)
    o_ref[...] = (x * lax.rsqrt(ms + EPS) * g_ref[...]).astype(BF16)


def norm_cast(x, g, tm):
    T, D = x.shape
    return pl.pallas_call(
        _norm_cast_kernel,
        grid=(T // tm,),
        in_specs=[pl.BlockSpec((tm, D), lambda i: (i, 0)), pl.BlockSpec((1, D), lambda i: (0, 0))],
        out_specs=pl.BlockSpec((tm, D), lambda i: (i, 0)),
        out_shape=jax.ShapeDtypeStruct((T, D), BF16),
        compiler_params=_cparams(("parallel",)),
        name="norm_cast",
    )(x, g.reshape(1, D))


def _new_expert(te_ref, i):
    prev = te_ref[jnp.maximum(i - 1, 0)]
    return jnp.logical_or(i == 0, te_ref[i] != prev)


def _ffn_up_kernel(te_ref, nv_ref, x_ref, w1_ref, w3_ref, h_ref, w1_s, w3_s):
    i = pl.program_id(1)

    @pl.when(_new_expert(te_ref, i))
    def _():
        w1_s[...] = w1_ref[...].astype(BF16)
        w3_s[...] = w3_ref[...].astype(BF16)

    @pl.when(i < nv_ref[0])
    def _():
        x = x_ref[...]
        a = jnp.dot(x, w1_s[...], preferred_element_type=F32)
        b = jnp.dot(x, w3_s[...], preferred_element_type=F32)
        h_ref[...] = (_silu(a) * b).astype(BF16)

    @pl.when(i >= nv_ref[0])
    def _():
        h_ref[...] = jnp.zeros_like(h_ref)


def ffn_up(xs, w1, w3, tile_expert, n_valid, tm, fc):
    R, D = xs.shape
    F = w1.shape[-1]
    wspec = pl.BlockSpec((None, D, fc), lambda j, i, te, nv: (te[i], 0, j))
    return pl.pallas_call(
        _ffn_up_kernel,
        grid_spec=pltpu.PrefetchScalarGridSpec(
            num_scalar_prefetch=2,
            grid=(pl.cdiv(F, fc), R // tm),
            in_specs=[pl.BlockSpec((tm, D), lambda j, i, te, nv: (i, 0)), wspec, wspec],
            out_specs=pl.BlockSpec((tm, fc), lambda j, i, te, nv: (i, j)),
            scratch_shapes=[pltpu.VMEM((D, fc), BF16)] * 2,
        ),
        out_shape=jax.ShapeDtypeStruct((R, F), BF16),
        compiler_params=_cparams(("arbitrary", "arbitrary")),
        name="ffn_up",
    )(tile_expert, n_valid, xs, w1, w3)


def _ffn_down_kernel(te_ref, nv_ref, h_ref, w2_ref, *rest, residual):
    if residual:
        r_ref, o_ref, w2_s = rest
    else:
        o_ref, w2_s = rest
    i = pl.program_id(1)

    @pl.when(_new_expert(te_ref, i))
    def _():
        w2_s[...] = w2_ref[...].astype(BF16)

    @pl.when(i < nv_ref[0])
    def _():
        y = jnp.dot(h_ref[...], w2_s[...], preferred_element_type=F32)
        o_ref[...] = r_ref[...] + y if residual else y

    @pl.when(i >= nv_ref[0])
    def _():
        o_ref[...] = jnp.zeros_like(o_ref)


def ffn_down(h, w2, tile_expert, n_valid, tm, tn, residual=None):
    R, F = h.shape
    D = w2.shape[-1]
    tile = pl.BlockSpec((tm, tn), lambda j, i, te, nv: (i, j))
    in_specs = [pl.BlockSpec((tm, F), lambda j, i, te, nv: (i, 0)),
                pl.BlockSpec((None, F, tn), lambda j, i, te, nv: (te[i], 0, j))]
    args = [h, w2]
    if residual is not None:
        in_specs.append(tile)
        args.append(residual)
    return pl.pallas_call(
        functools.partial(_ffn_down_kernel, residual=residual is not None),
        grid_spec=pltpu.PrefetchScalarGridSpec(
            num_scalar_prefetch=2,
            grid=(D // tn, R // tm),
            in_specs=in_specs,
            out_specs=tile,
            scratch_shapes=[pltpu.VMEM((F, tn), BF16)],
        ),
        out_shape=jax.ShapeDtypeStruct((R, D), F32),
        compiler_params=_cparams(("arbitrary", "arbitrary")),
        name="ffn_down",
    )(tile_expert, n_valid, *args)


def _router_kernel(x_ref, g_ref, wr_ref, br_ref, idx_ref, gate_ref):
    x = x_ref[...]
    ms = jnp.mean(x * x, axis=-1, keepdims=True)
    u = x * lax.rsqrt(ms + EPS) * g_ref[...]
    logits = lax.dot_general(wr_ref[...], u, (((1,), (1,)), ((), ())),
                             precision=lax.Precision.HIGHEST,
                             preferred_element_type=F32) + br_ref[...]
    n_e = logits.shape[0]
    eid = lax.broadcasted_iota(jnp.int32, logits.shape, 0)
    m1 = jnp.max(logits, axis=0, keepdims=True)
    i1 = jnp.min(jnp.where(logits == m1, eid, n_e), axis=0, keepdims=True)
    rest = jnp.where(eid == i1, -jnp.inf, logits)
    m2 = jnp.max(rest, axis=0, keepdims=True)
    i2 = jnp.min(jnp.where(rest == m2, eid, n_e), axis=0, keepdims=True)
    e2 = jnp.exp(m2 - m1)
    g1 = 1.0 / (1.0 + e2)
    idx_ref[...] = jnp.where(eid == 0, i1, i2)
    gate_ref[...] = jnp.where(eid == 0, g1, e2 * g1)


def router(x, g, w_router, b_router, tm):
    T, D = x.shape
    E = w_router.shape[-1]
    return pl.pallas_call(
        _router_kernel,
        grid=(T // tm,),
        in_specs=[pl.BlockSpec((tm, D), lambda i: (i, 0)),
                  pl.BlockSpec((1, D), lambda i: (0, 0)),
                  pl.BlockSpec((E, D), lambda i: (0, 0)),
                  pl.BlockSpec((E, 1), lambda i: (0, 0))],
        out_specs=[pl.BlockSpec((E, tm), lambda i: (0, i)),
                   pl.BlockSpec((E, tm), lambda i: (0, i))],
        out_shape=[jax.ShapeDtypeStruct((E, T), jnp.int32), jax.ShapeDtypeStruct((E, T), F32)],
        compiler_params=_cparams(("parallel",)),
        name="moe_router",
    )(x, g.reshape(1, D), w_router.T, b_router.reshape(E, 1))


def _row_copy(src_hbm, row, dst, r, sem):
    return pltpu.make_async_copy(src_hbm.at[pl.ds(row, 1), :], dst.at[pl.ds(r, 1), :], sem)


GATHER_UNROLL = 8


def _rows_loop(tm, fn):
    def body(it, c):
        for u in range(GATHER_UNROLL):
            fn(it * GATHER_UNROLL + u)
        return c

    lax.fori_loop(0, tm // GATHER_UNROLL, body, 0)


def _gather_norm_kernel(src_ref, nv_ref, x_hbm, g_ref, o_ref, buf, sem):
    tm = buf.shape[1]
    i = pl.program_id(0)
    n_used = nv_ref[0]

    def fetch(tile, slot):
        _rows_loop(tm, lambda r: _row_copy(x_hbm, src_ref[tile * tm + r], buf.at[slot], r,
                                           sem.at[slot]).start())

    @pl.when(i == 0)
    def _():
        fetch(0, 0)

    @pl.when(i + 1 < n_used)
    def _():
        fetch(i + 1, (i + 1) % 2)

    @pl.when(i < n_used)
    def _():
        slot = i % 2
        _rows_loop(tm, lambda r: _row_copy(x_hbm, 0, buf.at[slot], r, sem.at[slot]).wait())
        x = buf[slot]
        ms = jnp.mean(x * x, axis=-1, keepdims=True)
        o_ref[...] = (x * lax.rsqrt(ms + EPS) * g_ref[...]).astype(BF16)

    @pl.when(i >= n_used)
    def _():
        o_ref[...] = jnp.zeros_like(o_ref)


def gather_norm(x, g, src, n_used, tm):
    T, D = x.shape
    R = src.shape[0]
    return pl.pallas_call(
        _gather_norm_kernel,
        grid_spec=pltpu.PrefetchScalarGridSpec(
            num_scalar_prefetch=2,
            grid=(R // tm,),
            in_specs=[pl.BlockSpec(memory_space=pl.ANY),
                      pl.BlockSpec((1, D), lambda i, s, n: (0, 0))],
            out_specs=pl.BlockSpec((tm, D), lambda i, s, n: (i, 0)),
            scratch_shapes=[pltpu.VMEM((2, tm, D), F32), pltpu.SemaphoreType.DMA((2,))],
        ),
        out_shape=jax.ShapeDtypeStruct((R, D), BF16),
        compiler_params=_cparams(("arbitrary",)),
        name="moe_gather",
    )(src, n_used, x, g.reshape(1, D))


def _combine_kernel(p0_ref, p1_ref, h_ref, gate_ref, y_hbm, o_ref, buf, sem):
    tm = buf.shape[2]
    i = pl.program_id(0)

    def fetch(tile, slot):
        def one(r):
            _row_copy(y_hbm, p0_ref[tile * tm + r], buf.at[slot, 0], r, sem.at[slot, 0]).start()
            _row_copy(y_hbm, p1_ref[tile * tm + r], buf.at[slot, 1], r, sem.at[slot, 1]).start()

        _rows_loop(tm, one)

    @pl.when(i == 0)
    def _():
        fetch(0, 0)

    @pl.when(i + 1 < pl.num_programs(0))
    def _():
        fetch(i + 1, (i + 1) % 2)

    slot = i % 2

    def wait(r):
        _row_copy(y_hbm, 0, buf.at[slot, 0], r, sem.at[slot, 0]).wait()
        _row_copy(y_hbm, 0, buf.at[slot, 1], r, sem.at[slot, 1]).wait()

    _rows_loop(tm, wait)
    gate = gate_ref[...]
    o_ref[...] = h_ref[...] + gate[:, 0:1] * buf[slot, 0] + gate[:, 1:2] * buf[slot, 1]


def moe_combine(h, gates, ys, pos0, pos1, tm):
    T, D = h.shape
    return pl.pallas_call(
        _combine_kernel,
        grid_spec=pltpu.PrefetchScalarGridSpec(
            num_scalar_prefetch=2,
            grid=(T // tm,),
            in_specs=[pl.BlockSpec((tm, D), lambda i, a, b: (i, 0)),
                      pl.BlockSpec((tm, TOP_K), lambda i, a, b: (i, 0)),
                      pl.BlockSpec(memory_space=pl.ANY)],
            out_specs=pl.BlockSpec((tm, D), lambda i, a, b: (i, 0)),
            scratch_shapes=[pltpu.VMEM((2, TOP_K, tm, D), F32), pltpu.SemaphoreType.DMA((2, TOP_K))],
        ),
        out_shape=jax.ShapeDtypeStruct((T, D), F32),
        compiler_params=_cparams(("arbitrary",)),
        name="moe_combine",
    )(pos0, pos1, h, gates, ys)


MOE_TM = 512
MOE_TM_DOWN = 256
GATHER_TM = 256


def moe_layout(idx, tm):
    T, K = idx.shape
    flat = idx.reshape(-1)
    onehot = (flat[:, None] == jnp.arange(N_EXPERTS)[None, :]).astype(jnp.int32)
    rank = jnp.take_along_axis(jnp.cumsum(onehot, axis=0) - onehot, flat[:, None], axis=1)[:, 0]
    counts = jnp.sum(onehot, axis=0)
    tiles = (counts + tm - 1) // tm
    tile_end = jnp.cumsum(tiles)
    start = (tile_end - tiles) * tm
    pos = start[flat] + rank
    n_tiles = (T * K) // tm + N_EXPERTS
    src = jnp.zeros((n_tiles * tm,), jnp.int32).at[pos].set(jnp.arange(T * K, dtype=jnp.int32) // K)
    owner = jnp.sum((jnp.arange(n_tiles)[:, None] >= tile_end[None, :]).astype(jnp.int32), axis=1)
    tile_expert = jnp.minimum(owner, N_EXPERTS - 1).astype(jnp.int32)
    n_valid = tile_end[-1:].astype(jnp.int32)
    return pos.reshape(T, K).astype(jnp.int32), src, tile_expert, n_valid


def moe_ffn(h, g, w_router, b_router, w1, w3, w2):
    T, D = h.shape
    idx_t, gate_t = router(h, g, w_router, b_router, tm=512)
    idx = idx_t[:TOP_K].T
    gates = gate_t[:TOP_K].T
    pos, src, tile_expert, n_valid = moe_layout(idx, MOE_TM)
    xs = gather_norm(h, g, src, n_valid * (MOE_TM // GATHER_TM), GATHER_TM)
    hid = ffn_up(xs, w1, w3, tile_expert, n_valid, MOE_TM, fc=1024)
    split = MOE_TM // MOE_TM_DOWN
    ys = ffn_down(hid, w2, jnp.repeat(tile_expert, split), n_valid * split, MOE_TM_DOWN, tn=512)
    return moe_combine(h, gates, ys, pos[:, 0], pos[:, 1], GATHER_TM)


def dense_ffn(h, g, w1, w3, w2):
    T, D = h.shape
    tm_up, tm_down = T // 8, T // 16

    def one_group(tm):
        return jnp.zeros((T // tm,), jnp.int32), jnp.full((1,), T // tm, jnp.int32)

    xs = norm_cast(h, g, tm=512)
    hid = ffn_up(xs, w1, w3, *one_group(tm_up), tm_up, fc=512)
    return ffn_down(hid, w2, *one_group(tm_down), tm_down, tn=512, residual=h)


def kernel(x_prompt, x_sample, cache_k, cache_v, state_conv, state_ssm_re, state_ssm_im,
           norm1_g, w_in, w_out, a_w_dw, a_b_dw, a_ln_g, a_ln_b, a_w_pw, b_q_g, b_k_g,
           c_a_re, c_a_im, c_log_dt, c_b_re, c_b_im, c_c_re, c_c_im, c_d, c_w_glu, c_b_glu,
           norm2_g, ffn_w1, ffn_w3, ffn_w2, moe_w_router, moe_b_router, moe_w1, moe_w3, moe_w2):
    n_batch, L, D = x_prompt.shape
    n_b, n_new, _ = x_sample.shape
    depth = w_in.shape[0]
    c_a = a_w_pw.shape[-1]
    c_c = c_w_glu.shape[-1]
    c_b = D - c_a - c_c
    n_heads = c_b // HEAD_DIM
    window = cache_k.shape[2]
    tp, ts = n_batch * L, n_b * n_new
    T = tp + ts
    q_col0, v_col0, u_col0 = 2 * c_a, 2 * c_a + 2 * c_b, 2 * c_a + 3 * c_b
    assert q_col0 == c_b and window == DILATIONS[-1] * DIL_STEPS and L == window
    tm_big = T // 8

    p = dict(a_w_dw=a_w_dw, a_b_dw=a_b_dw, a_ln_g=a_ln_g, a_ln_b=a_ln_b, a_w_pw=a_w_pw,
             c_a_re=c_a_re, c_a_im=c_a_im, c_log_dt=c_log_dt, c_b_re=c_b_re, c_b_im=c_b_im,
             c_c_re=c_c_re, c_c_im=c_c_im, c_d=c_d, c_w_glu=c_w_glu, c_b_glu=c_b_glu)

    x = jnp.concatenate([x_prompt.reshape(tp, D),
                         x_sample.transpose(1, 0, 2).reshape(ts, D)], axis=0)
    conv_past_tm = state_conv.transpose(0, 2, 1, 3)

    w_win = jnp.asarray(_sample_weights(n_new, window, np.arange(window), n_heads))
    w_new = jnp.asarray(_sample_weights(n_new, window, window + np.arange(NEW_ROWS_PAD), n_heads))
    pad_new = lambda a: jnp.pad(a, ((0, 0), (0, NEW_ROWS_PAD - n_new), (0, 0)))

    st = {k: [] for k in ('k_p', 'v_p', 'conv_p', 're_p', 'im_p', 'k_s', 'v_s', 'conv_s', 're_s', 'im_s')}
    for l in range(depth):
        proj = norm_matmul(x, norm1_g[l], w_in, l, tm=tm_big, tn=512)
        qn, kn = qk_norm(proj, b_q_g[l], b_k_g[l], c_b, tm=512)
        ya_p, conv_p = conv_prompt(proj, n_batch, L, c_a, p, l)
        ya_s, conv_s_tm = conv_sample(proj, tp, n_new, n_b, c_a, conv_past_tm, p, l)
        yb_p = attn_prompt(qn, kn, proj, n_batch, L, c_b, v_col0)
        bm = lambda a: a.reshape(n_new, n_b, c_b).transpose(1, 0, 2)
        k_new, v_new = bm(kn[tp:]), bm(proj[tp:, v_col0:v_col0 + c_b])
        q_b = bm(qn[tp:]).reshape(n_b, n_new * n_heads, HEAD_DIM)
        yb_s = attn_sample(q_b, pad_new(k_new), pad_new(v_new), cache_k, cache_v, l, w_win, w_new)
        yb_s = yb_s.reshape(n_b, n_new, c_b).transpose(1, 0, 2).reshape(ts, c_b)
        sp = s5_params(p, l)
        yc_p, re_p, im_p = s5_prompt(proj, n_batch, L, c_c, u_col0, sp, p, l)
        h0_re = state_ssm_re[l].reshape(n_b, -1)
        h0_im = state_ssm_im[l].reshape(n_b, -1)
        yc_s, re_s, im_s = s5_sample(proj, tp, n_new, n_b, c_c, u_col0, h0_re, h0_im, sp, p, l)

        h = out_proj((ya_p, yb_p, yc_p), (ya_s, yb_s, yc_s), w_out, l, x, tm=ts, tn=1024)

        j = l // 2
        if l % 2 == 0:
            x = dense_ffn(h, norm2_g[l], ffn_w1[j:j + 1], ffn_w3[j:j + 1], ffn_w2[j:j + 1])
        else:
            x = moe_ffn(h, norm2_g[l], moe_w_router[j], moe_b_router[j],
                        moe_w1[j], moe_w3[j], moe_w2[j])

        g_shape = state_ssm_re.shape[2:]
        st['k_p'].append(kn)
        st['v_p'].append(proj)
        st['conv_p'].append(conv_p)
        st['re_p'].append(re_p[:, 0].reshape((n_batch,) + g_shape))
        st['im_p'].append(im_p[:, 0].reshape((n_batch,) + g_shape))
        st['k_s'].append(k_new.reshape(n_b, n_new, n_heads, HEAD_DIM))
        st['v_s'].append(v_new.reshape(n_b, n_new, n_heads, HEAD_DIM))
        st['conv_s'].append(conv_s_tm.transpose(1, 0, 2))
        st['re_s'].append(re_s.reshape((n_b,) + g_shape))
        st['im_s'].append(im_s.reshape((n_b,) + g_shape))

    y_p = x[:tp].reshape(n_batch, L, D)
    y_s = x[tp:].reshape(n_new, n_b, D).transpose(1, 0, 2)
    stk = lambda k: jnp.stack(st[k])
    kt, vt = kv_window(st['k_p'], st['v_p'], n_batch, L, c_b, v_col0)
    window_rows = lambda a: a.reshape(depth, n_batch, n_heads, HEAD_DIM, L).transpose(0, 1, 4, 2, 3)
    return (y_p, y_s, window_rows(kt), window_rows(vt), stk('conv_p'), stk('re_p'), stk('im_p'),
            stk('k_s'), stk('v_s'), stk('conv_s'), stk('re_s'), stk('im_s'))
```

```python
import functools
import math

import numpy as np
import jax
import jax.numpy as jnp
from jax import lax
from jax.experimental import pallas as pl
from jax.experimental.pallas import tpu as pltpu

F32 = jnp.float32
BF16 = jnp.bfloat16
EPS = 1e-6
NEG = -1e30

HEAD_DIM = 64
ATTN_SCALE = HEAD_DIM ** -0.5
ATTN_BLOCK = 128
DILATIONS = (1, 4, 16)
DIL_STEPS = 128
CONV_WIDTH = 31
CONV_PAST = CONV_WIDTH - 1
SSM_GROUP = 16
SSM_STATE = 64
N_EXPERTS = 8
TOP_K = 2
LANES = 128
SUBLANES = 8
VMEM_LIMIT = 56 * 1024 * 1024


def _cparams(sem, vmem=VMEM_LIMIT):
    return pltpu.CompilerParams(dimension_semantics=sem, vmem_limit_bytes=vmem)


def _nt_dot(a, b):
    return lax.dot_general(a, b, (((1,), (1,)), ((), ())), preferred_element_type=F32)


def _sigmoid(x):
    return 1.0 / (1.0 + jnp.exp(-x))


def _silu(x):
    return x * _sigmoid(x)


def _gelu_tanh(x):
    c = math.sqrt(2.0 / math.pi)
    return 0.5 * x * (1.0 + jnp.tanh(c * (x + 0.044715 * (x * x * x))))


def _norm_mm_kernel(x_ref, g_ref, w_ref, o_ref, u_scr):
    @pl.when(pl.program_id(1) == 0)
    def _():
        x = x_ref[...]
        ms = jnp.mean(x * x, axis=-1, keepdims=True)
        u_scr[...] = (x * lax.rsqrt(ms + EPS) * g_ref[...]).astype(BF16)

    o_ref[...] = jnp.dot(u_scr[...], w_ref[...].astype(BF16), preferred_element_type=F32)


def norm_matmul(x, g, w_stack, layer, tm, tn):
    T, D = x.shape
    N = w_stack.shape[-1]
    return pl.pallas_call(
        _norm_mm_kernel,
        grid=(T // tm, N // tn),
        in_specs=[
            pl.BlockSpec((tm, D), lambda i, j: (i, 0)),
            pl.BlockSpec((1, D), lambda i, j: (0, 0)),
            pl.BlockSpec((None, D, tn), lambda i, j: (layer, 0, j)),
        ],
        out_specs=pl.BlockSpec((tm, tn), lambda i, j: (i, j)),
        out_shape=jax.ShapeDtypeStruct((T, N), F32),
        scratch_shapes=[pltpu.VMEM((tm, D), BF16)],
        compiler_params=_cparams(("parallel", "arbitrary")),
        name="norm_matmul",
    )(x, g.reshape(1, D), w_stack)


def _qknorm_kernel(q_ref, k_ref, gq_ref, gk_ref, qo_ref, ko_ref):
    tm = q_ref.shape[0]
    lane = lax.broadcasted_iota(jnp.int32, (tm, LANES), 1)
    head0 = lane < HEAD_DIM

    def norm(x, g):
        sq = x * x
        s0 = jnp.sum(jnp.where(head0, sq, 0.0), axis=-1, keepdims=True)
        s1 = jnp.sum(jnp.where(head0, 0.0, sq), axis=-1, keepdims=True)
        ms = jnp.where(head0, s0, s1) * (1.0 / HEAD_DIM)
        return x * lax.rsqrt(ms + EPS) * g

    for t in range(q_ref.shape[1] // LANES):
        sl = slice(t * LANES, (t + 1) * LANES)
        qo_ref[:, sl] = norm(q_ref[:, sl], gq_ref[...]) * ATTN_SCALE
        ko_ref[:, sl] = norm(k_ref[:, sl], gk_ref[...])


def qk_norm(proj, gq, gk, c_b, tm):
    T = proj.shape[0]
    gq2 = jnp.concatenate([gq, gq]).reshape(1, LANES)
    gk2 = jnp.concatenate([gk, gk]).reshape(1, LANES)
    qblk = 1
    return pl.pallas_call(
        _qknorm_kernel,
        grid=(T // tm,),
        in_specs=[
            pl.BlockSpec((tm, c_b), lambda i: (i, qblk)),
            pl.BlockSpec((tm, c_b), lambda i: (i, qblk + 1)),
            pl.BlockSpec((1, LANES), lambda i: (0, 0)),
            pl.BlockSpec((1, LANES), lambda i: (0, 0)),
        ],
        out_specs=[pl.BlockSpec((tm, c_b), lambda i: (i, 0)),
                   pl.BlockSpec((tm, c_b), lambda i: (i, 0))],
        out_shape=[jax.ShapeDtypeStruct((T, c_b), F32)] * 2,
        compiler_params=_cparams(("parallel",)),
        name="qk_norm",
    )(proj, proj, gq2, gk2)


def _attn_prompt_kernel(q_ref, k_ref, v_ref, o_ref,
                        q0_s, q1_s, k_s, v_s,
                        m1, l1, a1, m4, l4, a4, m16, l16, a16):
    L = q_ref.shape[0]
    B = ATTN_BLOCK
    head0 = lax.broadcasted_iota(jnp.int32, (B, LANES), 1) < HEAD_DIM
    qi = lax.broadcasted_iota(jnp.int32, (2 * B, B), 0) % B
    ki = lax.broadcasted_iota(jnp.int32, (2 * B, B), 1)
    tri_cur = ki <= qi
    tri_prev = ki >= qi

    stats = {1: (m1, l1, a1), 4: (m4, l4, a4), 16: (m16, l16, a16)}

    for d in DILATIONS:
        n = L // d
        nb = n // B
        m_s, l_s, a_s = stats[d]
        for r in range(d):
            src = pl.ds(r, n, stride=d) if d > 1 else slice(None)
            dst = slice(r * n, (r + 1) * n)
            hm = lax.broadcasted_iota(jnp.int32, (n, LANES), 1) < HEAD_DIM
            q = q_ref[src, :]
            q0_s[dst, :] = jnp.where(hm, q, 0.0).astype(BF16)
            q1_s[dst, :] = jnp.where(hm, 0.0, q).astype(BF16)
            k_s[dst, :] = k_ref[src, :].astype(BF16)
            v_s[dst, :] = v_ref[src, :].astype(BF16)

        def block(j, carry, nb=nb, m_s=m_s, l_s=l_s, a_s=a_s):
            cur = pl.ds(pl.multiple_of(j * B, B), B)
            qb = jnp.concatenate([q0_s[cur, :], q1_s[cur, :]], axis=0)
            s_c = jnp.where(tri_cur, _nt_dot(qb, k_s[cur, :]), NEG)
            if nb > 1:
                prev = pl.ds(pl.multiple_of(jnp.maximum(j - 1, 0) * B, B), B)
                mask_prev = jnp.logical_and(tri_prev, (j % nb) > 0)
                s_p = jnp.where(mask_prev, _nt_dot(qb, k_s[prev, :]), NEG)
                m = jnp.max(jnp.maximum(s_c, s_p), axis=-1, keepdims=True)
                p_c, p_p = jnp.exp(s_c - m), jnp.exp(s_p - m)
                den = jnp.sum(p_c + p_p, axis=-1, keepdims=True)
                pv = (jnp.dot(p_c.astype(BF16), v_s[cur, :], preferred_element_type=F32)
                      + jnp.dot(p_p.astype(BF16), v_s[prev, :], preferred_element_type=F32))
            else:
                m = jnp.max(s_c, axis=-1, keepdims=True)
                p_c = jnp.exp(s_c - m)
                den = jnp.sum(p_c, axis=-1, keepdims=True)
                pv = jnp.dot(p_c.astype(BF16), v_s[cur, :], preferred_element_type=F32)
            m_s[cur, :] = jnp.where(head0, m[:B], m[B:])
            l_s[cur, :] = jnp.where(head0, den[:B], den[B:])
            a_s[cur, :] = jnp.where(head0, pv[:B], pv[B:])
            return carry

        lax.fori_loop(0, L // B, block, 0, unroll=16)

    dmax = DILATIONS[-1]
    nrow = L // dmax
    for r in range(dmax):
        o1 = pl.ds(r, nrow, stride=dmax)
        o4 = pl.ds((r % 4) * (L // 4) + r // 4, nrow, stride=dmax // 4)
        o16 = slice(r * nrow, (r + 1) * nrow)
        mm1, mm4, mm16 = m1[o1, :], m4[o4, :], m16[o16, :]
        mx = jnp.maximum(jnp.maximum(mm1, mm4), mm16)
        w1, w4, w16 = jnp.exp(mm1 - mx), jnp.exp(mm4 - mx), jnp.exp(mm16 - mx)
        num = w1 * a1[o1, :] + w4 * a4[o4, :] + w16 * a16[o16, :]
        den = w1 * l1[o1, :] + w4 * l4[o4, :] + w16 * l16[o16, :]
        o_ref[o1, :] = num / den


def attn_prompt(qn, kn, proj, n_batch, L, c_b, v_col0):
    n_hp = c_b // LANES
    vblk = v_col0 // LANES
    stat = [pltpu.VMEM((L, LANES), F32)] * 9
    return pl.pallas_call(
        _attn_prompt_kernel,
        grid=(n_batch, n_hp),
        in_specs=[
            pl.BlockSpec((L, LANES), lambda b, h: (b, h)),
            pl.BlockSpec((L, LANES), lambda b, h: (b, h)),
            pl.BlockSpec((L, LANES), lambda b, h: (b, vblk + h)),
        ],
        out_specs=pl.BlockSpec((L, LANES), lambda b, h: (b, h)),
        out_shape=jax.ShapeDtypeStruct((n_batch * L, c_b), F32),
        scratch_shapes=[pltpu.VMEM((L, LANES), BF16)] * 4 + stat,
        compiler_params=_cparams(("parallel", "parallel")),
        name="attn_prompt",
    )(qn, kn, proj)


def _kv_window_kernel(*refs, depth):
    k_refs, v_refs = refs[:depth], refs[depth:2 * depth]
    kt_ref, vt_ref = refs[2 * depth:]
    layer = pl.program_id(0)
    for l in range(depth):
        @pl.when(layer == l)
        def _(l=l):
            kt_ref[...] = k_refs[l][...].T
            vt_ref[...] = v_refs[l][...].T


def kv_window(kns, projs, n_batch, L, c_b, v_col0):
    depth = len(kns)
    n_hp = c_b // LANES
    vblk = v_col0 // LANES

    def spec(l, col0):
        def imap(layer, b, h):
            return jnp.where(layer == l, b, 0), col0 + jnp.where(layer == l, h, 0)
        return pl.BlockSpec((L, LANES), imap)

    out_spec = pl.BlockSpec((None, None, LANES, L), lambda layer, b, h: (layer, b, h, 0))
    out = jax.ShapeDtypeStruct((depth, n_batch, c_b, L), F32)
    return pl.pallas_call(
        functools.partial(_kv_window_kernel, depth=depth),
        grid=(depth, n_batch, n_hp),
        in_specs=[spec(l, 0) for l in range(depth)] + [spec(l, vblk) for l in range(depth)],
        out_specs=[out_spec, out_spec],
        out_shape=[out, out],
        compiler_params=_cparams(("arbitrary", "arbitrary", "arbitrary")),
        name="kv_window",
    )(*kns, *projs)


def _branch_multiplicity(dist):
    c = np.zeros(dist.shape, np.float32)
    for d in DILATIONS:
        c += ((dist >= 0) & (dist % d == 0) & (dist <= d * DIL_STEPS)).astype(np.float32)
    return c


NEW_ROWS_PAD = 16


def _sample_weights(n_new, past_len, pos, n_heads):
    s = np.repeat(np.arange(n_new), n_heads)[:, None]
    return _branch_multiplicity(past_len + s - np.asarray(pos)[None, :]).astype(np.float32)


def _attn_sample_kernel(q_ref, kt_ref, vt_ref, kn_ref, vn_ref, w_ref, wn_ref, o_ref):
    n_heads, hd, window = kt_ref.shape
    width = n_heads * hd
    q = q_ref[...]
    nq = q.shape[0]
    q2 = jnp.concatenate([q, q], axis=-1)
    qt = jnp.concatenate([q2] * (width // (2 * hd)), axis=-1)
    own_head = (lax.broadcasted_iota(jnp.int32, (nq, width), 0) % n_heads
                == lax.broadcasted_iota(jnp.int32, (nq, width), 1) // hd)
    qbd = jnp.where(own_head, qt, 0.0).astype(BF16)

    w, wn = w_ref[...], wn_ref[...]
    s_c = jnp.dot(qbd, kt_ref[...].reshape(width, window).astype(BF16),
                  preferred_element_type=F32)
    s_n = _nt_dot(qbd, kn_ref[...].astype(BF16))
    s_c = jnp.where(w > 0.0, s_c, NEG)
    s_n = jnp.where(wn > 0.0, s_n, NEG)
    m = jnp.maximum(jnp.max(s_c, axis=-1, keepdims=True), jnp.max(s_n, axis=-1, keepdims=True))
    p_c = w * jnp.exp(s_c - m)
    p_n = wn * jnp.exp(s_n - m)
    den = jnp.sum(p_c, axis=-1, keepdims=True) + jnp.sum(p_n, axis=-1, keepdims=True)
    full = (_nt_dot(p_c.astype(BF16), vt_ref[...].reshape(width, window).astype(BF16))
            + jnp.dot(p_n.astype(BF16), vn_ref[...].astype(BF16), preferred_element_type=F32))
    full = jnp.where(own_head, full, 0.0)
    acc = full[:, 0:LANES]
    for t in range(1, width // LANES):
        acc = acc + full[:, t * LANES:(t + 1) * LANES]
    o_ref[...] = (acc[:, :hd] + acc[:, hd:]) / den


def attn_sample(q_b, k_new, v_new, cache_k, cache_v, layer, w, wn):
    n_b, nq, _ = q_b.shape
    depth, _, window, n_heads, _ = cache_k.shape
    width = n_heads * HEAD_DIM
    kt = cache_k.transpose(0, 1, 3, 4, 2)
    vt = cache_v.transpose(0, 1, 3, 4, 2)
    cache_spec = pl.BlockSpec((None, None, n_heads, HEAD_DIM, window), lambda b: (layer, b, 0, 0, 0))
    new_spec = pl.BlockSpec((None, NEW_ROWS_PAD, width), lambda b: (b, 0, 0))
    q_spec = pl.BlockSpec((None, nq, HEAD_DIM), lambda b: (b, 0, 0))

    def const_spec(a):
        return pl.BlockSpec(a.shape, lambda b: (0, 0))

    return pl.pallas_call(
        _attn_sample_kernel,
        grid=(n_b,),
        in_specs=[q_spec, cache_spec, cache_spec, new_spec, new_spec, const_spec(w), const_spec(wn)],
        out_specs=q_spec,
        out_shape=jax.ShapeDtypeStruct((n_b, nq, HEAD_DIM), F32),
        compiler_params=_cparams(("parallel",)),
        name="attn_sample",
    )(q_b, kt, vt, k_new, v_new, w, wn)


def _layernorm_silu(y, g, b):
    mu = jnp.mean(y, axis=-1, keepdims=True)
    yc = y - mu
    var = jnp.mean(yc * yc, axis=-1, keepdims=True)
    return _silu(yc * lax.rsqrt(var + EPS) * g + b)


CONV_HALO = 32
CONV_ROWS = 32


def _conv_prompt_kernel(val_ref, gate_ref, wdw_ref, bdw_ref, lng_ref, lnb_ref, wpw_ref,
                        y_ref, st_ref, xp_s, y_s):
    c = pl.program_id(1)
    lc, ca = val_ref.shape
    off = CONV_HALO - CONV_PAST

    @pl.when(c == 0)
    def _():
        xp_s[0:CONV_HALO, :] = jnp.zeros((CONV_HALO, ca), F32)

    @pl.when(c > 0)
    def _():
        xp_s[0:CONV_HALO, :] = xp_s[lc:lc + CONV_HALO, :]

    xp_s[CONV_HALO:CONV_HALO + lc, :] = val_ref[...] * _sigmoid(gate_ref[...])

    for rb in range(lc // CONV_ROWS):
        for lb in range(ca // LANES):
            ls = slice(lb * LANES, (lb + 1) * LANES)
            acc = jnp.zeros((CONV_ROWS, LANES), F32)
            for j in range(CONV_WIDTH):
                r0 = rb * CONV_ROWS + off + j
                acc = acc + wdw_ref[j:j + 1, ls] * xp_s[r0:r0 + CONV_ROWS, ls]
            y_s[rb * CONV_ROWS:(rb + 1) * CONV_ROWS, ls] = acc

    z = _layernorm_silu(y_s[...] + bdw_ref[...], lng_ref[...], lnb_ref[...])
    y_ref[...] = jnp.dot(z.astype(BF16), wpw_ref[...].astype(BF16), preferred_element_type=F32)

    @pl.when(c == pl.num_programs(1) - 1)
    def _():
        st_ref[...] = xp_s[lc + off:lc + CONV_HALO, :]


def _layer_vec(a, layer):
    return a[layer].reshape(1, -1)


def conv_prompt(proj, n_batch, L, c_a, p, layer, lc=256):
    nch = L // lc
    vec = pl.BlockSpec((1, c_a), lambda b, c: (0, 0))
    return pl.pallas_call(
        _conv_prompt_kernel,
        grid=(n_batch, nch),
        in_specs=[
            pl.BlockSpec((lc, c_a), lambda b, c: (b * nch + c, 0)),
            pl.BlockSpec((lc, c_a), lambda b, c: (b * nch + c, 1)),
            pl.BlockSpec((None, CONV_WIDTH, c_a), lambda b, c: (layer, 0, 0)),
            vec, vec, vec,
            pl.BlockSpec((None, c_a, c_a), lambda b, c: (layer, 0, 0)),
        ],
        out_specs=[pl.BlockSpec((lc, c_a), lambda b, c: (b * nch + c, 0)),
                   pl.BlockSpec((None, CONV_PAST, c_a), lambda b, c: (b, 0, 0))],
        out_shape=[jax.ShapeDtypeStruct((n_batch * L, c_a), F32),
                   jax.ShapeDtypeStruct((n_batch, CONV_PAST, c_a), F32)],
        scratch_shapes=[pltpu.VMEM((lc + CONV_HALO, c_a), F32), pltpu.VMEM((lc, c_a), F32)],
        compiler_params=_cparams(("parallel", "arbitrary")),
        name="conv_prompt",
    )(proj, proj, p['a_w_dw'], _layer_vec(p['a_b_dw'], layer), _layer_vec(p['a_ln_g'], layer),
      _layer_vec(p['a_ln_b'], layer), p['a_w_pw'])


def _conv_sample_kernel(val_ref, gate_ref, past_ref, wdw_ref, bdw_ref, lng_ref, lnb_ref, wpw_ref,
                        y_ref, st_ref, y_s):
    n_b = past_ref.shape[1]
    n_new = val_ref.shape[0] // n_b
    g = val_ref[...] * _sigmoid(gate_ref[...])

    def xp(t):
        if t < CONV_PAST:
            return past_ref[t]
        return g[(t - CONV_PAST) * n_b:(t - CONV_PAST + 1) * n_b, :]

    for s in range(n_new):
        acc = jnp.zeros_like(xp(0))
        for j in range(CONV_WIDTH):
            acc = acc + wdw_ref[j:j + 1, :] * xp(s + j)
        y_s[s * n_b:(s + 1) * n_b, :] = acc
    z = _layernorm_silu(y_s[...] + bdw_ref[...], lng_ref[...], lnb_ref[...])
    y_ref[...] = jnp.dot(z.astype(BF16), wpw_ref[...].astype(BF16), preferred_element_type=F32)
    for t in range(CONV_PAST):
        st_ref[t] = xp(t + n_new)


def conv_sample(proj, row0, n_new, n_b, c_a, past_tm, p, layer):
    ts = n_new * n_b
    vec = pl.BlockSpec((1, c_a), lambda i: (0, 0))
    return pl.pallas_call(
        _conv_sample_kernel,
        grid=(1,),
        in_specs=[
            pl.BlockSpec((ts, c_a), lambda i: (row0 // ts, 0)),
            pl.BlockSpec((ts, c_a), lambda i: (row0 // ts, 1)),
            pl.BlockSpec((None, CONV_PAST, n_b, c_a), lambda i: (layer, 0, 0, 0)),
            pl.BlockSpec((None, CONV_WIDTH, c_a), lambda i: (layer, 0, 0)),
            vec, vec, vec,
            pl.BlockSpec((None, c_a, c_a), lambda i: (layer, 0, 0)),
        ],
        out_specs=[pl.BlockSpec((ts, c_a), lambda i: (0, 0)),
                   pl.BlockSpec((CONV_PAST, n_b, c_a), lambda i: (0, 0, 0))],
        out_shape=[jax.ShapeDtypeStruct((ts, c_a), F32),
                   jax.ShapeDtypeStruct((CONV_PAST, n_b, c_a), F32)],
        scratch_shapes=[pltpu.VMEM((ts, c_a), F32)],
        compiler_params=_cparams(("arbitrary",)),
        name="conv_sample",
    )(proj, proj, past_tm, p['a_w_dw'], _layer_vec(p['a_b_dw'], layer),
      _layer_vec(p['a_ln_g'], layer), _layer_vec(p['a_ln_b'], layer), p['a_w_pw'])


def _s5_param_kernel(are_ref, aim_ref, ldt_ref, bre_ref, bim_ref,
                     pre_ref, pim_ref, bbre_ref, bbim_ref):
    a_re, a_im = are_ref[...], aim_ref[...]
    dt = jnp.exp(ldt_ref[...])
    mag = jnp.exp(a_re * dt)
    ab_re, ab_im = mag * jnp.cos(a_im * dt), mag * jnp.sin(a_im * dt)
    nr, ni = ab_re - 1.0, ab_im
    inv = 1.0 / (a_re * a_re + a_im * a_im)
    f_re = (nr * a_re + ni * a_im) * inv
    f_im = (ni * a_re - nr * a_im) * inv
    b_re, b_im = bre_ref[...], bim_ref[...]
    bbre_ref[...] = f_re * b_re - f_im * b_im
    bbim_ref[...] = f_re * b_im + f_im * b_re
    pr, pi = ab_re, ab_im
    pre_ref[0] = pr
    pim_ref[0] = pi
    for k in range(1, SUBLANES):
        pr, pi = pr * ab_re - pi * ab_im, pr * ab_im + pi * ab_re
        pre_ref[k] = pr
        pim_ref[k] = pi


def s5_params(p, layer):
    a_re, a_im = p['c_a_re'][layer], p['c_a_im'][layer]
    G, N = a_re.shape
    C = SSM_GROUP
    b_re_t = p['c_b_re'][layer].transpose(0, 2, 1)
    b_im_t = p['c_b_im'][layer].transpose(0, 2, 1)
    pre, pim, bbre, bbim = pl.pallas_call(
        _s5_param_kernel,
        out_shape=[jax.ShapeDtypeStruct((SUBLANES, G, 1, N), F32)] * 2
        + [jax.ShapeDtypeStruct((G, C, N), F32)] * 2,
        name="s5_params",
    )(a_re.reshape(G, 1, N), a_im.reshape(G, 1, N), p['c_log_dt'][layer].reshape(G, 1, 1),
      b_re_t, b_im_t)
    S = G * N
    pre, pim = pre.reshape(SUBLANES, S), pim.reshape(SUBLANES, S)
    t = np.arange(SUBLANES)[:, None]
    tabs = []
    for k in (1, 2, 4):
        keep = jnp.asarray(t >= k)
        tabs += [jnp.where(keep, pre[k - 1][None, :], 0.0), jnp.where(keep, pim[k - 1][None, :], 0.0)]
    tabs += [pre, pim]
    tab = jnp.stack(tabs)
    eye = jnp.eye(G, dtype=F32)

    def in_proj(bb):
        return (eye[:, None, :, None] * bb[:, :, None, :]).reshape(G * C, S).astype(BF16)

    def out_proj(c):
        return (eye[:, None, :, None] * c.transpose(0, 2, 1)[:, :, None, :]).reshape(S, G * C)

    c_cat = jnp.concatenate([out_proj(p['c_c_re'][layer]), -out_proj(p['c_c_im'][layer])]).astype(BF16)
    return dict(tab=tab, bb_re=in_proj(bbre), bb_im=in_proj(bbim), c_cat=c_cat,
                ab_re=pre[0:1], ab_im=pim[0:1])


def _s5_readout(u, xr, xi, ccat_ref, d_ref, wglu_ref, bglu_ref):
    S = xr.shape[-1]
    y = (jnp.dot(xr.astype(BF16), ccat_ref[0:S, :], preferred_element_type=F32)
         + jnp.dot(xi.astype(BF16), ccat_ref[S:2 * S, :], preferred_element_type=F32)
         + d_ref[...] * u)
    z = _gelu_tanh(y)
    gate = jnp.dot(z.astype(BF16), wglu_ref[...].astype(BF16), preferred_element_type=F32)
    return z * _sigmoid(gate + bglu_ref[...])


S5_LANE_GROUP = 512


def _s5_prompt_kernel(u_ref, bbre_ref, bbim_ref, tab_ref, ccat_ref, d_ref, wglu_ref, bglu_ref,
                      y_ref, hre_ref, him_ref, xr_s, xi_s, h_s):
    c = pl.program_id(1)
    tc = u_ref.shape[0]
    S = xr_s.shape[1]

    @pl.when(c == 0)
    def _():
        h_s[...] = jnp.zeros_like(h_s)

    u = u_ref[...]
    ub = u.astype(BF16)
    xr_s[...] = jnp.dot(ub, bbre_ref[...], preferred_element_type=F32)
    xi_s[...] = jnp.dot(ub, bbim_ref[...], preferred_element_type=F32)

    for jg in range(S // S5_LANE_GROUP):
        ls = slice(jg * S5_LANE_GROUP, (jg + 1) * S5_LANE_GROUP)
        levels = [(k, tab_ref[2 * i, :, ls], tab_ref[2 * i + 1, :, ls])
                  for i, k in enumerate((1, 2, 4))]
        cr, ci = tab_ref[6, :, ls], tab_ref[7, :, ls]

        def tile(i, carry, ls=ls, levels=levels, cr=cr, ci=ci):
            hr, hi = carry
            rows = pl.ds(pl.multiple_of(i * SUBLANES, SUBLANES), SUBLANES)
            br, bi = xr_s[rows, ls], xi_s[rows, ls]
            for k, pr, pi in levels:
                sr, si = pltpu.roll(br, k, axis=0), pltpu.roll(bi, k, axis=0)
                br, bi = br + pr * sr - pi * si, bi + pr * si + pi * sr
            br, bi = br + cr * hr - ci * hi, bi + cr * hi + ci * hr
            xr_s[rows, ls] = br
            xi_s[rows, ls] = bi
            last = slice(SUBLANES - 1, SUBLANES)
            return (jnp.broadcast_to(br[last, :], br.shape), jnp.broadcast_to(bi[last, :], bi.shape))

        hr, hi = lax.fori_loop(0, tc // SUBLANES, tile, (h_s[0, :, ls], h_s[1, :, ls]))
        h_s[0, :, ls] = hr
        h_s[1, :, ls] = hi

    y_ref[...] = _s5_readout(u, xr_s[...], xi_s[...], ccat_ref, d_ref, wglu_ref, bglu_ref)
    hre_ref[...] = h_s[0]
    him_ref[...] = h_s[1]


def s5_prompt(proj, n_batch, L, c_c, u_col0, sp, p, layer, tc=256):
    nch = L // tc
    S = sp['tab'].shape[-1]
    ublk = u_col0 // c_c

    def full(a):
        return pl.BlockSpec(a.shape, lambda b, c: (0,) * a.ndim)

    d = p['c_d'][layer].reshape(1, c_c)
    bglu = p['c_b_glu'][layer].reshape(1, c_c)
    return pl.pallas_call(
        _s5_prompt_kernel,
        grid=(n_batch, nch),
        in_specs=[
            pl.BlockSpec((tc, c_c), lambda b, c: (b * nch + c, ublk)),
            full(sp['bb_re']), full(sp['bb_im']), full(sp['tab']), full(sp['c_cat']), full(d),
            pl.BlockSpec((None, c_c, c_c), lambda b, c: (layer, 0, 0)),
            full(bglu),
        ],
        out_specs=[pl.BlockSpec((tc, c_c), lambda b, c: (b * nch + c, 0)),
                   pl.BlockSpec((None, SUBLANES, S), lambda b, c: (b, 0, 0)),
                   pl.BlockSpec((None, SUBLANES, S), lambda b, c: (b, 0, 0))],
        out_shape=[jax.ShapeDtypeStruct((n_batch * L, c_c), F32),
                   jax.ShapeDtypeStruct((n_batch, SUBLANES, S), F32),
                   jax.ShapeDtypeStruct((n_batch, SUBLANES, S), F32)],
        scratch_shapes=[pltpu.VMEM((tc, S), F32), pltpu.VMEM((tc, S), F32),
                        pltpu.VMEM((2, SUBLANES, S), F32)],
        compiler_params=_cparams(("parallel", "arbitrary")),
        name="s5_prompt",
    )(proj, sp['bb_re'], sp['bb_im'], sp['tab'], sp['c_cat'], d, p['c_w_glu'], bglu)


def _s5_sample_kernel(u_ref, h0re_ref, h0im_ref, bbre_ref, bbim_ref, abre_ref, abim_ref,
                      ccat_ref, d_ref, wglu_ref, bglu_ref,
                      y_ref, hre_ref, him_ref, xr_s, xi_s):
    n_b = h0re_ref.shape[0]
    n_new = u_ref.shape[0] // n_b
    u = u_ref[...]
    ub = u.astype(BF16)
    xr_s[...] = jnp.dot(ub, bbre_ref[...], preferred_element_type=F32)
    xi_s[...] = jnp.dot(ub, bbim_ref[...], preferred_element_type=F32)
    ar, ai = abre_ref[...], abim_ref[...]
    hr, hi = h0re_ref[...], h0im_ref[...]
    for s in range(n_new):
        rows = slice(s * n_b, (s + 1) * n_b)
        hr, hi = ar * hr - ai * hi + xr_s[rows, :], ar * hi + ai * hr + xi_s[rows, :]
        xr_s[rows, :] = hr
        xi_s[rows, :] = hi
    y_ref[...] = _s5_readout(u, xr_s[...], xi_s[...], ccat_ref, d_ref, wglu_ref, bglu_ref)
    hre_ref[...] = hr
    him_ref[...] = hi


def s5_sample(proj, row0, n_new, n_b, c_c, u_col0, h0_re, h0_im, sp, p, layer):
    ts = n_new * n_b
    S = sp['tab'].shape[-1]

    def full(a):
        return pl.BlockSpec(a.shape, lambda i: (0,) * a.ndim)

    d = p['c_d'][layer].reshape(1, c_c)
    bglu = p['c_b_glu'][layer].reshape(1, c_c)
    return pl.pallas_call(
        _s5_sample_kernel,
        grid=(1,),
        in_specs=[
            pl.BlockSpec((ts, c_c), lambda i: (row0 // ts, u_col0 // c_c)),
            full(h0_re), full(h0_im), full(sp['bb_re']), full(sp['bb_im']),
            full(sp['ab_re']), full(sp['ab_im']), full(sp['c_cat']), full(d),
            pl.BlockSpec((None, c_c, c_c), lambda i: (layer, 0, 0)),
            full(bglu),
        ],
        out_specs=[pl.BlockSpec((ts, c_c), lambda i: (0, 0)),
                   pl.BlockSpec((n_b, S), lambda i: (0, 0)),
                   pl.BlockSpec((n_b, S), lambda i: (0, 0))],
        out_shape=[jax.ShapeDtypeStruct((ts, c_c), F32),
                   jax.ShapeDtypeStruct((n_b, S), F32),
                   jax.ShapeDtypeStruct((n_b, S), F32)],
        scratch_shapes=[pltpu.VMEM((ts, S), F32), pltpu.VMEM((ts, S), F32)],
        compiler_params=_cparams(("arbitrary",)),
        name="s5_sample",
    )(proj, h0_re, h0_im, sp['bb_re'], sp['bb_im'], sp['ab_re'], sp['ab_im'], sp['c_cat'], d,
      p['c_w_glu'], bglu)


def _out_proj_kernel(yap_ref, ybp_ref, ycp_ref, yas_ref, ybs_ref, ycs_ref, w_ref, x_ref,
                     o_ref, cat_s, w_s, *, n_prompt_tiles):
    i = pl.program_id(1)

    @pl.when(i == 0)
    def _():
        w_s[...] = w_ref[...].astype(BF16)

    def fill(ya_ref, yb_ref, yc_ref):
        ca, cb = ya_ref.shape[1], yb_ref.shape[1]
        cat_s[:, 0:ca] = ya_ref[...].astype(BF16)
        cat_s[:, ca:ca + cb] = yb_ref[...].astype(BF16)
        cat_s[:, ca + cb:] = yc_ref[...].astype(BF16)

    @pl.when(i < n_prompt_tiles)
    def _():
        fill(yap_ref, ybp_ref, ycp_ref)

    @pl.when(i >= n_prompt_tiles)
    def _():
        fill(yas_ref, ybs_ref, ycs_ref)

    o_ref[...] = x_ref[...] + jnp.dot(cat_s[...], w_s[...], preferred_element_type=F32)


def out_proj(prompt_parts, sample_parts, w_stack, layer, x, tm, tn):
    T, D = x.shape
    dm = w_stack.shape[1]
    n_p = prompt_parts[0].shape[0] // tm
    assert prompt_parts[0].shape[0] % tm == 0 and sample_parts[0].shape[0] % tm == 0

    def prompt_spec(a):
        return pl.BlockSpec((tm, a.shape[1]), lambda j, i: (jnp.minimum(i, n_p - 1), 0))

    def sample_spec(a):
        return pl.BlockSpec((tm, a.shape[1]), lambda j, i: (jnp.maximum(i - n_p, 0), 0))

    return pl.pallas_call(
        functools.partial(_out_proj_kernel, n_prompt_tiles=n_p),
        grid=(D // tn, T // tm),
        in_specs=[prompt_spec(a) for a in prompt_parts] + [sample_spec(a) for a in sample_parts]
        + [pl.BlockSpec((None, dm, tn), lambda j, i: (layer, 0, j)),
           pl.BlockSpec((tm, tn), lambda j, i: (i, j))],
        out_specs=pl.BlockSpec((tm, tn), lambda j, i: (i, j)),
        out_shape=jax.ShapeDtypeStruct((T, D), F32),
        scratch_shapes=[pltpu.VMEM((tm, dm), BF16), pltpu.VMEM((dm, tn), BF16)],
        compiler_params=_cparams(("arbitrary", "arbitrary")),
        name="out_proj",
    )(*prompt_parts, *sample_parts, w_stack, x)


def _norm_cast_kernel(x_ref, g_ref, o_ref):
    x = x_ref[...]
    ms = jnp.mean(x * x, axis=-1, keepdims=True)
    o_ref[...] = (x * lax.rsqrt(ms + EPS) * g_ref[...]).astype(BF16)


def norm_cast(x, g, tm):
    T, D = x.shape
    return pl.pallas_call(
        _norm_cast_kernel,
        grid=(T // tm,),
        in_specs=[pl.BlockSpec((tm, D), lambda i: (i, 0)), pl.BlockSpec((1, D), lambda i: (0, 0))],
        out_specs=pl.BlockSpec((tm, D), lambda i: (i, 0)),
        out_shape=jax.ShapeDtypeStruct((T, D), BF16),
        compiler_params=_cparams(("parallel",)),
        name="norm_cast",
    )(x, g.reshape(1, D))


def _new_expert(te_ref, i):
    prev = te_ref[jnp.maximum(i - 1, 0)]
    return jnp.logical_or(i == 0, te_ref[i] != prev)


def _ffn_up_kernel(te_ref, nv_ref, x_ref, w1_ref, w3_ref, h_ref, w1_s, w3_s):
    i = pl.program_id(1)

    @pl.when(_new_expert(te_ref, i))
    def _():
        w1_s[...] = w1_ref[...].astype(BF16)
        w3_s[...] = w3_ref[...].astype(BF16)

    @pl.when(i < nv_ref[0])
    def _():
        x = x_ref[...]
        a = jnp.dot(x, w1_s[...], preferred_element_type=F32)
        b = jnp.dot(x, w3_s[...], preferred_element_type=F32)
        h_ref[...] = (_silu(a) * b).astype(BF16)

    @pl.when(i >= nv_ref[0])
    def _():
        h_ref[...] = jnp.zeros_like(h_ref)


def ffn_up(xs, w1, w3, tile_expert, n_valid, tm, fc):
    R, D = xs.shape
    F = w1.shape[-1]
    wspec = pl.BlockSpec((None, D, fc), lambda j, i, te, nv: (te[i], 0, j))
    return pl.pallas_call(
        _ffn_up_kernel,
        grid_spec=pltpu.PrefetchScalarGridSpec(
            num_scalar_prefetch=2,
            grid=(pl.cdiv(F, fc), R // tm),
            in_specs=[pl.BlockSpec((tm, D), lambda j, i, te, nv: (i, 0)), wspec, wspec],
            out_specs=pl.BlockSpec((tm, fc), lambda j, i, te, nv: (i, j)),
            scratch_shapes=[pltpu.VMEM((D, fc), BF16)] * 2,
        ),
        out_shape=jax.ShapeDtypeStruct((R, F), BF16),
        compiler_params=_cparams(("arbitrary", "arbitrary")),
        name="ffn_up",
    )(tile_expert, n_valid, xs, w1, w3)


def _ffn_down_kernel(te_ref, nv_ref, h_ref, w2_ref, *rest, residual):
    if residual:
        r_ref, o_ref, w2_s = rest
    else:
        o_ref, w2_s = rest
    i = pl.program_id(1)

    @pl.when(_new_expert(te_ref, i))
    def _():
        w2_s[...] = w2_ref[...].astype(BF16)

    @pl.when(i < nv_ref[0])
    def _():
        y = jnp.dot(h_ref[...], w2_s[...], preferred_element_type=F32)
        o_ref[...] = r_ref[...] + y if residual else y

    @pl.when(i >= nv_ref[0])
    def _():
        o_ref[...] = jnp.zeros_like(o_ref)


def ffn_down(h, w2, tile_expert, n_valid, tm, tn, residual=None):
    R, F = h.shape
    D = w2.shape[-1]
    tile = pl.BlockSpec((tm, tn), lambda j, i, te, nv: (i, j))
    in_specs = [pl.BlockSpec((tm, F), lambda j, i, te, nv: (i, 0)),
                pl.BlockSpec((None, F, tn), lambda j, i, te, nv: (te[i], 0, j))]
    args = [h, w2]
    if residual is not None:
        in_specs.append(tile)
        args.append(residual)
    return pl.pallas_call(
        functools.partial(_ffn_down_kernel, residual=residual is not None),
        grid_spec=pltpu.PrefetchScalarGridSpec(
            num_scalar_prefetch=2,
            grid=(D // tn, R // tm),
            in_specs=in_specs,
            out_specs=tile,
            scratch_shapes=[pltpu.VMEM((F, tn), BF16)],
        ),
        out_shape=jax.ShapeDtypeStruct((R, D), F32),
        compiler_params=_cparams(("arbitrary", "arbitrary")),
        name="ffn_down",
    )(tile_expert, n_valid, *args)


def _router_kernel(x_ref, g_ref, wr_ref, br_ref, idx_ref, gate_ref):
    x = x_ref[...]
    ms = jnp.mean(x * x, axis=-1, keepdims=True)
    u = x * lax.rsqrt(ms + EPS) * g_ref[...]
    logits = lax.dot_general(wr_ref[...], u, (((1,), (1,)), ((), ())),
                             precision=lax.Precision.HIGHEST,
                             preferred_element_type=F32) + br_ref[...]
    n_e = logits.shape[0]
    eid = lax.broadcasted_iota(jnp.int32, logits.shape, 0)
    m1 = jnp.max(logits, axis=0, keepdims=True)
    i1 = jnp.min(jnp.where(logits == m1, eid, n_e), axis=0, keepdims=True)
    rest = jnp.where(eid == i1, -jnp.inf, logits)
    m2 = jnp.max(rest, axis=0, keepdims=True)
    i2 = jnp.min(jnp.where(rest == m2, eid, n_e), axis=0, keepdims=True)
    e2 = jnp.exp(m2 - m1)
    g1 = 1.0 / (1.0 + e2)
    idx_ref[...] = jnp.where(eid == 0, i1, i2)
    gate_ref[...] = jnp.where(eid == 0, g1, e2 * g1)


def router(x, g, w_router, b_router, tm):
    T, D = x.shape
    E = w_router.shape[-1]
    return pl.pallas_call(
        _router_kernel,
        grid=(T // tm,),
        in_specs=[pl.BlockSpec((tm, D), lambda i: (i, 0)),
                  pl.BlockSpec((1, D), lambda i: (0, 0)),
                  pl.BlockSpec((E, D), lambda i: (0, 0)),
                  pl.BlockSpec((E, 1), lambda i: (0, 0))],
        out_specs=[pl.BlockSpec((E, tm), lambda i: (0, i)),
                   pl.BlockSpec((E, tm), lambda i: (0, i))],
        out_shape=[jax.ShapeDtypeStruct((E, T), jnp.int32), jax.ShapeDtypeStruct((E, T), F32)],
        compiler_params=_cparams(("parallel",)),
        name="moe_router",
    )(x, g.reshape(1, D), w_router.T, b_router.reshape(E, 1))


def _row_copy(src_hbm, row, dst, r, sem):
    return pltpu.make_async_copy(src_hbm.at[pl.ds(row, 1), :], dst.at[pl.ds(r, 1), :], sem)


GATHER_UNROLL = 8


def _rows_loop(tm, fn):
    def body(it, c):
        for u in range(GATHER_UNROLL):
            fn(it * GATHER_UNROLL + u)
        return c

    lax.fori_loop(0, tm // GATHER_UNROLL, body, 0)


def _gather_norm_kernel(src_ref, nv_ref, x_hbm, g_ref, o_ref, buf, sem):
    tm = buf.shape[1]
    i = pl.program_id(0)
    n_used = nv_ref[0]

    def fetch(tile, slot):
        _rows_loop(tm, lambda r: _row_copy(x_hbm, src_ref[tile * tm + r], buf.at[slot], r,
                                           sem.at[slot]).start())

    @pl.when(i == 0)
    def _():
        fetch(0, 0)

    @pl.when(i + 1 < n_used)
    def _():
        fetch(i + 1, (i + 1) % 2)

    @pl.when(i < n_used)
    def _():
        slot = i % 2
        _rows_loop(tm, lambda r: _row_copy(x_hbm, 0, buf.at[slot], r, sem.at[slot]).wait())
        x = buf[slot]
        ms = jnp.mean(x * x, axis=-1, keepdims=True)
        o_ref[...] = (x * lax.rsqrt(ms + EPS) * g_ref[...]).astype(BF16)

    @pl.when(i >= n_used)
    def _():
        o_ref[...] = jnp.zeros_like(o_ref)


def gather_norm(x, g, src, n_used, tm):
    T, D = x.shape
    R = src.shape[0]
    return pl.pallas_call(
        _gather_norm_kernel,
        grid_spec=pltpu.PrefetchScalarGridSpec(
            num_scalar_prefetch=2,
            grid=(R // tm,),
            in_specs=[pl.BlockSpec(memory_space=pl.ANY),
                      pl.BlockSpec((1, D), lambda i, s, n: (0, 0))],
            out_specs=pl.BlockSpec((tm, D), lambda i, s, n: (i, 0)),
            scratch_shapes=[pltpu.VMEM((2, tm, D), F32), pltpu.SemaphoreType.DMA((2,))],
        ),
        out_shape=jax.ShapeDtypeStruct((R, D), BF16),
        compiler_params=_cparams(("arbitrary",)),
        name="moe_gather",
    )(src, n_used, x, g.reshape(1, D))


def _combine_kernel(p0_ref, p1_ref, h_ref, gate_ref, y_hbm, o_ref, buf, sem):
    tm = buf.shape[2]
    i = pl.program_id(0)

    def fetch(tile, slot):
        def one(r):
            _row_copy(y_hbm, p0_ref[tile * tm + r], buf.at[slot, 0], r, sem.at[slot, 0]).start()
            _row_copy(y_hbm, p1_ref[tile * tm + r], buf.at[slot, 1], r, sem.at[slot, 1]).start()

        _rows_loop(tm, one)

    @pl.when(i == 0)
    def _():
        fetch(0, 0)

    @pl.when(i + 1 < pl.num_programs(0))
    def _():
        fetch(i + 1, (i + 1) % 2)

    slot = i % 2

    def wait(r):
        _row_copy(y_hbm, 0, buf.at[slot, 0], r, sem.at[slot, 0]).wait()
        _row_copy(y_hbm, 0, buf.at[slot, 1], r, sem.at[slot, 1]).wait()

    _rows_loop(tm, wait)
    gate = gate_ref[...]
    o_ref[...] = h_ref[...] + gate[:, 0:1] * buf[slot, 0] + gate[:, 1:2] * buf[slot, 1]


def moe_combine(h, gates, ys, pos0, pos1, tm):
    T, D = h.shape
    return pl.pallas_call(
        _combine_kernel,
        grid_spec=pltpu.PrefetchScalarGridSpec(
            num_scalar_prefetch=2,
            grid=(T // tm,),
            in_specs=[pl.BlockSpec((tm, D), lambda i, a, b: (i, 0)),
                      pl.BlockSpec((tm, TOP_K), lambda i, a, b: (i, 0)),
                      pl.BlockSpec(memory_space=pl.ANY)],
            out_specs=pl.BlockSpec((tm, D), lambda i, a, b: (i, 0)),
            scratch_shapes=[pltpu.VMEM((2, TOP_K, tm, D), F32), pltpu.SemaphoreType.DMA((2, TOP_K))],
        ),
        out_shape=jax.ShapeDtypeStruct((T, D), F32),
        compiler_params=_cparams(("arbitrary",)),
        name="moe_combine",
    )(pos0, pos1, h, gates, ys)


MOE_TM = 512
MOE_TM_DOWN = 512
GATHER_TM = 256


def moe_layout(idx, tm):
    T, K = idx.shape
    flat = idx.reshape(-1)
    onehot = (flat[:, None] == jnp.arange(N_EXPERTS)[None, :]).astype(jnp.int32)
    rank = jnp.take_along_axis(jnp.cumsum(onehot, axis=0) - onehot, flat[:, None], axis=1)[:, 0]
    counts = jnp.sum(onehot, axis=0)
    tiles = (counts + tm - 1) // tm
    tile_end = jnp.cumsum(tiles)
    start = (tile_end - tiles) * tm
    pos = start[flat] + rank
    n_tiles = (T * K) // tm + N_EXPERTS
    src = jnp.zeros((n_tiles * tm,), jnp.int32).at[pos].set(jnp.arange(T * K, dtype=jnp.int32) // K)
    owner = jnp.sum((jnp.arange(n_tiles)[:, None] >= tile_end[None, :]).astype(jnp.int32), axis=1)
    tile_expert = jnp.minimum(owner, N_EXPERTS - 1).astype(jnp.int32)
    n_valid = tile_end[-1:].astype(jnp.int32)
    return pos.reshape(T, K).astype(jnp.int32), src, tile_expert, n_valid


def moe_ffn(h, g, w_router, b_router, w1, w3, w2):
    T, D = h.shape
    idx_t, gate_t = router(h, g, w_router, b_router, tm=512)
    idx = idx_t[:TOP_K].T
    gates = gate_t[:TOP_K].T
    pos, src, tile_expert, n_valid = moe_layout(idx, MOE_TM)
    xs = gather_norm(h, g, src, n_valid * (MOE_TM // GATHER_TM), GATHER_TM)
    hid = ffn_up(xs, w1, w3, tile_expert, n_valid, MOE_TM, fc=1024)
    split = MOE_TM // MOE_TM_DOWN
    ys = ffn_down(hid, w2, jnp.repeat(tile_expert, split), n_valid * split, MOE_TM_DOWN, tn=512)
    return moe_combine(h, gates, ys, pos[:, 0], pos[:, 1], GATHER_TM)


def dense_ffn(h, g, w1, w3, w2):
    T, D = h.shape
    tm_up, tm_down = T // 8, T // 16

    def one_group(tm):
        return jnp.zeros((T // tm,), jnp.int32), jnp.full((1,), T // tm, jnp.int32)

    xs = norm_cast(h, g, tm=512)
    hid = ffn_up(xs, w1, w3, *one_group(tm_up), tm_up, fc=512)
    return ffn_down(hid, w2, *one_group(tm_down), tm_down, tn=512, residual=h)


def kernel(x_prompt, x_sample, cache_k, cache_v, state_conv, state_ssm_re, state_ssm_im,
           norm1_g, w_in, w_out, a_w_dw, a_b_dw, a_ln_g, a_ln_b, a_w_pw, b_q_g, b_k_g,
           c_a_re, c_a_im, c_log_dt, c_b_re, c_b_im, c_c_re, c_c_im, c_d, c_w_glu, c_b_glu,
           norm2_g, ffn_w1, ffn_w3, ffn_w2, moe_w_router, moe_b_router, moe_w1, moe_w3, moe_w2):
    n_batch, L, D = x_prompt.shape
    n_b, n_new, _ = x_sample.shape
    depth = w_in.shape[0]
    c_a = a_w_pw.shape[-1]
    c_c = c_w_glu.shape[-1]
    c_b = D - c_a - c_c
    n_heads = c_b // HEAD_DIM
    window = cache_k.shape[2]
    tp, ts = n_batch * L, n_b * n_new
    T = tp + ts
    q_col0, v_col0, u_col0 = 2 * c_a, 2 * c_a + 2 * c_b, 2 * c_a + 3 * c_b
    assert q_col0 == c_b and window == DILATIONS[-1] * DIL_STEPS and L == window
    tm_big = T // 8

    p = dict(a_w_dw=a_w_dw, a_b_dw=a_b_dw, a_ln_g=a_ln_g, a_ln_b=a_ln_b, a_w_pw=a_w_pw,
             c_a_re=c_a_re, c_a_im=c_a_im, c_log_dt=c_log_dt, c_b_re=c_b_re, c_b_im=c_b_im,
             c_c_re=c_c_re, c_c_im=c_c_im, c_d=c_d, c_w_glu=c_w_glu, c_b_glu=c_b_glu)

    x = jnp.concatenate([x_prompt.reshape(tp, D),
                         x_sample.transpose(1, 0, 2).reshape(ts, D)], axis=0)
    conv_past_tm = state_conv.transpose(0, 2, 1, 3)

    w_win = jnp.asarray(_sample_weights(n_new, window, np.arange(window), n_heads))
    w_new = jnp.asarray(_sample_weights(n_new, window, window + np.arange(NEW_ROWS_PAD), n_heads))
    pad_new = lambda a: jnp.pad(a, ((0, 0), (0, NEW_ROWS_PAD - n_new), (0, 0)))

    st = {k: [] for k in ('k_p', 'v_p', 'conv_p', 're_p', 'im_p', 'k_s', 'v_s', 'conv_s', 're_s', 'im_s')}
    for l in range(depth):
        proj = norm_matmul(x, norm1_g[l], w_in, l, tm=tm_big, tn=512)
        qn, kn = qk_norm(proj, b_q_g[l], b_k_g[l], c_b, tm=512)
        ya_p, conv_p = conv_prompt(proj, n_batch, L, c_a, p, l)
        ya_s, conv_s_tm = conv_sample(proj, tp, n_new, n_b, c_a, conv_past_tm, p, l)
        yb_p = attn_prompt(qn, kn, proj, n_batch, L, c_b, v_col0)
        bm = lambda a: a.reshape(n_new, n_b, c_b).transpose(1, 0, 2)
        k_new, v_new = bm(kn[tp:]), bm(proj[tp:, v_col0:v_col0 + c_b])
        q_b = bm(qn[tp:]).reshape(n_b, n_new * n_heads, HEAD_DIM)
        yb_s = attn_sample(q_b, pad_new(k_new), pad_new(v_new), cache_k, cache_v, l, w_win, w_new)
        yb_s = yb_s.reshape(n_b, n_new, c_b).transpose(1, 0, 2).reshape(ts, c_b)
        sp = s5_params(p, l)
        yc_p, re_p, im_p = s5_prompt(proj, n_batch, L, c_c, u_col0, sp, p, l)
        h0_re = state_ssm_re[l].reshape(n_b, -1)
        h0_im = state_ssm_im[l].reshape(n_b, -1)
        yc_s, re_s, im_s = s5_sample(proj, tp, n_new, n_b, c_c, u_col0, h0_re, h0_im, sp, p, l)

        h = out_proj((ya_p, yb_p, yc_p), (ya_s, yb_s, yc_s), w_out, l, x, tm=ts, tn=1024)

        j = l // 2
        if l % 2 == 0:
            x = dense_ffn(h, norm2_g[l], ffn_w1[j:j + 1], ffn_w3[j:j + 1], ffn_w2[j:j + 1])
        else:
            x = moe_ffn(h, norm2_g[l], moe_w_router[j], moe_b_router[j],
                        moe_w1[j], moe_w3[j], moe_w2[j])

        g_shape = state_ssm_re.shape[2:]
        st['k_p'].append(kn)
        st['v_p'].append(proj)
        st['conv_p'].append(conv_p)
        st['re_p'].append(re_p[:, 0].reshape((n_batch,) + g_shape))
        st['im_p'].append(im_p[:, 0].reshape((n_batch,) + g_shape))
        st['k_s'].append(k_new.reshape(n_b, n_new, n_heads, HEAD_DIM))
        st['v_s'].append(v_new.reshape(n_b, n_new, n_heads, HEAD_DIM))
        st['conv_s'].append(conv_s_tm.transpose(1, 0, 2))
        st['re_s'].append(re_s.reshape((n_b,) + g_shape))
        st['im_s'].append(im_s.reshape((n_b,) + g_shape))

    y_p = x[:tp].reshape(n_batch, L, D)
    y_s = x[tp:].reshape(n_new, n_b, D).transpose(1, 0, 2)
    stk = lambda k: jnp.stack(st[k])
    kt, vt = kv_window(st['k_p'], st['v_p'], n_batch, L, c_b, v_col0)
    window_rows = lambda a: a.reshape(depth, n_batch, n_heads, HEAD_DIM, L).transpose(0, 1, 4, 2, 3)
    return (y_p, y_s, window_rows(kt), window_rows(vt), stk('conv_p'), stk('re_p'), stk('im_p'),
            stk('k_s'), stk('v_s'), stk('conv_s'), stk('re_s'), stk('im_s'))
```

```python
import functools
import math

import numpy as np
import jax
import jax.numpy as jnp
from jax import lax
from jax.experimental import pallas as pl
from jax.experimental.pallas import tpu as pltpu

F32 = jnp.float32
BF16 = jnp.bfloat16
EPS = 1e-6
NEG = -1e30

HEAD_DIM = 64
ATTN_SCALE = HEAD_DIM ** -0.5
ATTN_BLOCK = 128
DILATIONS = (1, 4, 16)
DIL_STEPS = 128
CONV_WIDTH = 31
CONV_PAST = CONV_WIDTH - 1
SSM_GROUP = 16
SSM_STATE = 64
N_EXPERTS = 8
TOP_K = 2
LANES = 128
SUBLANES = 8
VMEM_LIMIT = 56 * 1024 * 1024


def _cparams(sem, vmem=VMEM_LIMIT):
    return pltpu.CompilerParams(dimension_semantics=sem, vmem_limit_bytes=vmem)


def _nt_dot(a, b):
    return lax.dot_general(a, b, (((1,), (1,)), ((), ())), preferred_element_type=F32)


def _sigmoid(x):
    return 1.0 / (1.0 + jnp.exp(-x))


def _silu(x):
    return x * _sigmoid(x)


def _gelu_tanh(x):
    c = math.sqrt(2.0 / math.pi)
    return 0.5 * x * (1.0 + jnp.tanh(c * (x + 0.044715 * (x * x * x))))


def _norm_mm_kernel(x_ref, g_ref, w_ref, o_ref, u_scr):
    @pl.when(pl.program_id(1) == 0)
    def _():
        x = x_ref[...]
        ms = jnp.mean(x * x, axis=-1, keepdims=True)
        u_scr[...] = (x * lax.rsqrt(ms + EPS) * g_ref[...]).astype(BF16)

    o_ref[...] = jnp.dot(u_scr[...], w_ref[...].astype(BF16), preferred_element_type=F32)


def norm_matmul(x, g, w_stack, layer, tm, tn):
    T, D = x.shape
    N = w_stack.shape[-1]
    return pl.pallas_call(
        _norm_mm_kernel,
        grid=(T // tm, N // tn),
        in_specs=[
            pl.BlockSpec((tm, D), lambda i, j: (i, 0)),
            pl.BlockSpec((1, D), lambda i, j: (0, 0)),
            pl.BlockSpec((None, D, tn), lambda i, j: (layer, 0, j)),
        ],
        out_specs=pl.BlockSpec((tm, tn), lambda i, j: (i, j)),
        out_shape=jax.ShapeDtypeStruct((T, N), F32),
        scratch_shapes=[pltpu.VMEM((tm, D), BF16)],
        compiler_params=_cparams(("parallel", "arbitrary")),
        name="norm_matmul",
    )(x, g.reshape(1, D), w_stack)


def _qknorm_kernel(q_ref, k_ref, gq_ref, gk_ref, qo_ref, ko_ref):
    tm = q_ref.shape[0]
    lane = lax.broadcasted_iota(jnp.int32, (tm, LANES), 1)
    head0 = lane < HEAD_DIM

    def norm(x, g):
        sq = x * x
        s0 = jnp.sum(jnp.where(head0, sq, 0.0), axis=-1, keepdims=True)
        s1 = jnp.sum(jnp.where(head0, 0.0, sq), axis=-1, keepdims=True)
        ms = jnp.where(head0, s0, s1) * (1.0 / HEAD_DIM)
        return x * lax.rsqrt(ms + EPS) * g

    for t in range(q_ref.shape[1] // LANES):
        sl = slice(t * LANES, (t + 1) * LANES)
        qo_ref[:, sl] = norm(q_ref[:, sl], gq_ref[...]) * ATTN_SCALE
        ko_ref[:, sl] = norm(k_ref[:, sl], gk_ref[...])


def qk_norm(proj, gq, gk, c_b, tm):
    T = proj.shape[0]
    gq2 = jnp.concatenate([gq, gq]).reshape(1, LANES)
    gk2 = jnp.concatenate([gk, gk]).reshape(1, LANES)
    qblk = 1
    return pl.pallas_call(
        _qknorm_kernel,
        grid=(T // tm,),
        in_specs=[
            pl.BlockSpec((tm, c_b), lambda i: (i, qblk)),
            pl.BlockSpec((tm, c_b), lambda i: (i, qblk + 1)),
            pl.BlockSpec((1, LANES), lambda i: (0, 0)),
            pl.BlockSpec((1, LANES), lambda i: (0, 0)),
        ],
        out_specs=[pl.BlockSpec((tm, c_b), lambda i: (i, 0)),
                   pl.BlockSpec((tm, c_b), lambda i: (i, 0))],
        out_shape=[jax.ShapeDtypeStruct((T, c_b), F32)] * 2,
        compiler_params=_cparams(("parallel",)),
        name="qk_norm",
    )(proj, proj, gq2, gk2)


def _attn_prompt_kernel(q_ref, k_ref, v_ref, o_ref,
                        q0_s, q1_s, k_s, v_s,
                        m1, l1, a1, m4, l4, a4, m16, l16, a16):
    L = q_ref.shape[0]
    B = ATTN_BLOCK
    head0 = lax.broadcasted_iota(jnp.int32, (B, LANES), 1) < HEAD_DIM
    qi = lax.broadcasted_iota(jnp.int32, (2 * B, B), 0) % B
    ki = lax.broadcasted_iota(jnp.int32, (2 * B, B), 1)
    tri_cur = ki <= qi
    tri_prev = ki >= qi

    stats = {1: (m1, l1, a1), 4: (m4, l4, a4), 16: (m16, l16, a16)}

    for d in DILATIONS:
        n = L // d
        nb = n // B
        m_s, l_s, a_s = stats[d]
        for r in range(d):
            src = pl.ds(r, n, stride=d) if d > 1 else slice(None)
            dst = slice(r * n, (r + 1) * n)
            hm = lax.broadcasted_iota(jnp.int32, (n, LANES), 1) < HEAD_DIM
            q = q_ref[src, :]
            q0_s[dst, :] = jnp.where(hm, q, 0.0).astype(BF16)
            q1_s[dst, :] = jnp.where(hm, 0.0, q).astype(BF16)
            k_s[dst, :] = k_ref[src, :].astype(BF16)
            v_s[dst, :] = v_ref[src, :].astype(BF16)

        def block(j, carry, nb=nb, m_s=m_s, l_s=l_s, a_s=a_s):
            cur = pl.ds(pl.multiple_of(j * B, B), B)
            qb = jnp.concatenate([q0_s[cur, :], q1_s[cur, :]], axis=0)
            s_c = jnp.where(tri_cur, _nt_dot(qb, k_s[cur, :]), NEG)
            if nb > 1:
                prev = pl.ds(pl.multiple_of(jnp.maximum(j - 1, 0) * B, B), B)
                mask_prev = jnp.logical_and(tri_prev, (j % nb) > 0)
                s_p = jnp.where(mask_prev, _nt_dot(qb, k_s[prev, :]), NEG)
                m = jnp.max(jnp.maximum(s_c, s_p), axis=-1, keepdims=True)
                p_c, p_p = jnp.exp(s_c - m), jnp.exp(s_p - m)
                den = jnp.sum(p_c + p_p, axis=-1, keepdims=True)
                pv = (jnp.dot(p_c.astype(BF16), v_s[cur, :], preferred_element_type=F32)
                      + jnp.dot(p_p.astype(BF16), v_s[prev, :], preferred_element_type=F32))
            else:
                m = jnp.max(s_c, axis=-1, keepdims=True)
                p_c = jnp.exp(s_c - m)
                den = jnp.sum(p_c, axis=-1, keepdims=True)
                pv = jnp.dot(p_c.astype(BF16), v_s[cur, :], preferred_element_type=F32)
            m_s[cur, :] = jnp.where(head0, m[:B], m[B:])
            l_s[cur, :] = jnp.where(head0, den[:B], den[B:])
            a_s[cur, :] = jnp.where(head0, pv[:B], pv[B:])
            return carry

        lax.fori_loop(0, L // B, block, 0, unroll=16)

    dmax = DILATIONS[-1]
    nrow = L // dmax
    for r in range(dmax):
        o1 = pl.ds(r, nrow, stride=dmax)
        o4 = pl.ds((r % 4) * (L // 4) + r // 4, nrow, stride=dmax // 4)
        o16 = slice(r * nrow, (r + 1) * nrow)
        mm1, mm4, mm16 = m1[o1, :], m4[o4, :], m16[o16, :]
        mx = jnp.maximum(jnp.maximum(mm1, mm4), mm16)
        w1, w4, w16 = jnp.exp(mm1 - mx), jnp.exp(mm4 - mx), jnp.exp(mm16 - mx)
        num = w1 * a1[o1, :] + w4 * a4[o4, :] + w16 * a16[o16, :]
        den = w1 * l1[o1, :] + w4 * l4[o4, :] + w16 * l16[o16, :]
        o_ref[o1, :] = num / den


def attn_prompt(qn, kn, proj, n_batch, L, c_b, v_col0):
    n_hp = c_b // LANES
    vblk = v_col0 // LANES
    stat = [pltpu.VMEM((L, LANES), F32)] * 9
    return pl.pallas_call(
        _attn_prompt_kernel,
        grid=(n_batch, n_hp),
        in_specs=[
            pl.BlockSpec((L, LANES), lambda b, h: (b, h)),
            pl.BlockSpec((L, LANES), lambda b, h: (b, h)),
            pl.BlockSpec((L, LANES), lambda b, h: (b, vblk + h)),
        ],
        out_specs=pl.BlockSpec((L, LANES), lambda b, h: (b, h)),
        out_shape=jax.ShapeDtypeStruct((n_batch * L, c_b), F32),
        scratch_shapes=[pltpu.VMEM((L, LANES), BF16)] * 4 + stat,
        compiler_params=_cparams(("parallel", "parallel")),
        name="attn_prompt",
    )(qn, kn, proj)


def _kv_window_kernel(*refs, depth):
    k_refs, v_refs = refs[:depth], refs[depth:2 * depth]
    kt_ref, vt_ref = refs[2 * depth:]
    layer = pl.program_id(0)
    for l in range(depth):
        @pl.when(layer == l)
        def _(l=l):
            kt_ref[...] = k_refs[l][...].T
            vt_ref[...] = v_refs[l][...].T


def kv_window(kns, projs, n_batch, L, c_b, v_col0):
    depth = len(kns)
    n_hp = c_b // LANES
    vblk = v_col0 // LANES

    def spec(l, col0):
        def imap(layer, b, h):
            return jnp.where(layer == l, b, 0), col0 + jnp.where(layer == l, h, 0)
        return pl.BlockSpec((L, LANES), imap)

    out_spec = pl.BlockSpec((None, None, LANES, L), lambda layer, b, h: (layer, b, h, 0))
    out = jax.ShapeDtypeStruct((depth, n_batch, c_b, L), F32)
    return pl.pallas_call(
        functools.partial(_kv_window_kernel, depth=depth),
        grid=(depth, n_batch, n_hp),
        in_specs=[spec(l, 0) for l in range(depth)] + [spec(l, vblk) for l in range(depth)],
        out_specs=[out_spec, out_spec],
        out_shape=[out, out],
        compiler_params=_cparams(("arbitrary", "arbitrary", "arbitrary")),
        name="kv_window",
    )(*kns, *projs)


def _branch_multiplicity(dist):
    c = np.zeros(dist.shape, np.float32)
    for d in DILATIONS:
        c += ((dist >= 0) & (dist % d == 0) & (dist <= d * DIL_STEPS)).astype(np.float32)
    return c


NEW_ROWS_PAD = 16


def _sample_weights(n_new, past_len, pos, n_heads):
    s = np.repeat(np.arange(n_new), n_heads)[:, None]
    return _branch_multiplicity(past_len + s - np.asarray(pos)[None, :]).astype(np.float32)


def _attn_sample_kernel(q_ref, kt_ref, vt_ref, kn_ref, vn_ref, w_ref, wn_ref, o_ref):
    n_heads, hd, window = kt_ref.shape
    width = n_heads * hd
    q = q_ref[...]
    nq = q.shape[0]
    q2 = jnp.concatenate([q, q], axis=-1)
    qt = jnp.concatenate([q2] * (width // (2 * hd)), axis=-1)
    own_head = (lax.broadcasted_iota(jnp.int32, (nq, width), 0) % n_heads
                == lax.broadcasted_iota(jnp.int32, (nq, width), 1) // hd)
    qbd = jnp.where(own_head, qt, 0.0).astype(BF16)

    w, wn = w_ref[...], wn_ref[...]
    s_c = jnp.dot(qbd, kt_ref[...].reshape(width, window).astype(BF16),
                  preferred_element_type=F32)
    s_n = _nt_dot(qbd, kn_ref[...].astype(BF16))
    s_c = jnp.where(w > 0.0, s_c, NEG)
    s_n = jnp.where(wn > 0.0, s_n, NEG)
    m = jnp.maximum(jnp.max(s_c, axis=-1, keepdims=True), jnp.max(s_n, axis=-1, keepdims=True))
    p_c = w * jnp.exp(s_c - m)
    p_n = wn * jnp.exp(s_n - m)
    den = jnp.sum(p_c, axis=-1, keepdims=True) + jnp.sum(p_n, axis=-1, keepdims=True)
    full = (_nt_dot(p_c.astype(BF16), vt_ref[...].reshape(width, window).astype(BF16))
            + jnp.dot(p_n.astype(BF16), vn_ref[...].astype(BF16), preferred_element_type=F32))
    full = jnp.where(own_head, full, 0.0)
    acc = full[:, 0:LANES]
    for t in range(1, width // LANES):
        acc = acc + full[:, t * LANES:(t + 1) * LANES]
    o_ref[...] = (acc[:, :hd] + acc[:, hd:]) / den


def attn_sample(q_b, k_new, v_new, cache_k, cache_v, layer, w, wn):
    n_b, nq, _ = q_b.shape
    depth, _, window, n_heads, _ = cache_k.shape
    width = n_heads * HEAD_DIM
    kt = cache_k.transpose(0, 1, 3, 4, 2)
    vt = cache_v.transpose(0, 1, 3, 4, 2)
    cache_spec = pl.BlockSpec((None, None, n_heads, HEAD_DIM, window), lambda b: (layer, b, 0, 0, 0))
    new_spec = pl.BlockSpec((None, NEW_ROWS_PAD, width), lambda b: (b, 0, 0))
    q_spec = pl.BlockSpec((None, nq, HEAD_DIM), lambda b: (b, 0, 0))

    def const_spec(a):
        return pl.BlockSpec(a.shape, lambda b: (0, 0))

    return pl.pallas_call(
        _attn_sample_kernel,
        grid=(n_b,),
        in_specs=[q_spec, cache_spec, cache_spec, new_spec, new_spec, const_spec(w), const_spec(wn)],
        out_specs=q_spec,
        out_shape=jax.ShapeDtypeStruct((n_b, nq, HEAD_DIM), F32),
        compiler_params=_cparams(("parallel",)),
        name="attn_sample",
    )(q_b, kt, vt, k_new, v_new, w, wn)


def _layernorm_silu(y, g, b):
    mu = jnp.mean(y, axis=-1, keepdims=True)
    yc = y - mu
    var = jnp.mean(yc * yc, axis=-1, keepdims=True)
    return _silu(yc * lax.rsqrt(var + EPS) * g + b)


CONV_HALO = 32
CONV_ROWS = 32


def _conv_prompt_kernel(val_ref, gate_ref, wdw_ref, bdw_ref, lng_ref, lnb_ref, wpw_ref,
                        y_ref, st_ref, xp_s, y_s):
    c = pl.program_id(1)
    lc, ca = val_ref.shape
    off = CONV_HALO - CONV_PAST

    @pl.when(c == 0)
    def _():
        xp_s[0:CONV_HALO, :] = jnp.zeros((CONV_HALO, ca), F32)

    @pl.when(c > 0)
    def _():
        xp_s[0:CONV_HALO, :] = xp_s[lc:lc + CONV_HALO, :]

    xp_s[CONV_HALO:CONV_HALO + lc, :] = val_ref[...] * _sigmoid(gate_ref[...])

    for rb in range(lc // CONV_ROWS):
        for lb in range(ca // LANES):
            ls = slice(lb * LANES, (lb + 1) * LANES)
            acc = jnp.zeros((CONV_ROWS, LANES), F32)
            for j in range(CONV_WIDTH):
                r0 = rb * CONV_ROWS + off + j
                acc = acc + wdw_ref[j:j + 1, ls] * xp_s[r0:r0 + CONV_ROWS, ls]
            y_s[rb * CONV_ROWS:(rb + 1) * CONV_ROWS, ls] = acc

    z = _layernorm_silu(y_s[...] + bdw_ref[...], lng_ref[...], lnb_ref[...])
    y_ref[...] = jnp.dot(z.astype(BF16), wpw_ref[...].astype(BF16), preferred_element_type=F32)

    @pl.when(c == pl.num_programs(1) - 1)
    def _():
        st_ref[...] = xp_s[lc + off:lc + CONV_HALO, :]


def _layer_vec(a, layer):
    return a[layer].reshape(1, -1)


def conv_prompt(proj, n_batch, L, c_a, p, layer, lc=256):
    nch = L // lc
    vec = pl.BlockSpec((1, c_a), lambda b, c: (0, 0))
    return pl.pallas_call(
        _conv_prompt_kernel,
        grid=(n_batch, nch),
        in_specs=[
            pl.BlockSpec((lc, c_a), lambda b, c: (b * nch + c, 0)),
            pl.BlockSpec((lc, c_a), lambda b, c: (b * nch + c, 1)),
            pl.BlockSpec((None, CONV_WIDTH, c_a), lambda b, c: (layer, 0, 0)),
            vec, vec, vec,
            pl.BlockSpec((None, c_a, c_a), lambda b, c: (layer, 0, 0)),
        ],
        out_specs=[pl.BlockSpec((lc, c_a), lambda b, c: (b * nch + c, 0)),
                   pl.BlockSpec((None, CONV_PAST, c_a), lambda b, c: (b, 0, 0))],
        out_shape=[jax.ShapeDtypeStruct((n_batch * L, c_a), F32),
                   jax.ShapeDtypeStruct((n_batch, CONV_PAST, c_a), F32)],
        scratch_shapes=[pltpu.VMEM((lc + CONV_HALO, c_a), F32), pltpu.VMEM((lc, c_a), F32)],
        compiler_params=_cparams(("parallel", "arbitrary")),
        name="conv_prompt",
    )(proj, proj, p['a_w_dw'], _layer_vec(p['a_b_dw'], layer), _layer_vec(p['a_ln_g'], layer),
      _layer_vec(p['a_ln_b'], layer), p['a_w_pw'])


def _conv_sample_kernel(val_ref, gate_ref, past_ref, wdw_ref, bdw_ref, lng_ref, lnb_ref, wpw_ref,
                        y_ref, st_ref, y_s):
    n_b = past_ref.shape[1]
    n_new = val_ref.shape[0] // n_b
    g = val_ref[...] * _sigmoid(gate_ref[...])

    def xp(t):
        if t < CONV_PAST:
            return past_ref[t]
        return g[(t - CONV_PAST) * n_b:(t - CONV_PAST + 1) * n_b, :]

    for s in range(n_new):
        acc = jnp.zeros_like(xp(0))
        for j in range(CONV_WIDTH):
            acc = acc + wdw_ref[j:j + 1, :] * xp(s + j)
        y_s[s * n_b:(s + 1) * n_b, :] = acc
    z = _layernorm_silu(y_s[...] + bdw_ref[...], lng_ref[...], lnb_ref[...])
    y_ref[...] = jnp.dot(z.astype(BF16), wpw_ref[...].astype(BF16), preferred_element_type=F32)
    for t in range(CONV_PAST):
        st_ref[t] = xp(t + n_new)


def conv_sample(proj, row0, n_new, n_b, c_a, past_tm, p, layer):
    ts = n_new * n_b
    vec = pl.BlockSpec((1, c_a), lambda i: (0, 0))
    return pl.pallas_call(
        _conv_sample_kernel,
        grid=(1,),
        in_specs=[
            pl.BlockSpec((ts, c_a), lambda i: (row0 // ts, 0)),
            pl.BlockSpec((ts, c_a), lambda i: (row0 // ts, 1)),
            pl.BlockSpec((None, CONV_PAST, n_b, c_a), lambda i: (layer, 0, 0, 0)),
            pl.BlockSpec((None, CONV_WIDTH, c_a), lambda i: (layer, 0, 0)),
            vec, vec, vec,
            pl.BlockSpec((None, c_a, c_a), lambda i: (layer, 0, 0)),
        ],
        out_specs=[pl.BlockSpec((ts, c_a), lambda i: (0, 0)),
                   pl.BlockSpec((CONV_PAST, n_b, c_a), lambda i: (0, 0, 0))],
        out_shape=[jax.ShapeDtypeStruct((ts, c_a), F32),
                   jax.ShapeDtypeStruct((CONV_PAST, n_b, c_a), F32)],
        scratch_shapes=[pltpu.VMEM((ts, c_a), F32)],
        compiler_params=_cparams(("arbitrary",)),
        name="conv_sample",
    )(proj, proj, past_tm, p['a_w_dw'], _layer_vec(p['a_b_dw'], layer),
      _layer_vec(p['a_ln_g'], layer), _layer_vec(p['a_ln_b'], layer), p['a_w_pw'])


def _s5_param_kernel(are_ref, aim_ref, ldt_ref, bre_ref, bim_ref,
                     pre_ref, pim_ref, bbre_ref, bbim_ref):
    a_re, a_im = are_ref[...], aim_ref[...]
    dt = jnp.exp(ldt_ref[...])
    mag = jnp.exp(a_re * dt)
    ab_re, ab_im = mag * jnp.cos(a_im * dt), mag * jnp.sin(a_im * dt)
    nr, ni = ab_re - 1.0, ab_im
    inv = 1.0 / (a_re * a_re + a_im * a_im)
    f_re = (nr * a_re + ni * a_im) * inv
    f_im = (ni * a_re - nr * a_im) * inv
    b_re, b_im = bre_ref[...], bim_ref[...]
    bbre_ref[...] = f_re * b_re - f_im * b_im
    bbim_ref[...] = f_re * b_im + f_im * b_re
    pr, pi = ab_re, ab_im
    pre_ref[0] = pr
    pim_ref[0] = pi
    for k in range(1, SUBLANES):
        pr, pi = pr * ab_re - pi * ab_im, pr * ab_im + pi * ab_re
        pre_ref[k] = pr
        pim_ref[k] = pi


def s5_params(p, layer):
    a_re, a_im = p['c_a_re'][layer], p['c_a_im'][layer]
    G, N = a_re.shape
    C = SSM_GROUP
    b_re_t = p['c_b_re'][layer].transpose(0, 2, 1)
    b_im_t = p['c_b_im'][layer].transpose(0, 2, 1)
    pre, pim, bbre, bbim = pl.pallas_call(
        _s5_param_kernel,
        out_shape=[jax.ShapeDtypeStruct((SUBLANES, G, 1, N), F32)] * 2
        + [jax.ShapeDtypeStruct((G, C, N), F32)] * 2,
        name="s5_params",
    )(a_re.reshape(G, 1, N), a_im.reshape(G, 1, N), p['c_log_dt'][layer].reshape(G, 1, 1),
      b_re_t, b_im_t)
    S = G * N
    pre, pim = pre.reshape(SUBLANES, S), pim.reshape(SUBLANES, S)
    t = np.arange(SUBLANES)[:, None]
    tabs = []
    for k in (1, 2, 4):
        keep = jnp.asarray(t >= k)
        tabs += [jnp.where(keep, pre[k - 1][None, :], 0.0), jnp.where(keep, pim[k - 1][None, :], 0.0)]
    tabs += [pre, pim]
    tab = jnp.stack(tabs)
    eye = jnp.eye(G, dtype=F32)

    def in_proj(bb):
        return (eye[:, None, :, None] * bb[:, :, None, :]).reshape(G * C, S).astype(BF16)

    def out_proj(c):
        return (eye[:, None, :, None] * c.transpose(0, 2, 1)[:, :, None, :]).reshape(S, G * C)

    c_cat = jnp.concatenate([out_proj(p['c_c_re'][layer]), -out_proj(p['c_c_im'][layer])]).astype(BF16)
    return dict(tab=tab, bb_re=in_proj(bbre), bb_im=in_proj(bbim), c_cat=c_cat,
                ab_re=pre[0:1], ab_im=pim[0:1])


def _s5_readout(u, xr, xi, ccat_ref, d_ref, wglu_ref, bglu_ref):
    S = xr.shape[-1]
    y = (jnp.dot(xr.astype(BF16), ccat_ref[0:S, :], preferred_element_type=F32)
         + jnp.dot(xi.astype(BF16), ccat_ref[S:2 * S, :], preferred_element_type=F32)
         + d_ref[...] * u)
    z = _gelu_tanh(y)
    gate = jnp.dot(z.astype(BF16), wglu_ref[...].astype(BF16), preferred_element_type=F32)
    return z * _sigmoid(gate + bglu_ref[...])


S5_LANE_GROUP = 512


def _s5_prompt_kernel(u_ref, bbre_ref, bbim_ref, tab_ref, ccat_ref, d_ref, wglu_ref, bglu_ref,
                      y_ref, hre_ref, him_ref, xr_s, xi_s, h_s):
    c = pl.program_id(1)
    tc = u_ref.shape[0]
    S = xr_s.shape[1]

    @pl.when(c == 0)
    def _():
        h_s[...] = jnp.zeros_like(h_s)

    u = u_ref[...]
    ub = u.astype(BF16)
    xr_s[...] = jnp.dot(ub, bbre_ref[...], preferred_element_type=F32)
    xi_s[...] = jnp.dot(ub, bbim_ref[...], preferred_element_type=F32)

    for jg in range(S // S5_LANE_GROUP):
        ls = slice(jg * S5_LANE_GROUP, (jg + 1) * S5_LANE_GROUP)
        levels = [(k, tab_ref[2 * i, :, ls], tab_ref[2 * i + 1, :, ls])
                  for i, k in enumerate((1, 2, 4))]
        cr, ci = tab_ref[6, :, ls], tab_ref[7, :, ls]

        def tile(i, carry, ls=ls, levels=levels, cr=cr, ci=ci):
            hr, hi = carry
            rows = pl.ds(pl.multiple_of(i * SUBLANES, SUBLANES), SUBLANES)
            br, bi = xr_s[rows, ls], xi_s[rows, ls]
            for k, pr, pi in levels:
                sr, si = pltpu.roll(br, k, axis=0), pltpu.roll(bi, k, axis=0)
                br, bi = br + pr * sr - pi * si, bi + pr * si + pi * sr
            br, bi = br + cr * hr - ci * hi, bi + cr * hi + ci * hr
            xr_s[rows, ls] = br
            xi_s[rows, ls] = bi
            last = slice(SUBLANES - 1, SUBLANES)
            return (jnp.broadcast_to(br[last, :], br.shape), jnp.broadcast_to(bi[last, :], bi.shape))

        hr, hi = lax.fori_loop(0, tc // SUBLANES, tile, (h_s[0, :, ls], h_s[1, :, ls]))
        h_s[0, :, ls] = hr
        h_s[1, :, ls] = hi

    y_ref[...] = _s5_readout(u, xr_s[...], xi_s[...], ccat_ref, d_ref, wglu_ref, bglu_ref)
    hre_ref[...] = h_s[0]
    him_ref[...] = h_s[1]


def s5_prompt(proj, n_batch, L, c_c, u_col0, sp, p, layer, tc=256):
    nch = L // tc
    S = sp['tab'].shape[-1]
    ublk = u_col0 // c_c

    def full(a):
        return pl.BlockSpec(a.shape, lambda b, c: (0,) * a.ndim)

    d = p['c_d'][layer].reshape(1, c_c)
    bglu = p['c_b_glu'][layer].reshape(1, c_c)
    return pl.pallas_call(
        _s5_prompt_kernel,
        grid=(n_batch, nch),
        in_specs=[
            pl.BlockSpec((tc, c_c), lambda b, c: (b * nch + c, ublk)),
            full(sp['bb_re']), full(sp['bb_im']), full(sp['tab']), full(sp['c_cat']), full(d),
            pl.BlockSpec((None, c_c, c_c), lambda b, c: (layer, 0, 0)),
            full(bglu),
        ],
        out_specs=[pl.BlockSpec((tc, c_c), lambda b, c: (b * nch + c, 0)),
                   pl.BlockSpec((None, SUBLANES, S), lambda b, c: (b, 0, 0)),
                   pl.BlockSpec((None, SUBLANES, S), lambda b, c: (b, 0, 0))],
        out_shape=[jax.ShapeDtypeStruct((n_batch * L, c_c), F32),
                   jax.ShapeDtypeStruct((n_batch, SUBLANES, S), F32),
                   jax.ShapeDtypeStruct((n_batch, SUBLANES, S), F32)],
        scratch_shapes=[pltpu.VMEM((tc, S), F32), pltpu.VMEM((tc, S), F32),
                        pltpu.VMEM((2, SUBLANES, S), F32)],
        compiler_params=_cparams(("parallel", "arbitrary")),
        name="s5_prompt",
    )(proj, sp['bb_re'], sp['bb_im'], sp['tab'], sp['c_cat'], d, p['c_w_glu'], bglu)


def _s5_sample_kernel(u_ref, h0re_ref, h0im_ref, bbre_ref, bbim_ref, abre_ref, abim_ref,
                      ccat_ref, d_ref, wglu_ref, bglu_ref,
                      y_ref, hre_ref, him_ref, xr_s, xi_s):
    n_b = h0re_ref.shape[0]
    n_new = u_ref.shape[0] // n_b
    u = u_ref[...]
    ub = u.astype(BF16)
    xr_s[...] = jnp.dot(ub, bbre_ref[...], preferred_element_type=F32)
    xi_s[...] = jnp.dot(ub, bbim_ref[...], preferred_element_type=F32)
    ar, ai = abre_ref[...], abim_ref[...]
    hr, hi = h0re_ref[...], h0im_ref[...]
    for s in range(n_new):
        rows = slice(s * n_b, (s + 1) * n_b)
        hr, hi = ar * hr - ai * hi + xr_s[rows, :], ar * hi + ai * hr + xi_s[rows, :]
        xr_s[rows, :] = hr
        xi_s[rows, :] = hi
    y_ref[...] = _s5_readout(u, xr_s[...], xi_s[...], ccat_ref, d_ref, wglu_ref, bglu_ref)
    hre_ref[...] = hr
    him_ref[...] = hi


def s5_sample(proj, row0, n_new, n_b, c_c, u_col0, h0_re, h0_im, sp, p, layer):
    ts = n_new * n_b
    S = sp['tab'].shape[-1]

    def full(a):
        return pl.BlockSpec(a.shape, lambda i: (0,) * a.ndim)

    d = p['c_d'][layer].reshape(1, c_c)
    bglu = p['c_b_glu'][layer].reshape(1, c_c)
    return pl.pallas_call(
        _s5_sample_kernel,
        grid=(1,),
        in_specs=[
            pl.BlockSpec((ts, c_c), lambda i: (row0 // ts, u_col0 // c_c)),
            full(h0_re), full(h0_im), full(sp['bb_re']), full(sp['bb_im']),
            full(sp['ab_re']), full(sp['ab_im']), full(sp['c_cat']), full(d),
            pl.BlockSpec((None, c_c, c_c), lambda i: (layer, 0, 0)),
            full(bglu),
        ],
        out_specs=[pl.BlockSpec((ts, c_c), lambda i: (0, 0)),
                   pl.BlockSpec((n_b, S), lambda i: (0, 0)),
                   pl.BlockSpec((n_b, S), lambda i: (0, 0))],
        out_shape=[jax.ShapeDtypeStruct((ts, c_c), F32),
                   jax.ShapeDtypeStruct((n_b, S), F32),
                   jax.ShapeDtypeStruct((n_b, S), F32)],
        scratch_shapes=[pltpu.VMEM((ts, S), F32), pltpu.VMEM((ts, S), F32)],
        compiler_params=_cparams(("arbitrary",)),
        name="s5_sample",
    )(proj, h0_re, h0_im, sp['bb_re'], sp['bb_im'], sp['ab_re'], sp['ab_im'], sp['c_cat'], d,
      p['c_w_glu'], bglu)


def _out_proj_kernel(yap_ref, ybp_ref, ycp_ref, yas_ref, ybs_ref, ycs_ref, w_ref, x_ref,
                     o_ref, cat_s, w_s, *, n_prompt_tiles):
    i = pl.program_id(1)

    @pl.when(i == 0)
    def _():
        w_s[...] = w_ref[...].astype(BF16)

    def fill(ya_ref, yb_ref, yc_ref):
        ca, cb = ya_ref.shape[1], yb_ref.shape[1]
        cat_s[:, 0:ca] = ya_ref[...].astype(BF16)
        cat_s[:, ca:ca + cb] = yb_ref[...].astype(BF16)
        cat_s[:, ca + cb:] = yc_ref[...].astype(BF16)

    @pl.when(i < n_prompt_tiles)
    def _():
        fill(yap_ref, ybp_ref, ycp_ref)

    @pl.when(i >= n_prompt_tiles)
    def _():
        fill(yas_ref, ybs_ref, ycs_ref)

    o_ref[...] = x_ref[...] + jnp.dot(cat_s[...], w_s[...], preferred_element_type=F32)


def out_proj(prompt_parts, sample_parts, w_stack, layer, x, tm, tn):
    T, D = x.shape
    dm = w_stack.shape[1]
    n_p = prompt_parts[0].shape[0] // tm
    assert prompt_parts[0].shape[0] % tm == 0 and sample_parts[0].shape[0] % tm == 0

    def prompt_spec(a):
        return pl.BlockSpec((tm, a.shape[1]), lambda j, i: (jnp.minimum(i, n_p - 1), 0))

    def sample_spec(a):
        return pl.BlockSpec((tm, a.shape[1]), lambda j, i: (jnp.maximum(i - n_p, 0), 0))

    return pl.pallas_call(
        functools.partial(_out_proj_kernel, n_prompt_tiles=n_p),
        grid=(D // tn, T // tm),
        in_specs=[prompt_spec(a) for a in prompt_parts] + [sample_spec(a) for a in sample_parts]
        + [pl.BlockSpec((None, dm, tn), lambda j, i: (layer, 0, j)),
           pl.BlockSpec((tm, tn), lambda j, i: (i, j))],
        out_specs=pl.BlockSpec((tm, tn), lambda j, i: (i, j)),
        out_shape=jax.ShapeDtypeStruct((T, D), F32),
        scratch_shapes=[pltpu.VMEM((tm, dm), BF16), pltpu.VMEM((dm, tn), BF16)],
        compiler_params=_cparams(("arbitrary", "arbitrary")),
        name="out_proj",
    )(*prompt_parts, *sample_parts, w_stack, x)


def _norm_cast_kernel(x_ref, g_ref, o_ref):
    x = x_ref[...]
    ms = jnp.mean(x * x, axis=-1, keepdims=True)
    o_ref[...] = (x * lax.rsqrt(ms + EPS) * g_ref[...]).astype(BF16)


def norm_cast(x, g, tm):
    T, D = x.shape
    return pl.pallas_call(
        _norm_cast_kernel,
        grid=(T // tm,),
        in_specs=[pl.BlockSpec((tm, D), lambda i: (i, 0)), pl.BlockSpec((1, D), lambda i: (0, 0))],
        out_specs=pl.BlockSpec((tm, D), lambda i: (i, 0)),
        out_shape=jax.ShapeDtypeStruct((T, D), BF16),
        compiler_params=_cparams(("parallel",)),
        name="norm_cast",
    )(x, g.reshape(1, D))


def _new_expert(te_ref, i):
    prev = te_ref[jnp.maximum(i - 1, 0)]
    return jnp.logical_or(i == 0, te_ref[i] != prev)


def _ffn_up_kernel(te_ref, nv_ref, x_ref, w1_ref, w3_ref, h_ref, w1_s, w3_s):
    i = pl.program_id(1)

    @pl.when(_new_expert(te_ref, i))
    def _():
        w1_s[...] = w1_ref[...].astype(BF16)
        w3_s[...] = w3_ref[...].astype(BF16)

    @pl.when(i < nv_ref[0])
    def _():
        x = x_ref[...]
        a = jnp.dot(x, w1_s[...], preferred_element_type=F32)
        b = jnp.dot(x, w3_s[...], preferred_element_type=F32)
        h_ref[...] = (_silu(a) * b).astype(BF16)

    @pl.when(i >= nv_ref[0])
    def _():
        h_ref[...] = jnp.zeros_like(h_ref)


def ffn_up(xs, w1, w3, tile_expert, n_valid, tm, fc):
    R, D = xs.shape
    F = w1.shape[-1]
    wspec = pl.BlockSpec((None, D, fc), lambda j, i, te, nv: (te[i], 0, j))
    return pl.pallas_call(
        _ffn_up_kernel,
        grid_spec=pltpu.PrefetchScalarGridSpec(
            num_scalar_prefetch=2,
            grid=(pl.cdiv(F, fc), R // tm),
            in_specs=[pl.BlockSpec((tm, D), lambda j, i, te, nv: (i, 0)), wspec, wspec],
            out_specs=pl.BlockSpec((tm, fc), lambda j, i, te, nv: (i, j)),
            scratch_shapes=[pltpu.VMEM((D, fc), BF16)] * 2,
        ),
        out_shape=jax.ShapeDtypeStruct((R, F), BF16),
        compiler_params=_cparams(("arbitrary", "arbitrary")),
        name="ffn_up",
    )(tile_expert, n_valid, xs, w1, w3)


def _ffn_down_kernel(te_ref, nv_ref, h_ref, w2_ref, *rest, residual):
    if residual:
        r_ref, o_ref, w2_s = rest
    else:
        o_ref, w2_s = rest
    i = pl.program_id(1)

    @pl.when(_new_expert(te_ref, i))
    def _():
        w2_s[...] = w2_ref[...].astype(BF16)

    @pl.when(i < nv_ref[0])
    def _():
        y = jnp.dot(h_ref[...], w2_s[...], preferred_element_type=F32)
        o_ref[...] = r_ref[...] + y if residual else y

    @pl.when(i >= nv_ref[0])
    def _():
        o_ref[...] = jnp.zeros_like(o_ref)


def ffn_down(h, w2, tile_expert, n_valid, tm, tn, residual=None):
    R, F = h.shape
    D = w2.shape[-1]
    tile = pl.BlockSpec((tm, tn), lambda j, i, te, nv: (i, j))
    in_specs = [pl.BlockSpec((tm, F), lambda j, i, te, nv: (i, 0)),
                pl.BlockSpec((None, F, tn), lambda j, i, te, nv: (te[i], 0, j))]
    args = [h, w2]
    if residual is not None:
        in_specs.append(tile)
        args.append(residual)
    return pl.pallas_call(
        functools.partial(_ffn_down_kernel, residual=residual is not None),
        grid_spec=pltpu.PrefetchScalarGridSpec(
            num_scalar_prefetch=2,
            grid=(D // tn, R // tm),
            in_specs=in_specs,
            out_specs=tile,
            scratch_shapes=[pltpu.VMEM((F, tn), BF16)],
        ),
        out_shape=jax.ShapeDtypeStruct((R, D), F32),
        compiler_params=_cparams(("arbitrary", "arbitrary")),
        name="ffn_down",
    )(tile_expert, n_valid, *args)


def _router_kernel(x_ref, g_ref, wr_ref, br_ref, idx_ref, gate_ref):
    x = x_ref[...]
    ms = jnp.mean(x * x, axis=-1, keepdims=True)
    u = x * lax.rsqrt(ms + EPS) * g_ref[...]
    logits = lax.dot_general(wr_ref[...], u, (((1,), (1,)), ((), ())),
                             precision=lax.Precision.HIGHEST,
                             preferred_element_type=F32) + br_ref[...]
    n_e = logits.shape[0]
    eid = lax.broadcasted_iota(jnp.int32, logits.shape, 0)
    m1 = jnp.max(logits, axis=0, keepdims=True)
    i1 = jnp.min(jnp.where(logits == m1, eid, n_e), axis=0, keepdims=True)
    rest = jnp.where(eid == i1, -jnp.inf, logits)
    m2 = jnp.max(rest, axis=0, keepdims=True)
    i2 = jnp.min(jnp.where(rest == m2, eid, n_e), axis=0, keepdims=True)
    e2 = jnp.exp(m2 - m1)
    g1 = 1.0 / (1.0 + e2)
    idx_ref[...] = jnp.where(eid == 0, i1, i2)
    gate_ref[...] = jnp.where(eid == 0, g1, e2 * g1)


def router(x, g, w_router, b_router, tm):
    T, D = x.shape
    E = w_router.shape[-1]
    return pl.pallas_call(
        _router_kernel,
        grid=(T // tm,),
        in_specs=[pl.BlockSpec((tm, D), lambda i: (i, 0)),
                  pl.BlockSpec((1, D), lambda i: (0, 0)),
                  pl.BlockSpec((E, D), lambda i: (0, 0)),
                  pl.BlockSpec((E, 1), lambda i: (0, 0))],
        out_specs=[pl.BlockSpec((E, tm), lambda i: (0, i)),
                   pl.BlockSpec((E, tm), lambda i: (0, i))],
        out_shape=[jax.ShapeDtypeStruct((E, T), jnp.int32), jax.ShapeDtypeStruct((E, T), F32)],
        compiler_params=_cparams(("parallel",)),
        name="moe_router",
    )(x, g.reshape(1, D), w_router.T, b_router.reshape(E, 1))


def _row_copy(src_hbm, row, dst, r, sem):
    return pltpu.make_async_copy(src_hbm.at[pl.ds(row, 1), :], dst.at[pl.ds(r, 1), :], sem)


def _wait_rows(src_hbm, dst, sem):
    pltpu.make_async_copy(src_hbm.at[pl.ds(0, dst.shape[0]), :], dst, sem).wait()


GATHER_UNROLL = 8


def _rows_loop(tm, fn):
    def body(it, c):
        for u in range(GATHER_UNROLL):
            fn(it * GATHER_UNROLL + u)
        return c

    lax.fori_loop(0, tm // GATHER_UNROLL, body, 0)


def _gather_norm_kernel(src_ref, nv_ref, x_hbm, g_ref, o_ref, buf, sem):
    tm = buf.shape[1]
    i = pl.program_id(0)
    n_used = nv_ref[0]

    def fetch(tile, slot):
        _rows_loop(tm, lambda r: _row_copy(x_hbm, src_ref[tile * tm + r], buf.at[slot], r,
                                           sem.at[slot]).start())

    @pl.when(i == 0)
    def _():
        fetch(0, 0)

    @pl.when(i + 1 < n_used)
    def _():
        fetch(i + 1, (i + 1) % 2)

    @pl.when(i < n_used)
    def _():
        slot = i % 2
        _wait_rows(x_hbm, buf.at[slot], sem.at[slot])
        x = buf[slot]
        ms = jnp.mean(x * x, axis=-1, keepdims=True)
        o_ref[...] = (x * lax.rsqrt(ms + EPS) * g_ref[...]).astype(BF16)

    @pl.when(i >= n_used)
    def _():
        o_ref[...] = jnp.zeros_like(o_ref)


def gather_norm(x, g, src, n_used, tm):
    T, D = x.shape
    R = src.shape[0]
    return pl.pallas_call(
        _gather_norm_kernel,
        grid_spec=pltpu.PrefetchScalarGridSpec(
            num_scalar_prefetch=2,
            grid=(R // tm,),
            in_specs=[pl.BlockSpec(memory_space=pl.ANY),
                      pl.BlockSpec((1, D), lambda i, s, n: (0, 0))],
            out_specs=pl.BlockSpec((tm, D), lambda i, s, n: (i, 0)),
            scratch_shapes=[pltpu.VMEM((2, tm, D), F32), pltpu.SemaphoreType.DMA((2,))],
        ),
        out_shape=jax.ShapeDtypeStruct((R, D), BF16),
        compiler_params=_cparams(("arbitrary",)),
        name="moe_gather",
    )(src, n_used, x, g.reshape(1, D))


def _combine_kernel(p0_ref, p1_ref, h_ref, gate_ref, y_hbm, o_ref, buf, sem):
    tm = buf.shape[2]
    i = pl.program_id(0)

    def fetch(tile, slot):
        def one(r):
            _row_copy(y_hbm, p0_ref[tile * tm + r], buf.at[slot, 0], r, sem.at[slot, 0]).start()
            _row_copy(y_hbm, p1_ref[tile * tm + r], buf.at[slot, 1], r, sem.at[slot, 1]).start()

        _rows_loop(tm, one)

    @pl.when(i == 0)
    def _():
        fetch(0, 0)

    @pl.when(i + 1 < pl.num_programs(0))
    def _():
        fetch(i + 1, (i + 1) % 2)

    slot = i % 2
    _wait_rows(y_hbm, buf.at[slot, 0], sem.at[slot, 0])
    _wait_rows(y_hbm, buf.at[slot, 1], sem.at[slot, 1])
    gate = gate_ref[...]
    o_ref[...] = h_ref[...] + gate[:, 0:1] * buf[slot, 0] + gate[:, 1:2] * buf[slot, 1]


def moe_combine(h, gates, ys, pos0, pos1, tm):
    T, D = h.shape
    return pl.pallas_call(
        _combine_kernel,
        grid_spec=pltpu.PrefetchScalarGridSpec(
            num_scalar_prefetch=2,
            grid=(T // tm,),
            in_specs=[pl.BlockSpec((tm, D), lambda i, a, b: (i, 0)),
                      pl.BlockSpec((tm, TOP_K), lambda i, a, b: (i, 0)),
                      pl.BlockSpec(memory_space=pl.ANY)],
            out_specs=pl.BlockSpec((tm, D), lambda i, a, b: (i, 0)),
            scratch_shapes=[pltpu.VMEM((2, TOP_K, tm, D), F32), pltpu.SemaphoreType.DMA((2, TOP_K))],
        ),
        out_shape=jax.ShapeDtypeStruct((T, D), F32),
        compiler_params=_cparams(("arbitrary",)),
        name="moe_combine",
    )(pos0, pos1, h, gates, ys)


MOE_TM = 512
MOE_TM_DOWN = 512
GATHER_TM = 256


def moe_layout(idx, tm):
    T, K = idx.shape
    flat = idx.reshape(-1)
    onehot = (flat[:, None] == jnp.arange(N_EXPERTS)[None, :]).astype(jnp.int32)
    rank = jnp.take_along_axis(jnp.cumsum(onehot, axis=0) - onehot, flat[:, None], axis=1)[:, 0]
    counts = jnp.sum(onehot, axis=0)
    tiles = (counts + tm - 1) // tm
    tile_end = jnp.cumsum(tiles)
    start = (tile_end - tiles) * tm
    pos = start[flat] + rank
    n_tiles = (T * K) // tm + N_EXPERTS
    src = jnp.zeros((n_tiles * tm,), jnp.int32).at[pos].set(jnp.arange(T * K, dtype=jnp.int32) // K)
    owner = jnp.sum((jnp.arange(n_tiles)[:, None] >= tile_end[None, :]).astype(jnp.int32), axis=1)
    tile_expert = jnp.minimum(owner, N_EXPERTS - 1).astype(jnp.int32)
    n_valid = tile_end[-1:].astype(jnp.int32)
    return pos.reshape(T, K).astype(jnp.int32), src, tile_expert, n_valid


def moe_ffn(h, g, w_router, b_router, w1, w3, w2):
    T, D = h.shape
    idx_t, gate_t = router(h, g, w_router, b_router, tm=512)
    idx = idx_t[:TOP_K].T
    gates = gate_t[:TOP_K].T
    pos, src, tile_expert, n_valid = moe_layout(idx, MOE_TM)
    xs = gather_norm(h, g, src, n_valid * (MOE_TM // GATHER_TM), GATHER_TM)
    hid = ffn_up(xs, w1, w3, tile_expert, n_valid, MOE_TM, fc=1024)
    split = MOE_TM // MOE_TM_DOWN
    ys = ffn_down(hid, w2, jnp.repeat(tile_expert, split), n_valid * split, MOE_TM_DOWN, tn=512)
    return moe_combine(h, gates, ys, pos[:, 0], pos[:, 1], GATHER_TM)


def dense_ffn(h, g, w1, w3, w2):
    T, D = h.shape
    tm_up, tm_down = T // 8, T // 16

    def one_group(tm):
        return jnp.zeros((T // tm,), jnp.int32), jnp.full((1,), T // tm, jnp.int32)

    xs = norm_cast(h, g, tm=512)
    hid = ffn_up(xs, w1, w3, *one_group(tm_up), tm_up, fc=512)
    return ffn_down(hid, w2, *one_group(tm_down), tm_down, tn=512, residual=h)


def kernel(x_prompt, x_sample, cache_k, cache_v, state_conv, state_ssm_re, state_ssm_im,
           norm1_g, w_in, w_out, a_w_dw, a_b_dw, a_ln_g, a_ln_b, a_w_pw, b_q_g, b_k_g,
           c_a_re, c_a_im, c_log_dt, c_b_re, c_b_im, c_c_re, c_c_im, c_d, c_w_glu, c_b_glu,
           norm2_g, ffn_w1, ffn_w3, ffn_w2, moe_w_router, moe_b_router, moe_w1, moe_w3, moe_w2):
    n_batch, L, D = x_prompt.shape
    n_b, n_new, _ = x_sample.shape
    depth = w_in.shape[0]
    c_a = a_w_pw.shape[-1]
    c_c = c_w_glu.shape[-1]
    c_b = D - c_a - c_c
    n_heads = c_b // HEAD_DIM
    window = cache_k.shape[2]
    tp, ts = n_batch * L, n_b * n_new
    T = tp + ts
    q_col0, v_col0, u_col0 = 2 * c_a, 2 * c_a + 2 * c_b, 2 * c_a + 3 * c_b
    assert q_col0 == c_b and window == DILATIONS[-1] * DIL_STEPS and L == window
    tm_big = T // 8

    p = dict(a_w_dw=a_w_dw, a_b_dw=a_b_dw, a_ln_g=a_ln_g, a_ln_b=a_ln_b, a_w_pw=a_w_pw,
             c_a_re=c_a_re, c_a_im=c_a_im, c_log_dt=c_log_dt, c_b_re=c_b_re, c_b_im=c_b_im,
             c_c_re=c_c_re, c_c_im=c_c_im, c_d=c_d, c_w_glu=c_w_glu, c_b_glu=c_b_glu)

    x = jnp.concatenate([x_prompt.reshape(tp, D),
                         x_sample.transpose(1, 0, 2).reshape(ts, D)], axis=0)
    conv_past_tm = state_conv.transpose(0, 2, 1, 3)

    w_win = jnp.asarray(_sample_weights(n_new, window, np.arange(window), n_heads))
    w_new = jnp.asarray(_sample_weights(n_new, window, window + np.arange(NEW_ROWS_PAD), n_heads))
    pad_new = lambda a: jnp.pad(a, ((0, 0), (0, NEW_ROWS_PAD - n_new), (0, 0)))

    st = {k: [] for k in ('k_p', 'v_p', 'conv_p', 're_p', 'im_p', 'k_s', 'v_s', 'conv_s', 're_s', 'im_s')}
    for l in range(depth):
        proj = norm_matmul(x, norm1_g[l], w_in, l, tm=tm_big, tn=768)
        qn, kn = qk_norm(proj, b_q_g[l], b_k_g[l], c_b, tm=512)
        ya_p, conv_p = conv_prompt(proj, n_batch, L, c_a, p, l)
        ya_s, conv_s_tm = conv_sample(proj, tp, n_new, n_b, c_a, conv_past_tm, p, l)
        yb_p = attn_prompt(qn, kn, proj, n_batch, L, c_b, v_col0)
        bm = lambda a: a.reshape(n_new, n_b, c_b).transpose(1, 0, 2)
        k_new, v_new = bm(kn[tp:]), bm(proj[tp:, v_col0:v_col0 + c_b])
        q_b = bm(qn[tp:]).reshape(n_b, n_new * n_heads, HEAD_DIM)
        yb_s = attn_sample(q_b, pad_new(k_new), pad_new(v_new), cache_k, cache_v, l, w_win, w_new)
        yb_s = yb_s.reshape(n_b, n_new, c_b).transpose(1, 0, 2).reshape(ts, c_b)
        sp = s5_params(p, l)
        yc_p, re_p, im_p = s5_prompt(proj, n_batch, L, c_c, u_col0, sp, p, l)
        h0_re = state_ssm_re[l].reshape(n_b, -1)
        h0_im = state_ssm_im[l].reshape(n_b, -1)
        yc_s, re_s, im_s = s5_sample(proj, tp, n_new, n_b, c_c, u_col0, h0_re, h0_im, sp, p, l)

        h = out_proj((ya_p, yb_p, yc_p), (ya_s, yb_s, yc_s), w_out, l, x, tm=ts, tn=1024)

        j = l // 2
        if l % 2 == 0:
            x = dense_ffn(h, norm2_g[l], ffn_w1[j:j + 1], ffn_w3[j:j + 1], ffn_w2[j:j + 1])
        else:
            x = moe_ffn(h, norm2_g[l], moe_w_router[j], moe_b_router[j],
                        moe_w1[j], moe_w3[j], moe_w2[j])

        g_shape = state_ssm_re.shape[2:]
        st['k_p'].append(kn)
        st['v_p'].append(proj)
        st['conv_p'].append(conv_p)
        st['re_p'].append(re_p[:, 0].reshape((n_batch,) + g_shape))
        st['im_p'].append(im_p[:, 0].reshape((n_batch,) + g_shape))
        st['k_s'].append(k_new.reshape(n_b, n_new, n_heads, HEAD_DIM))
        st['v_s'].append(v_new.reshape(n_b, n_new, n_heads, HEAD_DIM))
        st['conv_s'].append(conv_s_tm.transpose(1, 0, 2))
        st['re_s'].append(re_s.reshape((n_b,) + g_shape))
        st['im_s'].append(im_s.reshape((n_b,) + g_shape))

    y_p = x[:tp].reshape(n_batch, L, D)
    y_s = x[tp:].reshape(n_new, n_b, D).transpose(1, 0, 2)
    stk = lambda k: jnp.stack(st[k])
    kt, vt = kv_window(st['k_p'], st['v_p'], n_batch, L, c_b, v_col0)
    window_rows = lambda a: a.reshape(depth, n_batch, n_heads, HEAD_DIM, L).transpose(0, 1, 4, 2, 3)
    return (y_p, y_s, window_rows(kt), window_rows(vt), stk('conv_p'), stk('re_p'), stk('im_p'),
            stk('k_s'), stk('v_s'), stk('conv_s'), stk('re_s'), stk('im_s'))
```

```python
import functools
import math

import numpy as np
import jax
import jax.numpy as jnp
from jax import lax
from jax.experimental import pallas as pl
from jax.experimental.pallas import tpu as pltpu

F32 = jnp.float32
BF16 = jnp.bfloat16
EPS = 1e-6
NEG = -1e30

HEAD_DIM = 64
ATTN_SCALE = HEAD_DIM ** -0.5
ATTN_BLOCK = 128
DILATIONS = (1, 4, 16)
DIL_STEPS = 128
CONV_WIDTH = 31
CONV_PAST = CONV_WIDTH - 1
SSM_GROUP = 16
SSM_STATE = 64
N_EXPERTS = 8
TOP_K = 2
LANES = 128
SUBLANES = 8
VMEM_LIMIT = 56 * 1024 * 1024


def _cparams(sem, vmem=VMEM_LIMIT):
    return pltpu.CompilerParams(dimension_semantics=sem, vmem_limit_bytes=vmem)


def _nt_dot(a, b):
    return lax.dot_general(a, b, (((1,), (1,)), ((), ())), preferred_element_type=F32)


def _sigmoid(x):
    return 1.0 / (1.0 + jnp.exp(-x))


def _silu(x):
    return x * _sigmoid(x)


def _gelu_tanh(x):
    c = math.sqrt(2.0 / math.pi)
    return 0.5 * x * (1.0 + jnp.tanh(c * (x + 0.044715 * (x * x * x))))


def _norm_mm_kernel(x_ref, g_ref, w_ref, o_ref, u_scr):
    @pl.when(pl.program_id(1) == 0)
    def _():
        x = x_ref[...]
        ms = jnp.mean(x * x, axis=-1, keepdims=True)
        u_scr[...] = (x * lax.rsqrt(ms + EPS) * g_ref[...]).astype(BF16)

    o_ref[...] = jnp.dot(u_scr[...], w_ref[...].astype(BF16), preferred_element_type=F32)


def norm_matmul(x, g, w_stack, layer, tm, tn):
    T, D = x.shape
    N = w_stack.shape[-1]
    return pl.pallas_call(
        _norm_mm_kernel,
        grid=(T // tm, N // tn),
        in_specs=[
            pl.BlockSpec((tm, D), lambda i, j: (i, 0)),
            pl.BlockSpec((1, D), lambda i, j: (0, 0)),
            pl.BlockSpec((None, D, tn), lambda i, j: (layer, 0, j)),
        ],
        out_specs=pl.BlockSpec((tm, tn), lambda i, j: (i, j)),
        out_shape=jax.ShapeDtypeStruct((T, N), F32),
        scratch_shapes=[pltpu.VMEM((tm, D), BF16)],
        compiler_params=_cparams(("parallel", "arbitrary")),
        name="norm_matmul",
    )(x, g.reshape(1, D), w_stack)


def _qknorm_kernel(q_ref, k_ref, gq_ref, gk_ref, qo_ref, ko_ref):
    tm = q_ref.shape[0]
    lane = lax.broadcasted_iota(jnp.int32, (tm, LANES), 1)
    head0 = lane < HEAD_DIM

    def norm(x, g):
        sq = x * x
        s0 = jnp.sum(jnp.where(head0, sq, 0.0), axis=-1, keepdims=True)
        s1 = jnp.sum(jnp.where(head0, 0.0, sq), axis=-1, keepdims=True)
        ms = jnp.where(head0, s0, s1) * (1.0 / HEAD_DIM)
        return x * lax.rsqrt(ms + EPS) * g

    for t in range(q_ref.shape[1] // LANES):
        sl = slice(t * LANES, (t + 1) * LANES)
        qo_ref[:, sl] = norm(q_ref[:, sl], gq_ref[...]) * ATTN_SCALE
        ko_ref[:, sl] = norm(k_ref[:, sl], gk_ref[...])


def qk_norm(proj, gq, gk, c_b, tm):
    T = proj.shape[0]
    gq2 = jnp.concatenate([gq, gq]).reshape(1, LANES)
    gk2 = jnp.concatenate([gk, gk]).reshape(1, LANES)
    qblk = 1
    return pl.pallas_call(
        _qknorm_kernel,
        grid=(T // tm,),
        in_specs=[
            pl.BlockSpec((tm, c_b), lambda i: (i, qblk)),
            pl.BlockSpec((tm, c_b), lambda i: (i, qblk + 1)),
            pl.BlockSpec((1, LANES), lambda i: (0, 0)),
            pl.BlockSpec((1, LANES), lambda i: (0, 0)),
        ],
        out_specs=[pl.BlockSpec((tm, c_b), lambda i: (i, 0)),
                   pl.BlockSpec((tm, c_b), lambda i: (i, 0))],
        out_shape=[jax.ShapeDtypeStruct((T, c_b), F32)] * 2,
        compiler_params=_cparams(("parallel",)),
        name="qk_norm",
    )(proj, proj, gq2, gk2)


def _attn_prompt_kernel(q_ref, k_ref, v_ref, o_ref,
                        q0_s, q1_s, k_s, v_s,
                        m1, l1, a1, m4, l4, a4, m16, l16, a16):
    L = q_ref.shape[0]
    B = ATTN_BLOCK
    head0 = lax.broadcasted_iota(jnp.int32, (B, LANES), 1) < HEAD_DIM
    qi = lax.broadcasted_iota(jnp.int32, (2 * B, B), 0) % B
    ki = lax.broadcasted_iota(jnp.int32, (2 * B, B), 1)
    tri_cur = ki <= qi
    tri_prev = ki >= qi

    stats = {1: (m1, l1, a1), 4: (m4, l4, a4), 16: (m16, l16, a16)}

    for d in DILATIONS:
        n = L // d
        nb = n // B
        m_s, l_s, a_s = stats[d]
        for r in range(d):
            src = pl.ds(r, n, stride=d) if d > 1 else slice(None)
            dst = slice(r * n, (r + 1) * n)
            hm = lax.broadcasted_iota(jnp.int32, (n, LANES), 1) < HEAD_DIM
            q = q_ref[src, :]
            q0_s[dst, :] = jnp.where(hm, q, 0.0).astype(BF16)
            q1_s[dst, :] = jnp.where(hm, 0.0, q).astype(BF16)
            k_s[dst, :] = k_ref[src, :].astype(BF16)
            v_s[dst, :] = v_ref[src, :].astype(BF16)

        def block(j, carry, nb=nb, m_s=m_s, l_s=l_s, a_s=a_s):
            cur = pl.ds(pl.multiple_of(j * B, B), B)
            qb = jnp.concatenate([q0_s[cur, :], q1_s[cur, :]], axis=0)
            s_c = jnp.where(tri_cur, _nt_dot(qb, k_s[cur, :]), NEG)
            if nb > 1:
                prev = pl.ds(pl.multiple_of(jnp.maximum(j - 1, 0) * B, B), B)
                mask_prev = jnp.logical_and(tri_prev, (j % nb) > 0)
                s_p = jnp.where(mask_prev, _nt_dot(qb, k_s[prev, :]), NEG)
                m = jnp.max(jnp.maximum(s_c, s_p), axis=-1, keepdims=True)
                p_c, p_p = jnp.exp(s_c - m), jnp.exp(s_p - m)
                den = jnp.sum(p_c + p_p, axis=-1, keepdims=True)
                pv = (jnp.dot(p_c.astype(BF16), v_s[cur, :], preferred_element_type=F32)
                      + jnp.dot(p_p.astype(BF16), v_s[prev, :], preferred_element_type=F32))
            else:
                m = jnp.max(s_c, axis=-1, keepdims=True)
                p_c = jnp.exp(s_c - m)
                den = jnp.sum(p_c, axis=-1, keepdims=True)
                pv = jnp.dot(p_c.astype(BF16), v_s[cur, :], preferred_element_type=F32)
            m_s[cur, :] = jnp.where(head0, m[:B], m[B:])
            l_s[cur, :] = jnp.where(head0, den[:B], den[B:])
            a_s[cur, :] = jnp.where(head0, pv[:B], pv[B:])
            return carry

        lax.fori_loop(0, L // B, block, 0, unroll=16)

    dmax = DILATIONS[-1]
    nrow = L // dmax
    for r in range(dmax):
        o1 = pl.ds(r, nrow, stride=dmax)
        o4 = pl.ds((r % 4) * (L // 4) + r // 4, nrow, stride=dmax // 4)
        o16 = slice(r * nrow, (r + 1) * nrow)
        mm1, mm4, mm16 = m1[o1, :], m4[o4, :], m16[o16, :]
        mx = jnp.maximum(jnp.maximum(mm1, mm4), mm16)
        w1, w4, w16 = jnp.exp(mm1 - mx), jnp.exp(mm4 - mx), jnp.exp(mm16 - mx)
        num = w1 * a1[o1, :] + w4 * a4[o4, :] + w16 * a16[o16, :]
        den = w1 * l1[o1, :] + w4 * l4[o4, :] + w16 * l16[o16, :]
        o_ref[o1, :] = num / den


def attn_prompt(qn, kn, proj, n_batch, L, c_b, v_col0):
    n_hp = c_b // LANES
    vblk = v_col0 // LANES
    stat = [pltpu.VMEM((L, LANES), F32)] * 9
    return pl.pallas_call(
        _attn_prompt_kernel,
        grid=(n_batch, n_hp),
        in_specs=[
            pl.BlockSpec((L, LANES), lambda b, h: (b, h)),
            pl.BlockSpec((L, LANES), lambda b, h: (b, h)),
            pl.BlockSpec((L, LANES), lambda b, h: (b, vblk + h)),
        ],
        out_specs=pl.BlockSpec((L, LANES), lambda b, h: (b, h)),
        out_shape=jax.ShapeDtypeStruct((n_batch * L, c_b), F32),
        scratch_shapes=[pltpu.VMEM((L, LANES), BF16)] * 4 + stat,
        compiler_params=_cparams(("parallel", "parallel")),
        name="attn_prompt",
    )(qn, kn, proj)


def _kv_window_kernel(*refs, depth):
    k_refs, v_refs = refs[:depth], refs[depth:2 * depth]
    kt_ref, vt_ref = refs[2 * depth:]
    layer = pl.program_id(0)
    for l in range(depth):
        @pl.when(layer == l)
        def _(l=l):
            kt_ref[...] = k_refs[l][...].T
            vt_ref[...] = v_refs[l][...].T


def kv_window(kns, projs, n_batch, L, c_b, v_col0):
    depth = len(kns)
    n_hp = c_b // LANES
    vblk = v_col0 // LANES

    def spec(l, col0):
        def imap(layer, b, h):
            return jnp.where(layer == l, b, 0), col0 + jnp.where(layer == l, h, 0)
        return pl.BlockSpec((L, LANES), imap)

    out_spec = pl.BlockSpec((None, None, LANES, L), lambda layer, b, h: (layer, b, h, 0))
    out = jax.ShapeDtypeStruct((depth, n_batch, c_b, L), F32)
    return pl.pallas_call(
        functools.partial(_kv_window_kernel, depth=depth),
        grid=(depth, n_batch, n_hp),
        in_specs=[spec(l, 0) for l in range(depth)] + [spec(l, vblk) for l in range(depth)],
        out_specs=[out_spec, out_spec],
        out_shape=[out, out],
        compiler_params=_cparams(("arbitrary", "arbitrary", "arbitrary")),
        name="kv_window",
    )(*kns, *projs)


def _branch_multiplicity(dist):
    c = np.zeros(dist.shape, np.float32)
    for d in DILATIONS:
        c += ((dist >= 0) & (dist % d == 0) & (dist <= d * DIL_STEPS)).astype(np.float32)
    return c


NEW_ROWS_PAD = 16


def _sample_weights(n_new, past_len, pos, n_heads):
    s = np.repeat(np.arange(n_new), n_heads)[:, None]
    return _branch_multiplicity(past_len + s - np.asarray(pos)[None, :]).astype(np.float32)


def _attn_sample_kernel(q_ref, kt_ref, vt_ref, kn_ref, vn_ref, w_ref, wn_ref, o_ref):
    n_heads, hd, window = kt_ref.shape
    width = n_heads * hd
    q = q_ref[...]
    nq = q.shape[0]
    q2 = jnp.concatenate([q, q], axis=-1)
    qt = jnp.concatenate([q2] * (width // (2 * hd)), axis=-1)
    own_head = (lax.broadcasted_iota(jnp.int32, (nq, width), 0) % n_heads
                == lax.broadcasted_iota(jnp.int32, (nq, width), 1) // hd)
    qbd = jnp.where(own_head, qt, 0.0).astype(BF16)

    w, wn = w_ref[...], wn_ref[...]
    s_c = jnp.dot(qbd, kt_ref[...].reshape(width, window).astype(BF16),
                  preferred_element_type=F32)
    s_n = _nt_dot(qbd, kn_ref[...].astype(BF16))
    s_c = jnp.where(w > 0.0, s_c, NEG)
    s_n = jnp.where(wn > 0.0, s_n, NEG)
    m = jnp.maximum(jnp.max(s_c, axis=-1, keepdims=True), jnp.max(s_n, axis=-1, keepdims=True))
    p_c = w * jnp.exp(s_c - m)
    p_n = wn * jnp.exp(s_n - m)
    den = jnp.sum(p_c, axis=-1, keepdims=True) + jnp.sum(p_n, axis=-1, keepdims=True)
    full = (_nt_dot(p_c.astype(BF16), vt_ref[...].reshape(width, window).astype(BF16))
            + jnp.dot(p_n.astype(BF16), vn_ref[...].astype(BF16), preferred_element_type=F32))
    full = jnp.where(own_head, full, 0.0)
    acc = full[:, 0:LANES]
    for t in range(1, width // LANES):
        acc = acc + full[:, t * LANES:(t + 1) * LANES]
    o_ref[...] = (acc[:, :hd] + acc[:, hd:]) / den


def attn_sample(q_b, k_new, v_new, cache_k, cache_v, layer, w, wn):
    n_b, nq, _ = q_b.shape
    depth, _, window, n_heads, _ = cache_k.shape
    width = n_heads * HEAD_DIM
    kt = cache_k.transpose(0, 1, 3, 4, 2)
    vt = cache_v.transpose(0, 1, 3, 4, 2)
    cache_spec = pl.BlockSpec((None, None, n_heads, HEAD_DIM, window), lambda b: (layer, b, 0, 0, 0))
    new_spec = pl.BlockSpec((None, NEW_ROWS_PAD, width), lambda b: (b, 0, 0))
    q_spec = pl.BlockSpec((None, nq, HEAD_DIM), lambda b: (b, 0, 0))

    def const_spec(a):
        return pl.BlockSpec(a.shape, lambda b: (0, 0))

    return pl.pallas_call(
        _attn_sample_kernel,
        grid=(n_b,),
        in_specs=[q_spec, cache_spec, cache_spec, new_spec, new_spec, const_spec(w), const_spec(wn)],
        out_specs=q_spec,
        out_shape=jax.ShapeDtypeStruct((n_b, nq, HEAD_DIM), F32),
        compiler_params=_cparams(("parallel",)),
        name="attn_sample",
    )(q_b, kt, vt, k_new, v_new, w, wn)


def _layernorm_silu(y, g, b):
    mu = jnp.mean(y, axis=-1, keepdims=True)
    yc = y - mu
    var = jnp.mean(yc * yc, axis=-1, keepdims=True)
    return _silu(yc * lax.rsqrt(var + EPS) * g + b)


CONV_HALO = 32
CONV_ROWS = 32


def _conv_prompt_kernel(val_ref, gate_ref, wdw_ref, bdw_ref, lng_ref, lnb_ref, wpw_ref,
                        y_ref, st_ref, xp_s, sh_s, y_s):
    c = pl.program_id(1)
    lc, ca = val_ref.shape
    off = CONV_HALO - CONV_PAST

    @pl.when(c == 0)
    def _():
        xp_s[0:CONV_HALO, :] = jnp.zeros((CONV_HALO, ca), F32)

    @pl.when(c > 0)
    def _():
        xp_s[0:CONV_HALO, :] = xp_s[lc:lc + CONV_HALO, :]

    xp_s[CONV_HALO:CONV_HALO + lc, :] = val_ref[...] * _sigmoid(gate_ref[...])

    n_sh = sh_s.shape[1]
    for b in range(1, SUBLANES):
        for lb in range(ca // LANES):
            ls = slice(lb * LANES, (lb + 1) * LANES)
            sh_s[b - 1, :, ls] = xp_s[b:b + n_sh, ls]

    for rb in range(lc // CONV_ROWS):
        for lb in range(ca // LANES):
            ls = slice(lb * LANES, (lb + 1) * LANES)
            acc = jnp.zeros((CONV_ROWS, LANES), F32)
            for j in range(CONV_WIDTH):
                a, b = divmod(off + j, SUBLANES)
                r0 = rb * CONV_ROWS + a * SUBLANES
                rows = xp_s[r0:r0 + CONV_ROWS, ls] if b == 0 else sh_s[b - 1, r0:r0 + CONV_ROWS, ls]
                acc = acc + wdw_ref[j:j + 1, ls] * rows
            y_s[rb * CONV_ROWS:(rb + 1) * CONV_ROWS, ls] = acc

    z = _layernorm_silu(y_s[...] + bdw_ref[...], lng_ref[...], lnb_ref[...])
    y_ref[...] = jnp.dot(z.astype(BF16), wpw_ref[...].astype(BF16), preferred_element_type=F32)

    @pl.when(c == pl.num_programs(1) - 1)
    def _():
        st_ref[...] = xp_s[lc + off:lc + CONV_HALO, :]


def _layer_vec(a, layer):
    return a[layer].reshape(1, -1)


def conv_prompt(proj, n_batch, L, c_a, p, layer, lc=256):
    nch = L // lc
    vec = pl.BlockSpec((1, c_a), lambda b, c: (0, 0))
    return pl.pallas_call(
        _conv_prompt_kernel,
        grid=(n_batch, nch),
        in_specs=[
            pl.BlockSpec((lc, c_a), lambda b, c: (b * nch + c, 0)),
            pl.BlockSpec((lc, c_a), lambda b, c: (b * nch + c, 1)),
            pl.BlockSpec((None, CONV_WIDTH, c_a), lambda b, c: (layer, 0, 0)),
            vec, vec, vec,
            pl.BlockSpec((None, c_a, c_a), lambda b, c: (layer, 0, 0)),
        ],
        out_specs=[pl.BlockSpec((lc, c_a), lambda b, c: (b * nch + c, 0)),
                   pl.BlockSpec((None, CONV_PAST, c_a), lambda b, c: (b, 0, 0))],
        out_shape=[jax.ShapeDtypeStruct((n_batch * L, c_a), F32),
                   jax.ShapeDtypeStruct((n_batch, CONV_PAST, c_a), F32)],
        scratch_shapes=[pltpu.VMEM((lc + CONV_HALO, c_a), F32),
                        pltpu.VMEM((SUBLANES - 1, lc + CONV_HALO - SUBLANES, c_a), F32),
                        pltpu.VMEM((lc, c_a), F32)],
        compiler_params=_cparams(("parallel", "arbitrary")),
        name="conv_prompt",
    )(proj, proj, p['a_w_dw'], _layer_vec(p['a_b_dw'], layer), _layer_vec(p['a_ln_g'], layer),
      _layer_vec(p['a_ln_b'], layer), p['a_w_pw'])


def _conv_sample_kernel(val_ref, gate_ref, past_ref, wdw_ref, bdw_ref, lng_ref, lnb_ref, wpw_ref,
                        y_ref, st_ref, y_s):
    n_b = past_ref.shape[1]
    n_new = val_ref.shape[0] // n_b
    g = val_ref[...] * _sigmoid(gate_ref[...])

    def xp(t):
        if t < CONV_PAST:
            return past_ref[t]
        return g[(t - CONV_PAST) * n_b:(t - CONV_PAST + 1) * n_b, :]

    for s in range(n_new):
        acc = jnp.zeros_like(xp(0))
        for j in range(CONV_WIDTH):
            acc = acc + wdw_ref[j:j + 1, :] * xp(s + j)
        y_s[s * n_b:(s + 1) * n_b, :] = acc
    z = _layernorm_silu(y_s[...] + bdw_ref[...], lng_ref[...], lnb_ref[...])
    y_ref[...] = jnp.dot(z.astype(BF16), wpw_ref[...].astype(BF16), preferred_element_type=F32)
    for t in range(CONV_PAST):
        st_ref[t] = xp(t + n_new)


def conv_sample(proj, row0, n_new, n_b, c_a, past_tm, p, layer):
    ts = n_new * n_b
    vec = pl.BlockSpec((1, c_a), lambda i: (0, 0))
    return pl.pallas_call(
        _conv_sample_kernel,
        grid=(1,),
        in_specs=[
            pl.BlockSpec((ts, c_a), lambda i: (row0 // ts, 0)),
            pl.BlockSpec((ts, c_a), lambda i: (row0 // ts, 1)),
            pl.BlockSpec((None, CONV_PAST, n_b, c_a), lambda i: (layer, 0, 0, 0)),
            pl.BlockSpec((None, CONV_WIDTH, c_a), lambda i: (layer, 0, 0)),
            vec, vec, vec,
            pl.BlockSpec((None, c_a, c_a), lambda i: (layer, 0, 0)),
        ],
        out_specs=[pl.BlockSpec((ts, c_a), lambda i: (0, 0)),
                   pl.BlockSpec((CONV_PAST, n_b, c_a), lambda i: (0, 0, 0))],
        out_shape=[jax.ShapeDtypeStruct((ts, c_a), F32),
                   jax.ShapeDtypeStruct((CONV_PAST, n_b, c_a), F32)],
        scratch_shapes=[pltpu.VMEM((ts, c_a), F32)],
        compiler_params=_cparams(("arbitrary",)),
        name="conv_sample",
    )(proj, proj, past_tm, p['a_w_dw'], _layer_vec(p['a_b_dw'], layer),
      _layer_vec(p['a_ln_g'], layer), _layer_vec(p['a_ln_b'], layer), p['a_w_pw'])


def _s5_param_kernel(are_ref, aim_ref, ldt_ref, bre_ref, bim_ref,
                     pre_ref, pim_ref, bbre_ref, bbim_ref):
    a_re, a_im = are_ref[...], aim_ref[...]
    dt = jnp.exp(ldt_ref[...])
    mag = jnp.exp(a_re * dt)
    ab_re, ab_im = mag * jnp.cos(a_im * dt), mag * jnp.sin(a_im * dt)
    nr, ni = ab_re - 1.0, ab_im
    inv = 1.0 / (a_re * a_re + a_im * a_im)
    f_re = (nr * a_re + ni * a_im) * inv
    f_im = (ni * a_re - nr * a_im) * inv
    b_re, b_im = bre_ref[...], bim_ref[...]
    bbre_ref[...] = f_re * b_re - f_im * b_im
    bbim_ref[...] = f_re * b_im + f_im * b_re
    pr, pi = ab_re, ab_im
    pre_ref[0] = pr
    pim_ref[0] = pi
    for k in range(1, SUBLANES):
        pr, pi = pr * ab_re - pi * ab_im, pr * ab_im + pi * ab_re
        pre_ref[k] = pr
        pim_ref[k] = pi


def s5_params(p, layer):
    a_re, a_im = p['c_a_re'][layer], p['c_a_im'][layer]
    G, N = a_re.shape
    C = SSM_GROUP
    b_re_t = p['c_b_re'][layer].transpose(0, 2, 1)
    b_im_t = p['c_b_im'][layer].transpose(0, 2, 1)
    pre, pim, bbre, bbim = pl.pallas_call(
        _s5_param_kernel,
        out_shape=[jax.ShapeDtypeStruct((SUBLANES, G, 1, N), F32)] * 2
        + [jax.ShapeDtypeStruct((G, C, N), F32)] * 2,
        name="s5_params",
    )(a_re.reshape(G, 1, N), a_im.reshape(G, 1, N), p['c_log_dt'][layer].reshape(G, 1, 1),
      b_re_t, b_im_t)
    S = G * N
    pre, pim = pre.reshape(SUBLANES, S), pim.reshape(SUBLANES, S)
    t = np.arange(SUBLANES)[:, None]
    tabs = []
    for k in (1, 2, 4):
        keep = jnp.asarray(t >= k)
        tabs += [jnp.where(keep, pre[k - 1][None, :], 0.0), jnp.where(keep, pim[k - 1][None, :], 0.0)]
    tabs += [pre, pim]
    tab = jnp.stack(tabs)
    eye = jnp.eye(G, dtype=F32)

    def in_proj(bb):
        return (eye[:, None, :, None] * bb[:, :, None, :]).reshape(G * C, S).astype(BF16)

    def out_proj(c):
        return (eye[:, None, :, None] * c.transpose(0, 2, 1)[:, :, None, :]).reshape(S, G * C)

    c_cat = jnp.concatenate([out_proj(p['c_c_re'][layer]), -out_proj(p['c_c_im'][layer])]).astype(BF16)
    return dict(tab=tab, bb_re=in_proj(bbre), bb_im=in_proj(bbim), c_cat=c_cat,
                ab_re=pre[0:1], ab_im=pim[0:1])


def _s5_readout(u, xr, xi, ccat_ref, d_ref, wglu_ref, bglu_ref):
    S = xr.shape[-1]
    y = (jnp.dot(xr.astype(BF16), ccat_ref[0:S, :], preferred_element_type=F32)
         + jnp.dot(xi.astype(BF16), ccat_ref[S:2 * S, :], preferred_element_type=F32)
         + d_ref[...] * u)
    z = _gelu_tanh(y)
    gate = jnp.dot(z.astype(BF16), wglu_ref[...].astype(BF16), preferred_element_type=F32)
    return z * _sigmoid(gate + bglu_ref[...])


S5_LANE_GROUP = 512


def _s5_prompt_kernel(u_ref, bbre_ref, bbim_ref, tab_ref, ccat_ref, d_ref, wglu_ref, bglu_ref,
                      y_ref, hre_ref, him_ref, xr_s, xi_s, h_s):
    c = pl.program_id(1)
    tc = u_ref.shape[0]
    S = xr_s.shape[1]

    @pl.when(c == 0)
    def _():
        h_s[...] = jnp.zeros_like(h_s)

    u = u_ref[...]
    ub = u.astype(BF16)
    xr_s[...] = jnp.dot(ub, bbre_ref[...], preferred_element_type=F32)
    xi_s[...] = jnp.dot(ub, bbim_ref[...], preferred_element_type=F32)

    for jg in range(S // S5_LANE_GROUP):
        ls = slice(jg * S5_LANE_GROUP, (jg + 1) * S5_LANE_GROUP)
        levels = [(k, tab_ref[2 * i, :, ls], tab_ref[2 * i + 1, :, ls])
                  for i, k in enumerate((1, 2, 4))]
        cr, ci = tab_ref[6, :, ls], tab_ref[7, :, ls]

        def tile(i, carry, ls=ls, levels=levels, cr=cr, ci=ci):
            hr, hi = carry
            rows = pl.ds(pl.multiple_of(i * SUBLANES, SUBLANES), SUBLANES)
            br, bi = xr_s[rows, ls], xi_s[rows, ls]
            for k, pr, pi in levels:
                sr, si = pltpu.roll(br, k, axis=0), pltpu.roll(bi, k, axis=0)
                br, bi = br + pr * sr - pi * si, bi + pr * si + pi * sr
            br, bi = br + cr * hr - ci * hi, bi + cr * hi + ci * hr
            xr_s[rows, ls] = br
            xi_s[rows, ls] = bi
            last = slice(SUBLANES - 1, SUBLANES)
            return (jnp.broadcast_to(br[last, :], br.shape), jnp.broadcast_to(bi[last, :], bi.shape))

        hr, hi = lax.fori_loop(0, tc // SUBLANES, tile, (h_s[0, :, ls], h_s[1, :, ls]))
        h_s[0, :, ls] = hr
        h_s[1, :, ls] = hi

    y_ref[...] = _s5_readout(u, xr_s[...], xi_s[...], ccat_ref, d_ref, wglu_ref, bglu_ref)
    hre_ref[...] = h_s[0]
    him_ref[...] = h_s[1]


def s5_prompt(proj, n_batch, L, c_c, u_col0, sp, p, layer, tc=256):
    nch = L // tc
    S = sp['tab'].shape[-1]
    ublk = u_col0 // c_c

    def full(a):
        return pl.BlockSpec(a.shape, lambda b, c: (0,) * a.ndim)

    d = p['c_d'][layer].reshape(1, c_c)
    bglu = p['c_b_glu'][layer].reshape(1, c_c)
    return pl.pallas_call(
        _s5_prompt_kernel,
        grid=(n_batch, nch),
        in_specs=[
            pl.BlockSpec((tc, c_c), lambda b, c: (b * nch + c, ublk)),
            full(sp['bb_re']), full(sp['bb_im']), full(sp['tab']), full(sp['c_cat']), full(d),
            pl.BlockSpec((None, c_c, c_c), lambda b, c: (layer, 0, 0)),
            full(bglu),
        ],
        out_specs=[pl.BlockSpec((tc, c_c), lambda b, c: (b * nch + c, 0)),
                   pl.BlockSpec((None, SUBLANES, S), lambda b, c: (b, 0, 0)),
                   pl.BlockSpec((None, SUBLANES, S), lambda b, c: (b, 0, 0))],
        out_shape=[jax.ShapeDtypeStruct((n_batch * L, c_c), F32),
                   jax.ShapeDtypeStruct((n_batch, SUBLANES, S), F32),
                   jax.ShapeDtypeStruct((n_batch, SUBLANES, S), F32)],
        scratch_shapes=[pltpu.VMEM((tc, S), F32), pltpu.VMEM((tc, S), F32),
                        pltpu.VMEM((2, SUBLANES, S), F32)],
        compiler_params=_cparams(("parallel", "arbitrary")),
        name="s5_prompt",
    )(proj, sp['bb_re'], sp['bb_im'], sp['tab'], sp['c_cat'], d, p['c_w_glu'], bglu)


def _s5_sample_kernel(u_ref, h0re_ref, h0im_ref, bbre_ref, bbim_ref, abre_ref, abim_ref,
                      ccat_ref, d_ref, wglu_ref, bglu_ref,
                      y_ref, hre_ref, him_ref, xr_s, xi_s):
    n_b = h0re_ref.shape[0]
    n_new = u_ref.shape[0] // n_b
    u = u_ref[...]
    ub = u.astype(BF16)
    xr_s[...] = jnp.dot(ub, bbre_ref[...], preferred_element_type=F32)
    xi_s[...] = jnp.dot(ub, bbim_ref[...], preferred_element_type=F32)
    ar, ai = abre_ref[...], abim_ref[...]
    hr, hi = h0re_ref[...], h0im_ref[...]
    for s in range(n_new):
        rows = slice(s * n_b, (s + 1) * n_b)
        hr, hi = ar * hr - ai * hi + xr_s[rows, :], ar * hi + ai * hr + xi_s[rows, :]
        xr_s[rows, :] = hr
        xi_s[rows, :] = hi
    y_ref[...] = _s5_readout(u, xr_s[...], xi_s[...], ccat_ref, d_ref, wglu_ref, bglu_ref)
    hre_ref[...] = hr
    him_ref[...] = hi


def s5_sample(proj, row0, n_new, n_b, c_c, u_col0, h0_re, h0_im, sp, p, layer):
    ts = n_new * n_b
    S = sp['tab'].shape[-1]

    def full(a):
        return pl.BlockSpec(a.shape, lambda i: (0,) * a.ndim)

    d = p['c_d'][layer].reshape(1, c_c)
    bglu = p['c_b_glu'][layer].reshape(1, c_c)
    return pl.pallas_call(
        _s5_sample_kernel,
        grid=(1,),
        in_specs=[
            pl.BlockSpec((ts, c_c), lambda i: (row0 // ts, u_col0 // c_c)),
            full(h0_re), full(h0_im), full(sp['bb_re']), full(sp['bb_im']),
            full(sp['ab_re']), full(sp['ab_im']), full(sp['c_cat']), full(d),
            pl.BlockSpec((None, c_c, c_c), lambda i: (layer, 0, 0)),
            full(bglu),
        ],
        out_specs=[pl.BlockSpec((ts, c_c), lambda i: (0, 0)),
                   pl.BlockSpec((n_b, S), lambda i: (0, 0)),
                   pl.BlockSpec((n_b, S), lambda i: (0, 0))],
        out_shape=[jax.ShapeDtypeStruct((ts, c_c), F32),
                   jax.ShapeDtypeStruct((n_b, S), F32),
                   jax.ShapeDtypeStruct((n_b, S), F32)],
        scratch_shapes=[pltpu.VMEM((ts, S), F32), pltpu.VMEM((ts, S), F32)],
        compiler_params=_cparams(("arbitrary",)),
        name="s5_sample",
    )(proj, h0_re, h0_im, sp['bb_re'], sp['bb_im'], sp['ab_re'], sp['ab_im'], sp['c_cat'], d,
      p['c_w_glu'], bglu)


def _out_proj_kernel(yap_ref, ybp_ref, ycp_ref, yas_ref, ybs_ref, ycs_ref, w_ref, x_ref,
                     o_ref, cat_s, w_s, *, n_prompt_tiles):
    i = pl.program_id(1)

    @pl.when(i == 0)
    def _():
        w_s[...] = w_ref[...].astype(BF16)

    def fill(ya_ref, yb_ref, yc_ref):
        ca, cb = ya_ref.shape[1], yb_ref.shape[1]
        cat_s[:, 0:ca] = ya_ref[...].astype(BF16)
        cat_s[:, ca:ca + cb] = yb_ref[...].astype(BF16)
        cat_s[:, ca + cb:] = yc_ref[...].astype(BF16)

    @pl.when(i < n_prompt_tiles)
    def _():
        fill(yap_ref, ybp_ref, ycp_ref)

    @pl.when(i >= n_prompt_tiles)
    def _():
        fill(yas_ref, ybs_ref, ycs_ref)

    o_ref[...] = x_ref[...] + jnp.dot(cat_s[...], w_s[...], preferred_element_type=F32)


def out_proj(prompt_parts, sample_parts, w_stack, layer, x, tm, tn):
    T, D = x.shape
    dm = w_stack.shape[1]
    n_p = prompt_parts[0].shape[0] // tm
    assert prompt_parts[0].shape[0] % tm == 0 and sample_parts[0].shape[0] % tm == 0

    def prompt_spec(a):
        return pl.BlockSpec((tm, a.shape[1]), lambda j, i: (jnp.minimum(i, n_p - 1), 0))

    def sample_spec(a):
        return pl.BlockSpec((tm, a.shape[1]), lambda j, i: (jnp.maximum(i - n_p, 0), 0))

    return pl.pallas_call(
        functools.partial(_out_proj_kernel, n_prompt_tiles=n_p),
        grid=(D // tn, T // tm),
        in_specs=[prompt_spec(a) for a in prompt_parts] + [sample_spec(a) for a in sample_parts]
        + [pl.BlockSpec((None, dm, tn), lambda j, i: (layer, 0, j)),
           pl.BlockSpec((tm, tn), lambda j, i: (i, j))],
        out_specs=pl.BlockSpec((tm, tn), lambda j, i: (i, j)),
        out_shape=jax.ShapeDtypeStruct((T, D), F32),
        scratch_shapes=[pltpu.VMEM((tm, dm), BF16), pltpu.VMEM((dm, tn), BF16)],
        compiler_params=_cparams(("arbitrary", "arbitrary")),
        name="out_proj",
    )(*prompt_parts, *sample_parts, w_stack, x)


def _norm_cast_kernel(x_ref, g_ref, o_ref):
    x = x_ref[...]
    ms = jnp.mean(x * x, axis=-1, keepdims=True)
    o_ref[...] = (x * lax.rsqrt(ms + EPS) * g_ref[...]).astype(BF16)


def norm_cast(x, g, tm):
    T, D = x.shape
    return pl.pallas_call(
        _norm_cast_kernel,
        grid=(T // tm,),
        in_specs=[pl.BlockSpec((tm, D), lambda i: (i, 0)), pl.BlockSpec((1, D), lambda i: (0, 0))],
        out_specs=pl.BlockSpec((tm, D), lambda i: (i, 0)),
        out_shape=jax.ShapeDtypeStruct((T, D), BF16),
        compiler_params=_cparams(("parallel",)),
        name="norm_cast",
    )(x, g.reshape(1, D))


def _new_expert(te_ref, i):
    prev = te_ref[jnp.maximum(i - 1, 0)]
    return jnp.logical_or(i == 0, te_ref[i] != prev)


def _ffn_up_kernel(te_ref, nv_ref, x_ref, w1_ref, w3_ref, h_ref, w1_s, w3_s):
    i = pl.program_id(1)

    @pl.when(_new_expert(te_ref, i))
    def _():
        w1_s[...] = w1_ref[...].astype(BF16)
        w3_s[...] = w3_ref[...].astype(BF16)

    @pl.when(i < nv_ref[0])
    def _():
        x = x_ref[...]
        a = jnp.dot(x, w1_s[...], preferred_element_type=F32)
        b = jnp.dot(x, w3_s[...], preferred_element_type=F32)
        h_ref[...] = (_silu(a) * b).astype(BF16)

    @pl.when(i >= nv_ref[0])
    def _():
        h_ref[...] = jnp.zeros_like(h_ref)


def ffn_up(xs, w1, w3, tile_expert, n_valid, tm, fc):
    R, D = xs.shape
    F = w1.shape[-1]
    wspec = pl.BlockSpec((None, D, fc), lambda j, i, te, nv: (te[i], 0, j))
    return pl.pallas_call(
        _ffn_up_kernel,
        grid_spec=pltpu.PrefetchScalarGridSpec(
            num_scalar_prefetch=2,
            grid=(pl.cdiv(F, fc), R // tm),
            in_specs=[pl.BlockSpec((tm, D), lambda j, i, te, nv: (i, 0)), wspec, wspec],
            out_specs=pl.BlockSpec((tm, fc), lambda j, i, te, nv: (i, j)),
            scratch_shapes=[pltpu.VMEM((D, fc), BF16)] * 2,
        ),
        out_shape=jax.ShapeDtypeStruct((R, F), BF16),
        compiler_params=_cparams(("arbitrary", "arbitrary")),
        name="ffn_up",
    )(tile_expert, n_valid, xs, w1, w3)


def _ffn_down_kernel(te_ref, nv_ref, h_ref, w2_ref, *rest, residual):
    if residual:
        r_ref, o_ref, w2_s = rest
    else:
        o_ref, w2_s = rest
    i = pl.program_id(1)

    @pl.when(_new_expert(te_ref, i))
    def _():
        w2_s[...] = w2_ref[...].astype(BF16)

    @pl.when(i < nv_ref[0])
    def _():
        y = jnp.dot(h_ref[...], w2_s[...], preferred_element_type=F32)
        o_ref[...] = r_ref[...] + y if residual else y

    @pl.when(i >= nv_ref[0])
    def _():
        o_ref[...] = jnp.zeros_like(o_ref)


def ffn_down(h, w2, tile_expert, n_valid, tm, tn, residual=None):
    R, F = h.shape
    D = w2.shape[-1]
    tile = pl.BlockSpec((tm, tn), lambda j, i, te, nv: (i, j))
    in_specs = [pl.BlockSpec((tm, F), lambda j, i, te, nv: (i, 0)),
                pl.BlockSpec((None, F, tn), lambda j, i, te, nv: (te[i], 0, j))]
    args = [h, w2]
    if residual is not None:
        in_specs.append(tile)
        args.append(residual)
    return pl.pallas_call(
        functools.partial(_ffn_down_kernel, residual=residual is not None),
        grid_spec=pltpu.PrefetchScalarGridSpec(
            num_scalar_prefetch=2,
            grid=(D // tn, R // tm),
            in_specs=in_specs,
            out_specs=tile,
            scratch_shapes=[pltpu.VMEM((F, tn), BF16)],
        ),
        out_shape=jax.ShapeDtypeStruct((R, D), F32),
        compiler_params=_cparams(("arbitrary", "arbitrary")),
        name="ffn_down",
    )(tile_expert, n_valid, *args)


def _router_kernel(x_ref, g_ref, wr_ref, br_ref, idx_ref, gate_ref):
    x = x_ref[...]
    ms = jnp.mean(x * x, axis=-1, keepdims=True)
    u = x * lax.rsqrt(ms + EPS) * g_ref[...]
    logits = lax.dot_general(wr_ref[...], u, (((1,), (1,)), ((), ())),
                             precision=lax.Precision.HIGHEST,
                             preferred_element_type=F32) + br_ref[...]
    n_e = logits.shape[0]
    eid = lax.broadcasted_iota(jnp.int32, logits.shape, 0)
    m1 = jnp.max(logits, axis=0, keepdims=True)
    i1 = jnp.min(jnp.where(logits == m1, eid, n_e), axis=0, keepdims=True)
    rest = jnp.where(eid == i1, -jnp.inf, logits)
    m2 = jnp.max(rest, axis=0, keepdims=True)
    i2 = jnp.min(jnp.where(rest == m2, eid, n_e), axis=0, keepdims=True)
    e2 = jnp.exp(m2 - m1)
    g1 = 1.0 / (1.0 + e2)
    idx_ref[...] = jnp.where(eid == 0, i1, i2)
    gate_ref[...] = jnp.where(eid == 0, g1, e2 * g1)


def router(x, g, w_router, b_router, tm):
    T, D = x.shape
    E = w_router.shape[-1]
    return pl.pallas_call(
        _router_kernel,
        grid=(T // tm,),
        in_specs=[pl.BlockSpec((tm, D), lambda i: (i, 0)),
                  pl.BlockSpec((1, D), lambda i: (0, 0)),
                  pl.BlockSpec((E, D), lambda i: (0, 0)),
                  pl.BlockSpec((E, 1), lambda i: (0, 0))],
        out_specs=[pl.BlockSpec((E, tm), lambda i: (0, i)),
                   pl.BlockSpec((E, tm), lambda i: (0, i))],
        out_shape=[jax.ShapeDtypeStruct((E, T), jnp.int32), jax.ShapeDtypeStruct((E, T), F32)],
        compiler_params=_cparams(("parallel",)),
        name="moe_router",
    )(x, g.reshape(1, D), w_router.T, b_router.reshape(E, 1))


def _row_copy(src_hbm, row, dst, r, sem):
    return pltpu.make_async_copy(src_hbm.at[pl.ds(row, 1), :], dst.at[pl.ds(r, 1), :], sem)


def _wait_rows(src_hbm, dst, sem):
    pltpu.make_async_copy(src_hbm.at[pl.ds(0, dst.shape[0]), :], dst, sem).wait()


GATHER_UNROLL = 8


def _rows_loop(tm, fn):
    def body(it, c):
        for u in range(GATHER_UNROLL):
            fn(it * GATHER_UNROLL + u)
        return c

    lax.fori_loop(0, tm // GATHER_UNROLL, body, 0)


def _gather_norm_kernel(src_ref, nv_ref, x_hbm, g_ref, o_ref, buf, sem):
    tm = buf.shape[1]
    i = pl.program_id(0)
    n_used = nv_ref[0]

    def fetch(tile, slot):
        _rows_loop(tm, lambda r: _row_copy(x_hbm, src_ref[tile * tm + r], buf.at[slot], r,
                                           sem.at[slot]).start())

    @pl.when(i == 0)
    def _():
        fetch(0, 0)

    @pl.when(i + 1 < n_used)
    def _():
        fetch(i + 1, (i + 1) % 2)

    @pl.when(i < n_used)
    def _():
        slot = i % 2
        _wait_rows(x_hbm, buf.at[slot], sem.at[slot])
        x = buf[slot]
        ms = jnp.mean(x * x, axis=-1, keepdims=True)
        o_ref[...] = (x * lax.rsqrt(ms + EPS) * g_ref[...]).astype(BF16)

    @pl.when(i >= n_used)
    def _():
        o_ref[...] = jnp.zeros_like(o_ref)


def gather_norm(x, g, src, n_used, tm):
    T, D = x.shape
    R = src.shape[0]
    return pl.pallas_call(
        _gather_norm_kernel,
        grid_spec=pltpu.PrefetchScalarGridSpec(
            num_scalar_prefetch=2,
            grid=(R // tm,),
            in_specs=[pl.BlockSpec(memory_space=pl.ANY),
                      pl.BlockSpec((1, D), lambda i, s, n: (0, 0))],
            out_specs=pl.BlockSpec((tm, D), lambda i, s, n: (i, 0)),
            scratch_shapes=[pltpu.VMEM((2, tm, D), F32), pltpu.SemaphoreType.DMA((2,))],
        ),
        out_shape=jax.ShapeDtypeStruct((R, D), BF16),
        compiler_params=_cparams(("arbitrary",)),
        name="moe_gather",
    )(src, n_used, x, g.reshape(1, D))


def _combine_kernel(p0_ref, p1_ref, h_ref, gate_ref, y_hbm, o_ref, buf, sem):
    tm = buf.shape[2]
    i = pl.program_id(0)

    def fetch(tile, slot):
        def one(r):
            _row_copy(y_hbm, p0_ref[tile * tm + r], buf.at[slot, 0], r, sem.at[slot, 0]).start()
            _row_copy(y_hbm, p1_ref[tile * tm + r], buf.at[slot, 1], r, sem.at[slot, 1]).start()

        _rows_loop(tm, one)

    @pl.when(i == 0)
    def _():
        fetch(0, 0)

    @pl.when(i + 1 < pl.num_programs(0))
    def _():
        fetch(i + 1, (i + 1) % 2)

    slot = i % 2
    _wait_rows(y_hbm, buf.at[slot, 0], sem.at[slot, 0])
    _wait_rows(y_hbm, buf.at[slot, 1], sem.at[slot, 1])
    gate = gate_ref[...]
    o_ref[...] = h_ref[...] + gate[:, 0:1] * buf[slot, 0] + gate[:, 1:2] * buf[slot, 1]


def moe_combine(h, gates, ys, pos0, pos1, tm):
    T, D = h.shape
    return pl.pallas_call(
        _combine_kernel,
        grid_spec=pltpu.PrefetchScalarGridSpec(
            num_scalar_prefetch=2,
            grid=(T // tm,),
            in_specs=[pl.BlockSpec((tm, D), lambda i, a, b: (i, 0)),
                      pl.BlockSpec((tm, TOP_K), lambda i, a, b: (i, 0)),
                      pl.BlockSpec(memory_space=pl.ANY)],
            out_specs=pl.BlockSpec((tm, D), lambda i, a, b: (i, 0)),
            scratch_shapes=[pltpu.VMEM((2, TOP_K, tm, D), F32), pltpu.SemaphoreType.DMA((2, TOP_K))],
        ),
        out_shape=jax.ShapeDtypeStruct((T, D), F32),
        compiler_params=_cparams(("arbitrary",)),
        name="moe_combine",
    )(pos0, pos1, h, gates, ys)


MOE_TM = 512
MOE_TM_DOWN = 512
GATHER_TM = 256


def moe_layout(idx, tm):
    T, K = idx.shape
    flat = idx.reshape(-1)
    onehot = (flat[:, None] == jnp.arange(N_EXPERTS)[None, :]).astype(jnp.int32)
    rank = jnp.take_along_axis(jnp.cumsum(onehot, axis=0) - onehot, flat[:, None], axis=1)[:, 0]
    counts = jnp.sum(onehot, axis=0)
    tiles = (counts + tm - 1) // tm
    tile_end = jnp.cumsum(tiles)
    start = (tile_end - tiles) * tm
    pos = start[flat] + rank
    n_tiles = (T * K) // tm + N_EXPERTS
    src = jnp.zeros((n_tiles * tm,), jnp.int32).at[pos].set(jnp.arange(T * K, dtype=jnp.int32) // K)
    owner = jnp.sum((jnp.arange(n_tiles)[:, None] >= tile_end[None, :]).astype(jnp.int32), axis=1)
    tile_expert = jnp.minimum(owner, N_EXPERTS - 1).astype(jnp.int32)
    n_valid = tile_end[-1:].astype(jnp.int32)
    return pos.reshape(T, K).astype(jnp.int32), src, tile_expert, n_valid


def moe_ffn(h, g, w_router, b_router, w1, w3, w2):
    T, D = h.shape
    idx_t, gate_t = router(h, g, w_router, b_router, tm=512)
    idx = idx_t[:TOP_K].T
    gates = gate_t[:TOP_K].T
    pos, src, tile_expert, n_valid = moe_layout(idx, MOE_TM)
    xs = gather_norm(h, g, src, n_valid * (MOE_TM // GATHER_TM), GATHER_TM)
    hid = ffn_up(xs, w1, w3, tile_expert, n_valid, MOE_TM, fc=1024)
    split = MOE_TM // MOE_TM_DOWN
    ys = ffn_down(hid, w2, jnp.repeat(tile_expert, split), n_valid * split, MOE_TM_DOWN, tn=512)
    return moe_combine(h, gates, ys, pos[:, 0], pos[:, 1], GATHER_TM)


def dense_ffn(h, g, w1, w3, w2):
    T, D = h.shape
    tm_up, tm_down = T // 8, T // 16

    def one_group(tm):
        return jnp.zeros((T // tm,), jnp.int32), jnp.full((1,), T // tm, jnp.int32)

    xs = norm_cast(h, g, tm=512)
    hid = ffn_up(xs, w1, w3, *one_group(tm_up), tm_up, fc=512)
    return ffn_down(hid, w2, *one_group(tm_down), tm_down, tn=512, residual=h)


def kernel(x_prompt, x_sample, cache_k, cache_v, state_conv, state_ssm_re, state_ssm_im,
           norm1_g, w_in, w_out, a_w_dw, a_b_dw, a_ln_g, a_ln_b, a_w_pw, b_q_g, b_k_g,
           c_a_re, c_a_im, c_log_dt, c_b_re, c_b_im, c_c_re, c_c_im, c_d, c_w_glu, c_b_glu,
           norm2_g, ffn_w1, ffn_w3, ffn_w2, moe_w_router, moe_b_router, moe_w1, moe_w3, moe_w2):
    n_batch, L, D = x_prompt.shape
    n_b, n_new, _ = x_sample.shape
    depth = w_in.shape[0]
    c_a = a_w_pw.shape[-1]
    c_c = c_w_glu.shape[-1]
    c_b = D - c_a - c_c
    n_heads = c_b // HEAD_DIM
    window = cache_k.shape[2]
    tp, ts = n_batch * L, n_b * n_new
    T = tp + ts
    q_col0, v_col0, u_col0 = 2 * c_a, 2 * c_a + 2 * c_b, 2 * c_a + 3 * c_b
    assert q_col0 == c_b and window == DILATIONS[-1] * DIL_STEPS and L == window
    tm_big = T // 8

    p = dict(a_w_dw=a_w_dw, a_b_dw=a_b_dw, a_ln_g=a_ln_g, a_ln_b=a_ln_b, a_w_pw=a_w_pw,
             c_a_re=c_a_re, c_a_im=c_a_im, c_log_dt=c_log_dt, c_b_re=c_b_re, c_b_im=c_b_im,
             c_c_re=c_c_re, c_c_im=c_c_im, c_d=c_d, c_w_glu=c_w_glu, c_b_glu=c_b_glu)

    x = jnp.concatenate([x_prompt.reshape(tp, D),
                         x_sample.transpose(1, 0, 2).reshape(ts, D)], axis=0)
    conv_past_tm = state_conv.transpose(0, 2, 1, 3)

    w_win = jnp.asarray(_sample_weights(n_new, window, np.arange(window), n_heads))
    w_new = jnp.asarray(_sample_weights(n_new, window, window + np.arange(NEW_ROWS_PAD), n_heads))
    pad_new = lambda a: jnp.pad(a, ((0, 0), (0, NEW_ROWS_PAD - n_new), (0, 0)))

    st = {k: [] for k in ('k_p', 'v_p', 'conv_p', 're_p', 'im_p', 'k_s', 'v_s', 'conv_s', 're_s', 'im_s')}
    for l in range(depth):
        proj = norm_matmul(x, norm1_g[l], w_in, l, tm=tm_big, tn=768)
        qn, kn = qk_norm(proj, b_q_g[l], b_k_g[l], c_b, tm=tm_big)
        ya_p, conv_p = conv_prompt(proj, n_batch, L, c_a, p, l)
        ya_s, conv_s_tm = conv_sample(proj, tp, n_new, n_b, c_a, conv_past_tm, p, l)
        yb_p = attn_prompt(qn, kn, proj, n_batch, L, c_b, v_col0)
        bm = lambda a: a.reshape(n_new, n_b, c_b).transpose(1, 0, 2)
        k_new, v_new = bm(kn[tp:]), bm(proj[tp:, v_col0:v_col0 + c_b])
        q_b = bm(qn[tp:]).reshape(n_b, n_new * n_heads, HEAD_DIM)
        yb_s = attn_sample(q_b, pad_new(k_new), pad_new(v_new), cache_k, cache_v, l, w_win, w_new)
        yb_s = yb_s.reshape(n_b, n_new, c_b).transpose(1, 0, 2).reshape(ts, c_b)
        sp = s5_params(p, l)
        yc_p, re_p, im_p = s5_prompt(proj, n_batch, L, c_c, u_col0, sp, p, l)
        h0_re = state_ssm_re[l].reshape(n_b, -1)
        h0_im = state_ssm_im[l].reshape(n_b, -1)
        yc_s, re_s, im_s = s5_sample(proj, tp, n_new, n_b, c_c, u_col0, h0_re, h0_im, sp, p, l)

        h = out_proj((ya_p, yb_p, yc_p), (ya_s, yb_s, yc_s), w_out, l, x, tm=ts, tn=1024)

        j = l // 2
        if l % 2 == 0:
            x = dense_ffn(h, norm2_g[l], ffn_w1[j:j + 1], ffn_w3[j:j + 1], ffn_w2[j:j + 1])
        else:
            x = moe_ffn(h, norm2_g[l], moe_w_router[j], moe_b_router[j],
                        moe_w1[j], moe_w3[j], moe_w2[j])

        g_shape = state_ssm_re.shape[2:]
        st['k_p'].append(kn)
        st['v_p'].append(proj)
        st['conv_p'].append(conv_p)
        st['re_p'].append(re_p[:, 0].reshape((n_batch,) + g_shape))
        st['im_p'].append(im_p[:, 0].reshape((n_batch,) + g_shape))
        st['k_s'].append(k_new.reshape(n_b, n_new, n_heads, HEAD_DIM))
        st['v_s'].append(v_new.reshape(n_b, n_new, n_heads, HEAD_DIM))
        st['conv_s'].append(conv_s_tm.transpose(1, 0, 2))
        st['re_s'].append(re_s.reshape((n_b,) + g_shape))
        st['im_s'].append(im_s.reshape((n_b,) + g_shape))

    y_p = x[:tp].reshape(n_batch, L, D)
    y_s = x[tp:].reshape(n_new, n_b, D).transpose(1, 0, 2)
    stk = lambda k: jnp.stack(st[k])
    kt, vt = kv_window(st['k_p'], st['v_p'], n_batch, L, c_b, v_col0)
    window_rows = lambda a: a.reshape(depth, n_batch, n_heads, HEAD_DIM, L).transpose(0, 1, 4, 2, 3)
    return (y_p, y_s, window_rows(kt), window_rows(vt), stk('conv_p'), stk('re_p'), stk('im_p'),
            stk('k_s'), stk('v_s'), stk('conv_s'), stk('re_s'), stk('im_s'))
```

```python
import functools
import math

import numpy as np
import jax
import jax.numpy as jnp
from jax import lax
from jax.experimental import pallas as pl
from jax.experimental.pallas import tpu as pltpu

F32 = jnp.float32
BF16 = jnp.bfloat16
EPS = 1e-6
NEG = -1e30

HEAD_DIM = 64
ATTN_SCALE = HEAD_DIM ** -0.5
ATTN_BLOCK = 128
DILATIONS = (1, 4, 16)
DIL_STEPS = 128
CONV_WIDTH = 31
CONV_PAST = CONV_WIDTH - 1
SSM_GROUP = 16
SSM_STATE = 64
N_EXPERTS = 8
TOP_K = 2
LANES = 128
SUBLANES = 8
VMEM_LIMIT = 56 * 1024 * 1024


def _cparams(sem, vmem=VMEM_LIMIT):
    return pltpu.CompilerParams(dimension_semantics=sem, vmem_limit_bytes=vmem)


def _nt_dot(a, b):
    return lax.dot_general(a, b, (((1,), (1,)), ((), ())), preferred_element_type=F32)


def _sigmoid(x):
    return 1.0 / (1.0 + jnp.exp(-x))


def _silu(x):
    return x * _sigmoid(x)


def _gelu_tanh(x):
    c = math.sqrt(2.0 / math.pi)
    return 0.5 * x * (1.0 + jnp.tanh(c * (x + 0.044715 * (x * x * x))))


def _norm_mm_kernel(x_ref, g_ref, w_ref, o_ref, u_scr):
    @pl.when(pl.program_id(1) == 0)
    def _():
        x = x_ref[...]
        ms = jnp.mean(x * x, axis=-1, keepdims=True)
        u_scr[...] = (x * lax.rsqrt(ms + EPS) * g_ref[...]).astype(BF16)

    o_ref[...] = jnp.dot(u_scr[...], w_ref[...].astype(BF16), preferred_element_type=F32)


def norm_matmul(x, g, w_stack, layer, tm, tn):
    T, D = x.shape
    N = w_stack.shape[-1]
    return pl.pallas_call(
        _norm_mm_kernel,
        grid=(T // tm, N // tn),
        in_specs=[
            pl.BlockSpec((tm, D), lambda i, j: (i, 0)),
            pl.BlockSpec((1, D), lambda i, j: (0, 0)),
            pl.BlockSpec((None, D, tn), lambda i, j: (layer, 0, j)),
        ],
        out_specs=pl.BlockSpec((tm, tn), lambda i, j: (i, j)),
        out_shape=jax.ShapeDtypeStruct((T, N), F32),
        scratch_shapes=[pltpu.VMEM((tm, D), BF16)],
        compiler_params=_cparams(("parallel", "arbitrary")),
        name="norm_matmul",
    )(x, g.reshape(1, D), w_stack)


def _qknorm_kernel(q_ref, k_ref, gq_ref, gk_ref, qo_ref, ko_ref):
    tm = q_ref.shape[0]
    lane = lax.broadcasted_iota(jnp.int32, (tm, LANES), 1)
    head0 = lane < HEAD_DIM

    def norm(x, g):
        sq = x * x
        s0 = jnp.sum(jnp.where(head0, sq, 0.0), axis=-1, keepdims=True)
        s1 = jnp.sum(jnp.where(head0, 0.0, sq), axis=-1, keepdims=True)
        ms = jnp.where(head0, s0, s1) * (1.0 / HEAD_DIM)
        return x * lax.rsqrt(ms + EPS) * g

    for t in range(q_ref.shape[1] // LANES):
        sl = slice(t * LANES, (t + 1) * LANES)
        qo_ref[:, sl] = norm(q_ref[:, sl], gq_ref[...]) * ATTN_SCALE
        ko_ref[:, sl] = norm(k_ref[:, sl], gk_ref[...])


def qk_norm(proj, gq, gk, c_b, tm):
    T = proj.shape[0]
    gq2 = jnp.concatenate([gq, gq]).reshape(1, LANES)
    gk2 = jnp.concatenate([gk, gk]).reshape(1, LANES)
    qblk = 1
    return pl.pallas_call(
        _qknorm_kernel,
        grid=(T // tm,),
        in_specs=[
            pl.BlockSpec((tm, c_b), lambda i: (i, qblk)),
            pl.BlockSpec((tm, c_b), lambda i: (i, qblk + 1)),
            pl.BlockSpec((1, LANES), lambda i: (0, 0)),
            pl.BlockSpec((1, LANES), lambda i: (0, 0)),
        ],
        out_specs=[pl.BlockSpec((tm, c_b), lambda i: (i, 0)),
                   pl.BlockSpec((tm, c_b), lambda i: (i, 0))],
        out_shape=[jax.ShapeDtypeStruct((T, c_b), F32)] * 2,
        compiler_params=_cparams(("parallel",)),
        name="qk_norm",
    )(proj, proj, gq2, gk2)


def _attn_prompt_kernel(q_ref, k_ref, v_ref, o_ref,
                        q0_s, q1_s, k_s, v_s,
                        m1, l1, a1, m4, l4, a4, m16, l16, a16):
    L = q_ref.shape[0]
    B = ATTN_BLOCK
    head0 = lax.broadcasted_iota(jnp.int32, (B, LANES), 1) < HEAD_DIM
    qi = lax.broadcasted_iota(jnp.int32, (2 * B, B), 0) % B
    ki = lax.broadcasted_iota(jnp.int32, (2 * B, B), 1)
    tri_cur = ki <= qi
    tri_prev = ki >= qi

    stats = {1: (m1, l1, a1), 4: (m4, l4, a4), 16: (m16, l16, a16)}

    for d in DILATIONS:
        n = L // d
        nb = n // B
        m_s, l_s, a_s = stats[d]
        for r in range(d):
            src = pl.ds(r, n, stride=d) if d > 1 else slice(None)
            dst = slice(r * n, (r + 1) * n)
            hm = lax.broadcasted_iota(jnp.int32, (n, LANES), 1) < HEAD_DIM
            q = q_ref[src, :]
            q0_s[dst, :] = jnp.where(hm, q, 0.0).astype(BF16)
            q1_s[dst, :] = jnp.where(hm, 0.0, q).astype(BF16)
            k_s[dst, :] = k_ref[src, :].astype(BF16)
            v_s[dst, :] = v_ref[src, :].astype(BF16)

        def block(j, carry, nb=nb, m_s=m_s, l_s=l_s, a_s=a_s):
            cur = pl.ds(pl.multiple_of(j * B, B), B)
            qb = jnp.concatenate([q0_s[cur, :], q1_s[cur, :]], axis=0)
            s_c = jnp.where(tri_cur, _nt_dot(qb, k_s[cur, :]), NEG)
            if nb > 1:
                prev = pl.ds(pl.multiple_of(jnp.maximum(j - 1, 0) * B, B), B)
                mask_prev = jnp.logical_and(tri_prev, (j % nb) > 0)
                s_p = jnp.where(mask_prev, _nt_dot(qb, k_s[prev, :]), NEG)
                m = jnp.max(jnp.maximum(s_c, s_p), axis=-1, keepdims=True)
                p_c, p_p = jnp.exp(s_c - m), jnp.exp(s_p - m)
                den = jnp.sum(p_c + p_p, axis=-1, keepdims=True)
                pv = (jnp.dot(p_c.astype(BF16), v_s[cur, :], preferred_element_type=F32)
                      + jnp.dot(p_p.astype(BF16), v_s[prev, :], preferred_element_type=F32))
            else:
                m = jnp.max(s_c, axis=-1, keepdims=True)
                p_c = jnp.exp(s_c - m)
                den = jnp.sum(p_c, axis=-1, keepdims=True)
                pv = jnp.dot(p_c.astype(BF16), v_s[cur, :], preferred_element_type=F32)
            m_s[cur, :] = jnp.where(head0, m[:B], m[B:])
            l_s[cur, :] = jnp.where(head0, den[:B], den[B:])
            a_s[cur, :] = jnp.where(head0, pv[:B], pv[B:])
            return carry

        lax.fori_loop(0, L // B, block, 0, unroll=16)

    dmax = DILATIONS[-1]
    nrow = L // dmax
    for r in range(dmax):
        o1 = pl.ds(r, nrow, stride=dmax)
        o4 = pl.ds((r % 4) * (L // 4) + r // 4, nrow, stride=dmax // 4)
        o16 = slice(r * nrow, (r + 1) * nrow)
        mm1, mm4, mm16 = m1[o1, :], m4[o4, :], m16[o16, :]
        mx = jnp.maximum(jnp.maximum(mm1, mm4), mm16)
        w1, w4, w16 = jnp.exp(mm1 - mx), jnp.exp(mm4 - mx), jnp.exp(mm16 - mx)
        num = w1 * a1[o1, :] + w4 * a4[o4, :] + w16 * a16[o16, :]
        den = w1 * l1[o1, :] + w4 * l4[o4, :] + w16 * l16[o16, :]
        o_ref[o1, :] = num / den


def attn_prompt(qn, kn, proj, n_batch, L, c_b, v_col0):
    n_hp = c_b // LANES
    vblk = v_col0 // LANES
    stat = [pltpu.VMEM((L, LANES), F32)] * 9
    return pl.pallas_call(
        _attn_prompt_kernel,
        grid=(n_batch, n_hp),
        in_specs=[
            pl.BlockSpec((L, LANES), lambda b, h: (b, h)),
            pl.BlockSpec((L, LANES), lambda b, h: (b, h)),
            pl.BlockSpec((L, LANES), lambda b, h: (b, vblk + h)),
        ],
        out_specs=pl.BlockSpec((L, LANES), lambda b, h: (b, h)),
        out_shape=jax.ShapeDtypeStruct((n_batch * L, c_b), F32),
        scratch_shapes=[pltpu.VMEM((L, LANES), BF16)] * 4 + stat,
        compiler_params=_cparams(("parallel", "parallel")),
        name="attn_prompt",
    )(qn, kn, proj)


def _kv_window_kernel(*refs, depth):
    k_refs, v_refs = refs[:depth], refs[depth:2 * depth]
    kt_ref, vt_ref = refs[2 * depth:]
    layer = pl.program_id(0)
    for l in range(depth):
        @pl.when(layer == l)
        def _(l=l):
            kt_ref[...] = k_refs[l][...].T
            vt_ref[...] = v_refs[l][...].T


def kv_window(kns, projs, n_batch, L, c_b, v_col0):
    depth = len(kns)
    n_hp = c_b // LANES
    vblk = v_col0 // LANES

    def spec(l, col0):
        def imap(layer, b, h):
            return jnp.where(layer == l, b, 0), col0 + jnp.where(layer == l, h, 0)
        return pl.BlockSpec((L, LANES), imap)

    out_spec = pl.BlockSpec((None, None, LANES, L), lambda layer, b, h: (layer, b, h, 0))
    out = jax.ShapeDtypeStruct((depth, n_batch, c_b, L), F32)
    return pl.pallas_call(
        functools.partial(_kv_window_kernel, depth=depth),
        grid=(depth, n_batch, n_hp),
        in_specs=[spec(l, 0) for l in range(depth)] + [spec(l, vblk) for l in range(depth)],
        out_specs=[out_spec, out_spec],
        out_shape=[out, out],
        compiler_params=_cparams(("arbitrary", "arbitrary", "arbitrary")),
        name="kv_window",
    )(*kns, *projs)


def _branch_multiplicity(dist):
    c = np.zeros(dist.shape, np.float32)
    for d in DILATIONS:
        c += ((dist >= 0) & (dist % d == 0) & (dist <= d * DIL_STEPS)).astype(np.float32)
    return c


NEW_ROWS_PAD = 16


def _sample_weights(n_new, past_len, pos, n_heads):
    s = np.repeat(np.arange(n_new), n_heads)[:, None]
    return _branch_multiplicity(past_len + s - np.asarray(pos)[None, :]).astype(np.float32)


def _attn_sample_kernel(q_ref, kt_ref, vt_ref, kn_ref, vn_ref, w_ref, wn_ref, o_ref):
    n_heads, hd, window = kt_ref.shape
    width = n_heads * hd
    q = q_ref[...]
    nq = q.shape[0]
    q2 = jnp.concatenate([q, q], axis=-1)
    qt = jnp.concatenate([q2] * (width // (2 * hd)), axis=-1)
    own_head = (lax.broadcasted_iota(jnp.int32, (nq, width), 0) % n_heads
                == lax.broadcasted_iota(jnp.int32, (nq, width), 1) // hd)
    qbd = jnp.where(own_head, qt, 0.0).astype(BF16)

    w, wn = w_ref[...], wn_ref[...]
    s_c = jnp.dot(qbd, kt_ref[...].reshape(width, window).astype(BF16),
                  preferred_element_type=F32)
    s_n = _nt_dot(qbd, kn_ref[...].astype(BF16))
    s_c = jnp.where(w > 0.0, s_c, NEG)
    s_n = jnp.where(wn > 0.0, s_n, NEG)
    m = jnp.maximum(jnp.max(s_c, axis=-1, keepdims=True), jnp.max(s_n, axis=-1, keepdims=True))
    p_c = w * jnp.exp(s_c - m)
    p_n = wn * jnp.exp(s_n - m)
    den = jnp.sum(p_c, axis=-1, keepdims=True) + jnp.sum(p_n, axis=-1, keepdims=True)
    full = (_nt_dot(p_c.astype(BF16), vt_ref[...].reshape(width, window).astype(BF16))
            + jnp.dot(p_n.astype(BF16), vn_ref[...].astype(BF16), preferred_element_type=F32))
    full = jnp.where(own_head, full, 0.0)
    acc = full[:, 0:LANES]
    for t in range(1, width // LANES):
        acc = acc + full[:, t * LANES:(t + 1) * LANES]
    o_ref[...] = (acc[:, :hd] + acc[:, hd:]) / den


def attn_sample(q_b, k_new, v_new, cache_k, cache_v, layer, w, wn):
    n_b, nq, _ = q_b.shape
    depth, _, window, n_heads, _ = cache_k.shape
    width = n_heads * HEAD_DIM
    kt = cache_k.transpose(0, 1, 3, 4, 2)
    vt = cache_v.transpose(0, 1, 3, 4, 2)
    cache_spec = pl.BlockSpec((None, None, n_heads, HEAD_DIM, window), lambda b: (layer, b, 0, 0, 0))
    new_spec = pl.BlockSpec((None, NEW_ROWS_PAD, width), lambda b: (b, 0, 0))
    q_spec = pl.BlockSpec((None, nq, HEAD_DIM), lambda b: (b, 0, 0))

    def const_spec(a):
        return pl.BlockSpec(a.shape, lambda b: (0, 0))

    return pl.pallas_call(
        _attn_sample_kernel,
        grid=(n_b,),
        in_specs=[q_spec, cache_spec, cache_spec, new_spec, new_spec, const_spec(w), const_spec(wn)],
        out_specs=q_spec,
        out_shape=jax.ShapeDtypeStruct((n_b, nq, HEAD_DIM), F32),
        compiler_params=_cparams(("parallel",)),
        name="attn_sample",
    )(q_b, kt, vt, k_new, v_new, w, wn)


def _layernorm_silu(y, g, b):
    mu = jnp.mean(y, axis=-1, keepdims=True)
    yc = y - mu
    var = jnp.mean(yc * yc, axis=-1, keepdims=True)
    return _silu(yc * lax.rsqrt(var + EPS) * g + b)


CONV_HALO = 32
CONV_ROWS = 32


def _conv_prompt_kernel(val_ref, gate_ref, wdw_ref, bdw_ref, lng_ref, lnb_ref, wpw_ref,
                        y_ref, st_ref, xp_s, sh_s, y_s):
    c = pl.program_id(1)
    lc, ca = val_ref.shape
    off = CONV_HALO - CONV_PAST

    @pl.when(c == 0)
    def _():
        xp_s[0:CONV_HALO, :] = jnp.zeros((CONV_HALO, ca), F32)

    @pl.when(c > 0)
    def _():
        xp_s[0:CONV_HALO, :] = xp_s[lc:lc + CONV_HALO, :]

    xp_s[CONV_HALO:CONV_HALO + lc, :] = val_ref[...] * _sigmoid(gate_ref[...])

    n_sh = sh_s.shape[1]
    for b in range(1, SUBLANES):
        for lb in range(ca // LANES):
            ls = slice(lb * LANES, (lb + 1) * LANES)
            sh_s[b - 1, :, ls] = xp_s[b:b + n_sh, ls]

    for rb in range(lc // CONV_ROWS):
        for lb in range(ca // LANES):
            ls = slice(lb * LANES, (lb + 1) * LANES)
            acc = jnp.zeros((CONV_ROWS, LANES), F32)
            for j in range(CONV_WIDTH):
                a, b = divmod(off + j, SUBLANES)
                r0 = rb * CONV_ROWS + a * SUBLANES
                rows = xp_s[r0:r0 + CONV_ROWS, ls] if b == 0 else sh_s[b - 1, r0:r0 + CONV_ROWS, ls]
                acc = acc + wdw_ref[j:j + 1, ls] * rows
            y_s[rb * CONV_ROWS:(rb + 1) * CONV_ROWS, ls] = acc

    z = _layernorm_silu(y_s[...] + bdw_ref[...], lng_ref[...], lnb_ref[...])
    y_ref[...] = jnp.dot(z.astype(BF16), wpw_ref[...].astype(BF16), preferred_element_type=F32)

    @pl.when(c == pl.num_programs(1) - 1)
    def _():
        st_ref[...] = xp_s[lc + off:lc + CONV_HALO, :]


def _layer_vec(a, layer):
    return a[layer].reshape(1, -1)


def conv_prompt(proj, n_batch, L, c_a, p, layer, lc=256):
    nch = L // lc
    vec = pl.BlockSpec((1, c_a), lambda b, c: (0, 0))
    return pl.pallas_call(
        _conv_prompt_kernel,
        grid=(n_batch, nch),
        in_specs=[
            pl.BlockSpec((lc, c_a), lambda b, c: (b * nch + c, 0)),
            pl.BlockSpec((lc, c_a), lambda b, c: (b * nch + c, 1)),
            pl.BlockSpec((None, CONV_WIDTH, c_a), lambda b, c: (layer, 0, 0)),
            vec, vec, vec,
            pl.BlockSpec((None, c_a, c_a), lambda b, c: (layer, 0, 0)),
        ],
        out_specs=[pl.BlockSpec((lc, c_a), lambda b, c: (b * nch + c, 0)),
                   pl.BlockSpec((None, CONV_PAST, c_a), lambda b, c: (b, 0, 0))],
        out_shape=[jax.ShapeDtypeStruct((n_batch * L, c_a), F32),
                   jax.ShapeDtypeStruct((n_batch, CONV_PAST, c_a), F32)],
        scratch_shapes=[pltpu.VMEM((lc + CONV_HALO, c_a), F32),
                        pltpu.VMEM((SUBLANES - 1, lc + CONV_HALO - SUBLANES, c_a), F32),
                        pltpu.VMEM((lc, c_a), F32)],
        compiler_params=_cparams(("parallel", "arbitrary")),
        name="conv_prompt",
    )(proj, proj, p['a_w_dw'], _layer_vec(p['a_b_dw'], layer), _layer_vec(p['a_ln_g'], layer),
      _layer_vec(p['a_ln_b'], layer), p['a_w_pw'])


def _conv_sample_kernel(val_ref, gate_ref, past_ref, wdw_ref, bdw_ref, lng_ref, lnb_ref, wpw_ref,
                        y_ref, st_ref, y_s):
    n_b = past_ref.shape[1]
    n_new = val_ref.shape[0] // n_b
    g = val_ref[...] * _sigmoid(gate_ref[...])

    def xp(t):
        if t < CONV_PAST:
            return past_ref[t]
        return g[(t - CONV_PAST) * n_b:(t - CONV_PAST + 1) * n_b, :]

    for s in range(n_new):
        acc = jnp.zeros_like(xp(0))
        for j in range(CONV_WIDTH):
            acc = acc + wdw_ref[j:j + 1, :] * xp(s + j)
        y_s[s * n_b:(s + 1) * n_b, :] = acc
    z = _layernorm_silu(y_s[...] + bdw_ref[...], lng_ref[...], lnb_ref[...])
    y_ref[...] = jnp.dot(z.astype(BF16), wpw_ref[...].astype(BF16), preferred_element_type=F32)
    for t in range(CONV_PAST):
        st_ref[t] = xp(t + n_new)


def conv_sample(proj, row0, n_new, n_b, c_a, past_tm, p, layer):
    ts = n_new * n_b
    vec = pl.BlockSpec((1, c_a), lambda i: (0, 0))
    return pl.pallas_call(
        _conv_sample_kernel,
        grid=(1,),
        in_specs=[
            pl.BlockSpec((ts, c_a), lambda i: (row0 // ts, 0)),
            pl.BlockSpec((ts, c_a), lambda i: (row0 // ts, 1)),
            pl.BlockSpec((None, CONV_PAST, n_b, c_a), lambda i: (layer, 0, 0, 0)),
            pl.BlockSpec((None, CONV_WIDTH, c_a), lambda i: (layer, 0, 0)),
            vec, vec, vec,
            pl.BlockSpec((None, c_a, c_a), lambda i: (layer, 0, 0)),
        ],
        out_specs=[pl.BlockSpec((ts, c_a), lambda i: (0, 0)),
                   pl.BlockSpec((CONV_PAST, n_b, c_a), lambda i: (0, 0, 0))],
        out_shape=[jax.ShapeDtypeStruct((ts, c_a), F32),
                   jax.ShapeDtypeStruct((CONV_PAST, n_b, c_a), F32)],
        scratch_shapes=[pltpu.VMEM((ts, c_a), F32)],
        compiler_params=_cparams(("arbitrary",)),
        name="conv_sample",
    )(proj, proj, past_tm, p['a_w_dw'], _layer_vec(p['a_b_dw'], layer),
      _layer_vec(p['a_ln_g'], layer), _layer_vec(p['a_ln_b'], layer), p['a_w_pw'])


def _s5_param_kernel(are_ref, aim_ref, ldt_ref, bre_ref, bim_ref,
                     pre_ref, pim_ref, bbre_ref, bbim_ref):
    a_re, a_im = are_ref[...], aim_ref[...]
    dt = jnp.exp(ldt_ref[...])
    mag = jnp.exp(a_re * dt)
    ab_re, ab_im = mag * jnp.cos(a_im * dt), mag * jnp.sin(a_im * dt)
    nr, ni = ab_re - 1.0, ab_im
    inv = 1.0 / (a_re * a_re + a_im * a_im)
    f_re = (nr * a_re + ni * a_im) * inv
    f_im = (ni * a_re - nr * a_im) * inv
    b_re, b_im = bre_ref[...], bim_ref[...]
    bbre_ref[...] = f_re * b_re - f_im * b_im
    bbim_ref[...] = f_re * b_im + f_im * b_re
    pr, pi = ab_re, ab_im
    pre_ref[0] = pr
    pim_ref[0] = pi
    for k in range(1, SUBLANES):
        pr, pi = pr * ab_re - pi * ab_im, pr * ab_im + pi * ab_re
        pre_ref[k] = pr
        pim_ref[k] = pi


def s5_params(p, layer):
    a_re, a_im = p['c_a_re'][layer], p['c_a_im'][layer]
    G, N = a_re.shape
    C = SSM_GROUP
    b_re_t = p['c_b_re'][layer].transpose(0, 2, 1)
    b_im_t = p['c_b_im'][layer].transpose(0, 2, 1)
    pre, pim, bbre, bbim = pl.pallas_call(
        _s5_param_kernel,
        out_shape=[jax.ShapeDtypeStruct((SUBLANES, G, 1, N), F32)] * 2
        + [jax.ShapeDtypeStruct((G, C, N), F32)] * 2,
        name="s5_params",
    )(a_re.reshape(G, 1, N), a_im.reshape(G, 1, N), p['c_log_dt'][layer].reshape(G, 1, 1),
      b_re_t, b_im_t)
    S = G * N
    pre, pim = pre.reshape(SUBLANES, S), pim.reshape(SUBLANES, S)
    t = np.arange(SUBLANES)[:, None]
    tabs = []
    for k in (1, 2, 4):
        keep = jnp.asarray(t >= k)
        tabs += [jnp.where(keep, pre[k - 1][None, :], 0.0), jnp.where(keep, pim[k - 1][None, :], 0.0)]
    tabs += [pre, pim]
    tab = jnp.stack(tabs)
    eye = jnp.eye(G, dtype=F32)

    def in_proj(bb):
        return (eye[:, None, :, None] * bb[:, :, None, :]).reshape(G * C, S).astype(BF16)

    def out_proj(c):
        return (eye[:, None, :, None] * c.transpose(0, 2, 1)[:, :, None, :]).reshape(S, G * C)

    c_cat = jnp.concatenate([out_proj(p['c_c_re'][layer]), -out_proj(p['c_c_im'][layer])]).astype(BF16)
    return dict(tab=tab, bb_re=in_proj(bbre), bb_im=in_proj(bbim), c_cat=c_cat,
                ab_re=pre[0:1], ab_im=pim[0:1])


def _s5_readout(u, xr, xi, ccat_ref, d_ref, wglu_ref, bglu_ref):
    S = xr.shape[-1]
    y = (jnp.dot(xr.astype(BF16), ccat_ref[0:S, :], preferred_element_type=F32)
         + jnp.dot(xi.astype(BF16), ccat_ref[S:2 * S, :], preferred_element_type=F32)
         + d_ref[...] * u)
    z = _gelu_tanh(y)
    gate = jnp.dot(z.astype(BF16), wglu_ref[...].astype(BF16), preferred_element_type=F32)
    return z * _sigmoid(gate + bglu_ref[...])


S5_LANE_GROUP = 512


def _s5_prompt_kernel(u_ref, bbre_ref, bbim_ref, tab_ref, ccat_ref, d_ref, wglu_ref, bglu_ref,
                      y_ref, hre_ref, him_ref, xr_s, xi_s, h_s):
    c = pl.program_id(1)
    tc = u_ref.shape[0]
    S = xr_s.shape[1]

    @pl.when(c == 0)
    def _():
        h_s[...] = jnp.zeros_like(h_s)

    u = u_ref[...]
    ub = u.astype(BF16)
    xr_s[...] = jnp.dot(ub, bbre_ref[...], preferred_element_type=F32)
    xi_s[...] = jnp.dot(ub, bbim_ref[...], preferred_element_type=F32)

    for jg in range(S // S5_LANE_GROUP):
        ls = slice(jg * S5_LANE_GROUP, (jg + 1) * S5_LANE_GROUP)
        levels = [(k, tab_ref[2 * i, :, ls], tab_ref[2 * i + 1, :, ls])
                  for i, k in enumerate((1, 2, 4))]
        cr, ci = tab_ref[6, :, ls], tab_ref[7, :, ls]

        def tile(i, carry, ls=ls, levels=levels, cr=cr, ci=ci):
            hr, hi = carry
            rows = pl.ds(pl.multiple_of(i * SUBLANES, SUBLANES), SUBLANES)
            br, bi = xr_s[rows, ls], xi_s[rows, ls]
            for k, pr, pi in levels:
                sr, si = pltpu.roll(br, k, axis=0), pltpu.roll(bi, k, axis=0)
                br, bi = br + pr * sr - pi * si, bi + pr * si + pi * sr
            br, bi = br + cr * hr - ci * hi, bi + cr * hi + ci * hr
            xr_s[rows, ls] = br
            xi_s[rows, ls] = bi
            last = slice(SUBLANES - 1, SUBLANES)
            return (jnp.broadcast_to(br[last, :], br.shape), jnp.broadcast_to(bi[last, :], bi.shape))

        hr, hi = lax.fori_loop(0, tc // SUBLANES, tile, (h_s[0, :, ls], h_s[1, :, ls]))
        h_s[0, :, ls] = hr
        h_s[1, :, ls] = hi

    y_ref[...] = _s5_readout(u, xr_s[...], xi_s[...], ccat_ref, d_ref, wglu_ref, bglu_ref)
    hre_ref[...] = h_s[0]
    him_ref[...] = h_s[1]


def s5_prompt(proj, n_batch, L, c_c, u_col0, sp, p, layer, tc=256):
    nch = L // tc
    S = sp['tab'].shape[-1]
    ublk = u_col0 // c_c

    def full(a):
        return pl.BlockSpec(a.shape, lambda b, c: (0,) * a.ndim)

    d = p['c_d'][layer].reshape(1, c_c)
    bglu = p['c_b_glu'][layer].reshape(1, c_c)
    return pl.pallas_call(
        _s5_prompt_kernel,
        grid=(n_batch, nch),
        in_specs=[
            pl.BlockSpec((tc, c_c), lambda b, c: (b * nch + c, ublk)),
            full(sp['bb_re']), full(sp['bb_im']), full(sp['tab']), full(sp['c_cat']), full(d),
            pl.BlockSpec((None, c_c, c_c), lambda b, c: (layer, 0, 0)),
            full(bglu),
        ],
        out_specs=[pl.BlockSpec((tc, c_c), lambda b, c: (b * nch + c, 0)),
                   pl.BlockSpec((None, SUBLANES, S), lambda b, c: (b, 0, 0)),
                   pl.BlockSpec((None, SUBLANES, S), lambda b, c: (b, 0, 0))],
        out_shape=[jax.ShapeDtypeStruct((n_batch * L, c_c), F32),
                   jax.ShapeDtypeStruct((n_batch, SUBLANES, S), F32),
                   jax.ShapeDtypeStruct((n_batch, SUBLANES, S), F32)],
        scratch_shapes=[pltpu.VMEM((tc, S), F32), pltpu.VMEM((tc, S), F32),
                        pltpu.VMEM((2, SUBLANES, S), F32)],
        compiler_params=_cparams(("parallel", "arbitrary")),
        name="s5_prompt",
    )(proj, sp['bb_re'], sp['bb_im'], sp['tab'], sp['c_cat'], d, p['c_w_glu'], bglu)


def _s5_sample_kernel(u_ref, h0re_ref, h0im_ref, bbre_ref, bbim_ref, abre_ref, abim_ref,
                      ccat_ref, d_ref, wglu_ref, bglu_ref,
                      y_ref, hre_ref, him_ref, xr_s, xi_s):
    n_b = h0re_ref.shape[0]
    n_new = u_ref.shape[0] // n_b
    u = u_ref[...]
    ub = u.astype(BF16)
    xr_s[...] = jnp.dot(ub, bbre_ref[...], preferred_element_type=F32)
    xi_s[...] = jnp.dot(ub, bbim_ref[...], preferred_element_type=F32)
    ar, ai = abre_ref[...], abim_ref[...]
    hr, hi = h0re_ref[...], h0im_ref[...]
    for s in range(n_new):
        rows = slice(s * n_b, (s + 1) * n_b)
        hr, hi = ar * hr - ai * hi + xr_s[rows, :], ar * hi + ai * hr + xi_s[rows, :]
        xr_s[rows, :] = hr
        xi_s[rows, :] = hi
    y_ref[...] = _s5_readout(u, xr_s[...], xi_s[...], ccat_ref, d_ref, wglu_ref, bglu_ref)
    hre_ref[...] = hr
    him_ref[...] = hi


def s5_sample(proj, row0, n_new, n_b, c_c, u_col0, h0_re, h0_im, sp, p, layer):
    ts = n_new * n_b
    S = sp['tab'].shape[-1]

    def full(a):
        return pl.BlockSpec(a.shape, lambda i: (0,) * a.ndim)

    d = p['c_d'][layer].reshape(1, c_c)
    bglu = p['c_b_glu'][layer].reshape(1, c_c)
    return pl.pallas_call(
        _s5_sample_kernel,
        grid=(1,),
        in_specs=[
            pl.BlockSpec((ts, c_c), lambda i: (row0 // ts, u_col0 // c_c)),
            full(h0_re), full(h0_im), full(sp['bb_re']), full(sp['bb_im']),
            full(sp['ab_re']), full(sp['ab_im']), full(sp['c_cat']), full(d),
            pl.BlockSpec((None, c_c, c_c), lambda i: (layer, 0, 0)),
            full(bglu),
        ],
        out_specs=[pl.BlockSpec((ts, c_c), lambda i: (0, 0)),
                   pl.BlockSpec((n_b, S), lambda i: (0, 0)),
                   pl.BlockSpec((n_b, S), lambda i: (0, 0))],
        out_shape=[jax.ShapeDtypeStruct((ts, c_c), F32),
                   jax.ShapeDtypeStruct((n_b, S), F32),
                   jax.ShapeDtypeStruct((n_b, S), F32)],
        scratch_shapes=[pltpu.VMEM((ts, S), F32), pltpu.VMEM((ts, S), F32)],
        compiler_params=_cparams(("arbitrary",)),
        name="s5_sample",
    )(proj, h0_re, h0_im, sp['bb_re'], sp['bb_im'], sp['ab_re'], sp['ab_im'], sp['c_cat'], d,
      p['c_w_glu'], bglu)


def _out_proj_kernel(yap_ref, ybp_ref, ycp_ref, yas_ref, ybs_ref, ycs_ref, w_ref, x_ref,
                     o_ref, cat_s, w_s, *, n_prompt_tiles):
    i = pl.program_id(1)

    @pl.when(i == 0)
    def _():
        w_s[...] = w_ref[...].astype(BF16)

    def fill(ya_ref, yb_ref, yc_ref):
        ca, cb = ya_ref.shape[1], yb_ref.shape[1]
        cat_s[:, 0:ca] = ya_ref[...].astype(BF16)
        cat_s[:, ca:ca + cb] = yb_ref[...].astype(BF16)
        cat_s[:, ca + cb:] = yc_ref[...].astype(BF16)

    @pl.when(i < n_prompt_tiles)
    def _():
        fill(yap_ref, ybp_ref, ycp_ref)

    @pl.when(i >= n_prompt_tiles)
    def _():
        fill(yas_ref, ybs_ref, ycs_ref)

    o_ref[...] = x_ref[...] + jnp.dot(cat_s[...], w_s[...], preferred_element_type=F32)


def out_proj(prompt_parts, sample_parts, w_stack, layer, x, tm, tn):
    T, D = x.shape
    dm = w_stack.shape[1]
    n_p = prompt_parts[0].shape[0] // tm
    assert prompt_parts[0].shape[0] % tm == 0 and sample_parts[0].shape[0] % tm == 0

    def prompt_spec(a):
        return pl.BlockSpec((tm, a.shape[1]), lambda j, i: (jnp.minimum(i, n_p - 1), 0))

    def sample_spec(a):
        return pl.BlockSpec((tm, a.shape[1]), lambda j, i: (jnp.maximum(i - n_p, 0), 0))

    return pl.pallas_call(
        functools.partial(_out_proj_kernel, n_prompt_tiles=n_p),
        grid=(D // tn, T // tm),
        in_specs=[prompt_spec(a) for a in prompt_parts] + [sample_spec(a) for a in sample_parts]
        + [pl.BlockSpec((None, dm, tn), lambda j, i: (layer, 0, j)),
           pl.BlockSpec((tm, tn), lambda j, i: (i, j))],
        out_specs=pl.BlockSpec((tm, tn), lambda j, i: (i, j)),
        out_shape=jax.ShapeDtypeStruct((T, D), F32),
        scratch_shapes=[pltpu.VMEM((tm, dm), BF16), pltpu.VMEM((dm, tn), BF16)],
        compiler_params=_cparams(("arbitrary", "arbitrary")),
        name="out_proj",
    )(*prompt_parts, *sample_parts, w_stack, x)


def _norm_cast_kernel(x_ref, g_ref, o_ref):
    x = x_ref[...]
    ms = jnp.mean(x * x, axis=-1, keepdims=True)
    o_ref[...] = (x * lax.rsqrt(ms + EPS) * g_ref[...]).astype(BF16)


def norm_cast(x, g, tm):
    T, D = x.shape
    return pl.pallas_call(
        _norm_cast_kernel,
        grid=(T // tm,),
        in_specs=[pl.BlockSpec((tm, D), lambda i: (i, 0)), pl.BlockSpec((1, D), lambda i: (0, 0))],
        out_specs=pl.BlockSpec((tm, D), lambda i: (i, 0)),
        out_shape=jax.ShapeDtypeStruct((T, D), BF16),
        compiler_params=_cparams(("parallel",)),
        name="norm_cast",
    )(x, g.reshape(1, D))


def _new_expert(te_ref, i):
    prev = te_ref[jnp.maximum(i - 1, 0)]
    return jnp.logical_or(i == 0, te_ref[i] != prev)


def _ffn_up_kernel(te_ref, nv_ref, x_ref, w1_ref, w3_ref, h_ref, w1_s, w3_s):
    i = pl.program_id(1)

    @pl.when(_new_expert(te_ref, i))
    def _():
        w1_s[...] = w1_ref[...].astype(BF16)
        w3_s[...] = w3_ref[...].astype(BF16)

    @pl.when(i < nv_ref[0])
    def _():
        x = x_ref[...]
        a = jnp.dot(x, w1_s[...], preferred_element_type=F32)
        b = jnp.dot(x, w3_s[...], preferred_element_type=F32)
        h_ref[...] = (_silu(a) * b).astype(BF16)

    @pl.when(i >= nv_ref[0])
    def _():
        h_ref[...] = jnp.zeros_like(h_ref)


def ffn_up(xs, w1, w3, tile_expert, n_valid, tm, fc):
    R, D = xs.shape
    F = w1.shape[-1]
    wspec = pl.BlockSpec((None, D, fc), lambda j, i, te, nv: (te[i], 0, j))
    return pl.pallas_call(
        _ffn_up_kernel,
        grid_spec=pltpu.PrefetchScalarGridSpec(
            num_scalar_prefetch=2,
            grid=(pl.cdiv(F, fc), R // tm),
            in_specs=[pl.BlockSpec((tm, D), lambda j, i, te, nv: (i, 0)), wspec, wspec],
            out_specs=pl.BlockSpec((tm, fc), lambda j, i, te, nv: (i, j)),
            scratch_shapes=[pltpu.VMEM((D, fc), BF16)] * 2,
        ),
        out_shape=jax.ShapeDtypeStruct((R, F), BF16),
        compiler_params=_cparams(("arbitrary", "arbitrary")),
        name="ffn_up",
    )(tile_expert, n_valid, xs, w1, w3)


def _ffn_down_kernel(te_ref, nv_ref, h_ref, w2_ref, *rest, residual):
    if residual:
        r_ref, o_ref, w2_s = rest
    else:
        o_ref, w2_s = rest
    i = pl.program_id(1)

    @pl.when(_new_expert(te_ref, i))
    def _():
        w2_s[...] = w2_ref[...].astype(BF16)

    @pl.when(i < nv_ref[0])
    def _():
        y = jnp.dot(h_ref[...], w2_s[...], preferred_element_type=F32)
        o_ref[...] = r_ref[...] + y if residual else y

    @pl.when(i >= nv_ref[0])
    def _():
        o_ref[...] = jnp.zeros_like(o_ref)


def ffn_down(h, w2, tile_expert, n_valid, tm, tn, residual=None):
    R, F = h.shape
    D = w2.shape[-1]
    tile = pl.BlockSpec((tm, tn), lambda j, i, te, nv: (i, j))
    in_specs = [pl.BlockSpec((tm, F), lambda j, i, te, nv: (i, 0)),
                pl.BlockSpec((None, F, tn), lambda j, i, te, nv: (te[i], 0, j))]
    args = [h, w2]
    if residual is not None:
        in_specs.append(tile)
        args.append(residual)
    return pl.pallas_call(
        functools.partial(_ffn_down_kernel, residual=residual is not None),
        grid_spec=pltpu.PrefetchScalarGridSpec(
            num_scalar_prefetch=2,
            grid=(D // tn, R // tm),
            in_specs=in_specs,
            out_specs=tile,
            scratch_shapes=[pltpu.VMEM((F, tn), BF16)],
        ),
        out_shape=jax.ShapeDtypeStruct((R, D), F32),
        compiler_params=_cparams(("arbitrary", "arbitrary")),
        name="ffn_down",
    )(tile_expert, n_valid, *args)


def _router_kernel(x_ref, g_ref, wr_ref, br_ref, idx_ref, gate_ref):
    x = x_ref[...]
    ms = jnp.mean(x * x, axis=-1, keepdims=True)
    u = x * lax.rsqrt(ms + EPS) * g_ref[...]
    logits = lax.dot_general(wr_ref[...], u, (((1,), (1,)), ((), ())),
                             precision=lax.Precision.HIGHEST,
                             preferred_element_type=F32) + br_ref[...]
    n_e = logits.shape[0]
    eid = lax.broadcasted_iota(jnp.int32, logits.shape, 0)
    m1 = jnp.max(logits, axis=0, keepdims=True)
    i1 = jnp.min(jnp.where(logits == m1, eid, n_e), axis=0, keepdims=True)
    rest = jnp.where(eid == i1, -jnp.inf, logits)
    m2 = jnp.max(rest, axis=0, keepdims=True)
    i2 = jnp.min(jnp.where(rest == m2, eid, n_e), axis=0, keepdims=True)
    e2 = jnp.exp(m2 - m1)
    g1 = 1.0 / (1.0 + e2)
    idx_ref[...] = jnp.where(eid == 0, i1, i2)
    gate_ref[...] = jnp.where(eid == 0, g1, e2 * g1)


def router(x, g, w_router, b_router, tm):
    T, D = x.shape
    E = w_router.shape[-1]
    return pl.pallas_call(
        _router_kernel,
        grid=(T // tm,),
        in_specs=[pl.BlockSpec((tm, D), lambda i: (i, 0)),
                  pl.BlockSpec((1, D), lambda i: (0, 0)),
                  pl.BlockSpec((E, D), lambda i: (0, 0)),
                  pl.BlockSpec((E, 1), lambda i: (0, 0))],
        out_specs=[pl.BlockSpec((E, tm), lambda i: (0, i)),
                   pl.BlockSpec((E, tm), lambda i: (0, i))],
        out_shape=[jax.ShapeDtypeStruct((E, T), jnp.int32), jax.ShapeDtypeStruct((E, T), F32)],
        compiler_params=_cparams(("parallel",)),
        name="moe_router",
    )(x, g.reshape(1, D), w_router.T, b_router.reshape(E, 1))


def _row_copy(src_hbm, row, dst, r, sem):
    return pltpu.make_async_copy(src_hbm.at[pl.ds(row, 1), :], dst.at[pl.ds(r, 1), :], sem)


def _wait_rows(src_hbm, dst, sem):
    pltpu.make_async_copy(src_hbm.at[pl.ds(0, dst.shape[0]), :], dst, sem).wait()


GATHER_UNROLL = 8


def _rows_loop(tm, fn):
    def body(it, c):
        for u in range(GATHER_UNROLL):
            fn(it * GATHER_UNROLL + u, u % 2)
        return c

    lax.fori_loop(0, tm // GATHER_UNROLL, body, 0)


def _gather_norm_kernel(src_ref, nv_ref, x_hbm, g_ref, o_ref, buf, sem):
    tm = buf.shape[1]
    i = pl.program_id(0)
    n_used = nv_ref[0]

    def fetch(tile, slot):
        _rows_loop(tm, lambda r, pri: _row_copy(x_hbm, src_ref[tile * tm + r], buf.at[slot], r,
                                                sem.at[slot]).start(priority=pri))

    @pl.when(i == 0)
    def _():
        fetch(0, 0)

    @pl.when(i + 1 < n_used)
    def _():
        fetch(i + 1, (i + 1) % 2)

    @pl.when(i < n_used)
    def _():
        slot = i % 2
        _wait_rows(x_hbm, buf.at[slot], sem.at[slot])
        x = buf[slot]
        ms = jnp.mean(x * x, axis=-1, keepdims=True)
        o_ref[...] = (x * lax.rsqrt(ms + EPS) * g_ref[...]).astype(BF16)

    @pl.when(i >= n_used)
    def _():
        o_ref[...] = jnp.zeros_like(o_ref)


def gather_norm(x, g, src, n_used, tm):
    T, D = x.shape
    R = src.shape[0]
    return pl.pallas_call(
        _gather_norm_kernel,
        grid_spec=pltpu.PrefetchScalarGridSpec(
            num_scalar_prefetch=2,
            grid=(R // tm,),
            in_specs=[pl.BlockSpec(memory_space=pl.ANY),
                      pl.BlockSpec((1, D), lambda i, s, n: (0, 0))],
            out_specs=pl.BlockSpec((tm, D), lambda i, s, n: (i, 0)),
            scratch_shapes=[pltpu.VMEM((2, tm, D), F32), pltpu.SemaphoreType.DMA((2,))],
        ),
        out_shape=jax.ShapeDtypeStruct((R, D), BF16),
        compiler_params=_cparams(("arbitrary",)),
        name="moe_gather",
    )(src, n_used, x, g.reshape(1, D))


def _combine_kernel(p0_ref, p1_ref, h_ref, gate_ref, y_hbm, o_ref, buf, sem):
    tm = buf.shape[2]
    i = pl.program_id(0)

    def fetch(tile, slot):
        def one(r, pri):
            _row_copy(y_hbm, p0_ref[tile * tm + r], buf.at[slot, 0], r,
                      sem.at[slot, 0]).start(priority=pri)
            _row_copy(y_hbm, p1_ref[tile * tm + r], buf.at[slot, 1], r,
                      sem.at[slot, 1]).start(priority=1 - pri)

        _rows_loop(tm, one)

    @pl.when(i == 0)
    def _():
        fetch(0, 0)

    @pl.when(i + 1 < pl.num_programs(0))
    def _():
        fetch(i + 1, (i + 1) % 2)

    slot = i % 2
    _wait_rows(y_hbm, buf.at[slot, 0], sem.at[slot, 0])
    _wait_rows(y_hbm, buf.at[slot, 1], sem.at[slot, 1])
    gate = gate_ref[...]
    o_ref[...] = h_ref[...] + gate[:, 0:1] * buf[slot, 0] + gate[:, 1:2] * buf[slot, 1]


def moe_combine(h, gates, ys, pos0, pos1, tm):
    T, D = h.shape
    return pl.pallas_call(
        _combine_kernel,
        grid_spec=pltpu.PrefetchScalarGridSpec(
            num_scalar_prefetch=2,
            grid=(T // tm,),
            in_specs=[pl.BlockSpec((tm, D), lambda i, a, b: (i, 0)),
                      pl.BlockSpec((tm, TOP_K), lambda i, a, b: (i, 0)),
                      pl.BlockSpec(memory_space=pl.ANY)],
            out_specs=pl.BlockSpec((tm, D), lambda i, a, b: (i, 0)),
            scratch_shapes=[pltpu.VMEM((2, TOP_K, tm, D), F32), pltpu.SemaphoreType.DMA((2, TOP_K))],
        ),
        out_shape=jax.ShapeDtypeStruct((T, D), F32),
        compiler_params=_cparams(("arbitrary",)),
        name="moe_combine",
    )(pos0, pos1, h, gates, ys)


MOE_TM = 512
MOE_TM_DOWN = 512
GATHER_TM = 256


def moe_layout(idx, tm):
    T, K = idx.shape
    flat = idx.reshape(-1)
    onehot = (flat[:, None] == jnp.arange(N_EXPERTS)[None, :]).astype(jnp.int32)
    rank = jnp.take_along_axis(jnp.cumsum(onehot, axis=0) - onehot, flat[:, None], axis=1)[:, 0]
    counts = jnp.sum(onehot, axis=0)
    tiles = (counts + tm - 1) // tm
    tile_end = jnp.cumsum(tiles)
    start = (tile_end - tiles) * tm
    pos = start[flat] + rank
    n_tiles = (T * K) // tm + N_EXPERTS
    src = jnp.zeros((n_tiles * tm,), jnp.int32).at[pos].set(jnp.arange(T * K, dtype=jnp.int32) // K)
    owner = jnp.sum((jnp.arange(n_tiles)[:, None] >= tile_end[None, :]).astype(jnp.int32), axis=1)
    tile_expert = jnp.minimum(owner, N_EXPERTS - 1).astype(jnp.int32)
    n_valid = tile_end[-1:].astype(jnp.int32)
    return pos.reshape(T, K).astype(jnp.int32), src, tile_expert, n_valid


def moe_ffn(h, g, w_router, b_router, w1, w3, w2):
    T, D = h.shape
    idx_t, gate_t = router(h, g, w_router, b_router, tm=512)
    idx = idx_t[:TOP_K].T
    gates = gate_t[:TOP_K].T
    pos, src, tile_expert, n_valid = moe_layout(idx, MOE_TM)
    xs = gather_norm(h, g, src, n_valid * (MOE_TM // GATHER_TM), GATHER_TM)
    hid = ffn_up(xs, w1, w3, tile_expert, n_valid, MOE_TM, fc=1024)
    split = MOE_TM // MOE_TM_DOWN
    ys = ffn_down(hid, w2, jnp.repeat(tile_expert, split), n_valid * split, MOE_TM_DOWN, tn=512)
    return moe_combine(h, gates, ys, pos[:, 0], pos[:, 1], GATHER_TM)


def dense_ffn(h, g, w1, w3, w2):
    T, D = h.shape
    tm_up, tm_down = T // 8, T // 16

    def one_group(tm):
        return jnp.zeros((T // tm,), jnp.int32), jnp.full((1,), T // tm, jnp.int32)

    xs = norm_cast(h, g, tm=512)
    hid = ffn_up(xs, w1, w3, *one_group(tm_up), tm_up, fc=512)
    return ffn_down(hid, w2, *one_group(tm_down), tm_down, tn=512, residual=h)


def kernel(x_prompt, x_sample, cache_k, cache_v, state_conv, state_ssm_re, state_ssm_im,
           norm1_g, w_in, w_out, a_w_dw, a_b_dw, a_ln_g, a_ln_b, a_w_pw, b_q_g, b_k_g,
           c_a_re, c_a_im, c_log_dt, c_b_re, c_b_im, c_c_re, c_c_im, c_d, c_w_glu, c_b_glu,
           norm2_g, ffn_w1, ffn_w3, ffn_w2, moe_w_router, moe_b_router, moe_w1, moe_w3, moe_w2):
    n_batch, L, D = x_prompt.shape
    n_b, n_new, _ = x_sample.shape
    depth = w_in.shape[0]
    c_a = a_w_pw.shape[-1]
    c_c = c_w_glu.shape[-1]
    c_b = D - c_a - c_c
    n_heads = c_b // HEAD_DIM
    window = cache_k.shape[2]
    tp, ts = n_batch * L, n_b * n_new
    T = tp + ts
    q_col0, v_col0, u_col0 = 2 * c_a, 2 * c_a + 2 * c_b, 2 * c_a + 3 * c_b
    assert q_col0 == c_b and window == DILATIONS[-1] * DIL_STEPS and L == window
    tm_big = T // 8

    p = dict(a_w_dw=a_w_dw, a_b_dw=a_b_dw, a_ln_g=a_ln_g, a_ln_b=a_ln_b, a_w_pw=a_w_pw,
             c_a_re=c_a_re, c_a_im=c_a_im, c_log_dt=c_log_dt, c_b_re=c_b_re, c_b_im=c_b_im,
             c_c_re=c_c_re, c_c_im=c_c_im, c_d=c_d, c_w_glu=c_w_glu, c_b_glu=c_b_glu)

    x = jnp.concatenate([x_prompt.reshape(tp, D),
                         x_sample.transpose(1, 0, 2).reshape(ts, D)], axis=0)
    conv_past_tm = state_conv.transpose(0, 2, 1, 3)

    w_win = jnp.asarray(_sample_weights(n_new, window, np.arange(window), n_heads))
    w_new = jnp.asarray(_sample_weights(n_new, window, window + np.arange(NEW_ROWS_PAD), n_heads))
    pad_new = lambda a: jnp.pad(a, ((0, 0), (0, NEW_ROWS_PAD - n_new), (0, 0)))

    st = {k: [] for k in ('k_p', 'v_p', 'conv_p', 're_p', 'im_p', 'k_s', 'v_s', 'conv_s', 're_s', 'im_s')}
    for l in range(depth):
        proj = norm_matmul(x, norm1_g[l], w_in, l, tm=tm_big, tn=768)
        qn, kn = qk_norm(proj, b_q_g[l], b_k_g[l], c_b, tm=tm_big)
        ya_p, conv_p = conv_prompt(proj, n_batch, L, c_a, p, l)
        ya_s, conv_s_tm = conv_sample(proj, tp, n_new, n_b, c_a, conv_past_tm, p, l)
        yb_p = attn_prompt(qn, kn, proj, n_batch, L, c_b, v_col0)
        bm = lambda a: a.reshape(n_new, n_b, c_b).transpose(1, 0, 2)
        k_new, v_new = bm(kn[tp:]), bm(proj[tp:, v_col0:v_col0 + c_b])
        q_b = bm(qn[tp:]).reshape(n_b, n_new * n_heads, HEAD_DIM)
        yb_s = attn_sample(q_b, pad_new(k_new), pad_new(v_new), cache_k, cache_v, l, w_win, w_new)
        yb_s = yb_s.reshape(n_b, n_new, c_b).transpose(1, 0, 2).reshape(ts, c_b)
        sp = s5_params(p, l)
        yc_p, re_p, im_p = s5_prompt(proj, n_batch, L, c_c, u_col0, sp, p, l)
        h0_re = state_ssm_re[l].reshape(n_b, -1)
        h0_im = state_ssm_im[l].reshape(n_b, -1)
        yc_s, re_s, im_s = s5_sample(proj, tp, n_new, n_b, c_c, u_col0, h0_re, h0_im, sp, p, l)

        h = out_proj((ya_p, yb_p, yc_p), (ya_s, yb_s, yc_s), w_out, l, x, tm=ts, tn=1024)

        j = l // 2
        if l % 2 == 0:
            x = dense_ffn(h, norm2_g[l], ffn_w1[j:j + 1], ffn_w3[j:j + 1], ffn_w2[j:j + 1])
        else:
            x = moe_ffn(h, norm2_g[l], moe_w_router[j], moe_b_router[j],
                        moe_w1[j], moe_w3[j], moe_w2[j])

        g_shape = state_ssm_re.shape[2:]
        st['k_p'].append(kn)
        st['v_p'].append(proj)
        st['conv_p'].append(conv_p)
        st['re_p'].append(re_p[:, 0].reshape((n_batch,) + g_shape))
        st['im_p'].append(im_p[:, 0].reshape((n_batch,) + g_shape))
        st['k_s'].append(k_new.reshape(n_b, n_new, n_heads, HEAD_DIM))
        st['v_s'].append(v_new.reshape(n_b, n_new, n_heads, HEAD_DIM))
        st['conv_s'].append(conv_s_tm.transpose(1, 0, 2))
        st['re_s'].append(re_s.reshape((n_b,) + g_shape))
        st['im_s'].append(im_s.reshape((n_b,) + g_shape))

    y_p = x[:tp].reshape(n_batch, L, D)
    y_s = x[tp:].reshape(n_new, n_b, D).transpose(1, 0, 2)
    stk = lambda k: jnp.stack(st[k])
    kt, vt = kv_window(st['k_p'], st['v_p'], n_batch, L, c_b, v_col0)
    window_rows = lambda a: a.reshape(depth, n_batch, n_heads, HEAD_DIM, L).transpose(0, 1, 4, 2, 3)
    return (y_p, y_s, window_rows(kt), window_rows(vt), stk('conv_p'), stk('re_p'), stk('im_p'),
            stk('k_s'), stk('v_s'), stk('conv_s'), stk('re_s'), stk('im_s'))
```
